```python
import math
import jax, jax.numpy as jnp
from jax import lax
import numpy as np

D_MODEL = 2048
BATCH = 4
SEQ = 8192
DEPTH = 1

N_ATT_HEADS = 8
ATT_HEAD_DIM = 128
KV_LATENT = 256
IDX_HEADS = 16
IDX_DIM = 64
IDX_TOPK_MAX = 256
Q_BLOCK = 128
N_RET_HEADS = 4
RET_QK_DIM = 128
RET_V_DIM = 256
RET_CHUNK = 128
ROPE_BASE = 10000.0
N_BUCKETS = 32
MAX_DISTANCE = 128
N_MEM = 256
N_CROSS_HEADS = 4
CROSS_HEAD_DIM = 128
N_EXPERTS = 32
TOP_K = 4
D_FF = D_MODEL
SWIGLU_LIMIT = 7.0
SWIGLU_ALPHA = 1.702
MOE_BLOCK = 512
EPS = 1e-6

ATT_WIDTH = N_ATT_HEADS * ATT_HEAD_DIM
RET_WIDTH = N_RET_HEADS * RET_V_DIM
MIX_WIDTH = ATT_WIDTH + RET_WIDTH
SPLITS = (ATT_WIDTH,
          KV_LATENT,
          IDX_HEADS * IDX_DIM,
          IDX_DIM,
          IDX_HEADS,
          N_RET_HEADS * RET_QK_DIM,
          N_RET_HEADS * RET_QK_DIM,
          RET_WIDTH,
          RET_WIDTH)
IN_WIDTH = sum(SPLITS)
SPLIT_POINTS = [int(v) for v in np.cumsum(SPLITS)[:-1]]

kernel_name = "hybrid_dsa_retention_moe_layer"


def rmsnorm(x, g):
    xf = x.astype(jnp.float32)
    y = xf * lax.rsqrt(jnp.mean(xf * xf, axis=-1, keepdims=True) + EPS)
    return (y * g.astype(jnp.float32)).astype(x.dtype)


def t5_bucket(dist):
    n = jnp.maximum(dist, 0)
    max_exact = N_BUCKETS // 2
    nf = jnp.maximum(n, 1).astype(jnp.float32)
    large = max_exact + (jnp.log(nf / max_exact) / math.log(MAX_DISTANCE / max_exact)
                         * (N_BUCKETS - max_exact)).astype(jnp.int32)
    large = jnp.minimum(large, N_BUCKETS - 1)
    return jnp.where(n < max_exact, n, large)


def rotary(x, positions):
    d = x.shape[-1]
    inv = ROPE_BASE ** (-jnp.arange(0, d, 2, dtype=jnp.float32) / d)
    ang = positions.astype(jnp.float32)[..., None] * inv
    cos = jnp.cos(ang)[:, :, None, :]
    sin = jnp.sin(ang)[:, :, None, :]
    xf = x.astype(jnp.float32)
    x1, x2 = xf[..., : d // 2], xf[..., d // 2:]
    return jnp.concatenate([x1 * cos - x2 * sin, x1 * sin + x2 * cos], axis=-1).astype(x.dtype)


def dsa_attention(q, c_kv, q_idx, k_idx, w_idx, positions, w_uk, w_uv, rel_bias):
    B, S = q.shape[0], q.shape[1]
    topk = min(IDX_TOPK_MAX, S // 4)
    nb = S // Q_BLOCK
    q_lat = jnp.einsum('bshd,chd->bshc', q, w_uk) * (ATT_HEAD_DIM ** -0.5)
    w_idx = w_idx * (IDX_HEADS ** -0.5 * IDX_DIM ** -0.5)
    key_pos = jnp.arange(S)

    def blockify(a):
        return a.reshape((B, nb, Q_BLOCK) + a.shape[2:]).swapaxes(0, 1)

    def one_block(args):
        blk, ql, qi, wi, pq = args
        t = blk * Q_BLOCK + jnp.arange(Q_BLOCK)
        causal = key_pos[None, :] <= t[:, None]
        logits = jnp.einsum('bqhd,bsd->bqhs', qi, k_idx)
        score = jnp.einsum('bqh,bqhs->bqs', wi, jax.nn.relu(logits)).astype(jnp.float32)
        score = jnp.where(causal[None], score, -jnp.inf)
        _, sel = lax.top_k(score, topk)
        valid = sel <= t[None, :, None]
        c_sel = jax.vmap(lambda c, i: c[i])(c_kv, sel)
        p_sel = jax.vmap(lambda p, i: p[i])(positions, sel)
        bias = rel_bias[t5_bucket(pq[:, :, None] - p_sel)]
        s = (jnp.einsum('bqhc,bqkc->bqhk', ql, c_sel).astype(jnp.float32)
             + jnp.moveaxis(bias, -1, 2).astype(jnp.float32))
        s = jnp.where(valid[:, :, None, :], s, -jnp.inf)
        p = jax.nn.softmax(s, axis=-1).astype(c_sel.dtype)
        return jnp.einsum('bqhk,bqkc->bqhc', p, c_sel)

    o_lat = lax.map(one_block, (jnp.arange(nb), blockify(q_lat), blockify(q_idx),
                                blockify(w_idx), blockify(positions)))
    o_lat = o_lat.swapaxes(0, 1).reshape(B, S, N_ATT_HEADS, KV_LATENT)
    return jnp.einsum('bshc,chd->bshd', o_lat, w_uv).reshape(B, S, ATT_WIDTH)


def retention(q, k, v, g):
    B, S, H, dk = q.shape
    dv = v.shape[-1]
    n = S // RET_CHUNK
    log_gamma = jnp.log(1.0 - 2.0 ** (-5.0 - jnp.arange(H, dtype=jnp.float32)))
    j = jnp.arange(RET_CHUNK, dtype=jnp.float32)
    diff = j[:, None] - j[None, :]
    inner_decay = jnp.where(diff >= 0, jnp.exp(log_gamma[:, None, None] * jnp.maximum(diff, 0.0)), 0.0)
    cross_decay = jnp.exp(log_gamma[:, None] * (j + 1.0))
    state_decay = jnp.exp(log_gamma[:, None] * (RET_CHUNK - 1.0 - j))
    chunk_decay = jnp.exp(log_gamma * RET_CHUNK)
    k = k * (dk ** -0.5)

    def chunks(a):
        return a.reshape(B, n, RET_CHUNK, H, a.shape[-1]).transpose(1, 0, 3, 2, 4).astype(jnp.float32)

    def step(R, inp):
        qc, kc, vc = inp
        inner = jnp.einsum('bhid,bhjd->bhij', qc, kc) * inner_decay
        o = (jnp.einsum('bhij,bhje->bhie', inner, vc)
             + jnp.einsum('bhid,bhde->bhie', qc, R) * cross_decay[None, :, :, None])
        R = (R * chunk_decay[None, :, None, None]
             + jnp.einsum('bhjd,bhje->bhde', kc * state_decay[None, :, :, None], vc))
        return R, o

    R0 = jnp.zeros((B, H, dk, dv), jnp.float32)
    _, o = lax.scan(step, R0, (chunks(q), chunks(k), chunks(v)))
    o = o.transpose(1, 0, 3, 2, 4).reshape(B, S, H, dv)
    o = o * lax.rsqrt(jnp.mean(o * o, axis=-1, keepdims=True) + EPS)
    return (jax.nn.silu(g.astype(jnp.float32)) * o.reshape(B, S, H * dv)).astype(q.dtype)


def memory_cross_attention(hn, memn, wq, wk, wv, wo):
    B, S, _ = hn.shape
    M = memn.shape[1]
    q = (hn @ wq).reshape(B, S, N_CROSS_HEADS, CROSS_HEAD_DIM)
    k = (memn @ wk).reshape(B, M, N_CROSS_HEADS, CROSS_HEAD_DIM)
    v = (memn @ wv).reshape(B, M, N_CROSS_HEADS, CROSS_HEAD_DIM)
    s = jnp.einsum('bqhd,bmhd->bhqm', q, k).astype(jnp.float32) * (CROSS_HEAD_DIM ** -0.5)
    p = jax.nn.softmax(s, axis=-1).astype(v.dtype)
    o = jnp.einsum('bhqm,bmhd->bqhd', p, v).reshape(B, S, N_CROSS_HEADS * CROSS_HEAD_DIM)
    return o @ wo


def moe(xn, w_router, b_router, w_gu, b_gu, w_down, b_down):
    B, S, D = xn.shape
    xt = xn.reshape(-1, D)
    T = xt.shape[0]
    logits = (xt @ w_router).astype(jnp.float32) + b_router.astype(jnp.float32)
    top_val, top_idx = lax.top_k(logits, TOP_K)
    gates = jax.nn.softmax(top_val, axis=-1)
    A = T * TOP_K
    flat_e = top_idx.reshape(-1)
    flat_tok = jnp.arange(A, dtype=jnp.int32) // TOP_K
    flat_g = gates.reshape(-1)
    order = jnp.argsort(flat_e)
    sorted_e = flat_e[order]
    counts = jnp.zeros((N_EXPERTS,), jnp.int32).at[flat_e].add(1)
    start = jnp.cumsum(counts) - counts
    padded = (counts + MOE_BLOCK - 1) // MOE_BLOCK * MOE_BLOCK
    padded_end = jnp.cumsum(padded)
    padded_start = padded_end - padded
    dest = padded_start[sorted_e] + jnp.arange(A, dtype=jnp.int32) - start[sorted_e]
    n_blocks = -(-A // MOE_BLOCK) + N_EXPERTS
    P = n_blocks * MOE_BLOCK
    slot_tok = jnp.zeros((P,), jnp.int32).at[dest].set(flat_tok[order])
    slot_gate = jnp.zeros((P,), jnp.float32).at[dest].set(flat_g[order])
    block_start = jnp.arange(n_blocks, dtype=jnp.int32) * MOE_BLOCK
    block_expert = jnp.minimum(jnp.searchsorted(padded_end, block_start, side='right'), N_EXPERTS - 1)

    def block_step(acc, inp):
        e, tok, gate = inp
        xb = xt[tok]
        gu = xb @ w_gu[e] + b_gu[e]
        gt = jnp.minimum(gu[:, :D_FF], SWIGLU_LIMIT)
        up = jnp.clip(gu[:, D_FF:], -SWIGLU_LIMIT, SWIGLU_LIMIT)
        hmid = (up + 1.0) * (gt * jax.nn.sigmoid(SWIGLU_ALPHA * gt))
        y = (hmid @ w_down[e] + b_down[e]).astype(jnp.float32)
        return acc.at[tok].add(y * gate[:, None]), None

    acc0 = jnp.zeros((T, D), jnp.float32)
    out, _ = lax.scan(block_step, acc0, (block_expert, slot_tok.reshape(n_blocks, MOE_BLOCK),
                                         slot_gate.reshape(n_blocks, MOE_BLOCK)))
    return out.astype(xn.dtype).reshape(B, S, D)


def setup_inputs(seed: int = 0) -> dict:
    key = jax.random.key(seed)
    ks = jax.random.split(key, 32)
    L = DEPTH

    def nrm(k, shape, scale):
        return jax.random.normal(k, shape, jnp.float32) * scale

    def gain(k, shape):
        return 1.0 + 0.02 * jax.random.normal(k, shape, jnp.float32)

    return {
        "x": nrm(ks[0], (BATCH, SEQ, D_MODEL), 1.0),
        "mem": nrm(ks[1], (BATCH, N_MEM, D_MODEL), 1.0),
        "positions": jnp.broadcast_to(jnp.arange(SEQ, dtype=jnp.int32), (BATCH, SEQ)),
        "norm_mix": gain(ks[2], (L, D_MODEL)),
        "w_in": nrm(ks[3], (L, D_MODEL, IN_WIDTH), D_MODEL ** -0.5),
        "kv_norm": gain(ks[4], (L, KV_LATENT)),
        "idx_k_norm": gain(ks[5], (L, IDX_DIM)),
        "w_uk": nrm(ks[6], (L, KV_LATENT, N_ATT_HEADS, ATT_HEAD_DIM), KV_LATENT ** -0.5),
        "w_uv": nrm(ks[7], (L, KV_LATENT, N_ATT_HEADS, ATT_HEAD_DIM), KV_LATENT ** -0.5),
        "rel_bias": nrm(ks[8], (N_BUCKETS, N_ATT_HEADS), 0.5),
        "w_out": nrm(ks[9], (L, MIX_WIDTH, D_MODEL), MIX_WIDTH ** -0.5),
        "norm_cross": gain(ks[10], (L, D_MODEL)),
        "norm_mem": gain(ks[11], (L, D_MODEL)),
        "w_cq": nrm(ks[12], (L, D_MODEL, N_CROSS_HEADS * CROSS_HEAD_DIM), D_MODEL ** -0.5),
        "w_ck": nrm(ks[13], (L, D_MODEL, N_CROSS_HEADS * CROSS_HEAD_DIM), D_MODEL ** -0.5),
        "w_cv": nrm(ks[14], (L, D_MODEL, N_CROSS_HEADS * CROSS_HEAD_DIM), D_MODEL ** -0.5),
        "w_co": nrm(ks[15], (L, N_CROSS_HEADS * CROSS_HEAD_DIM, D_MODEL), (N_CROSS_HEADS * CROSS_HEAD_DIM) ** -0.5),
        "norm_ffn": gain(ks[16], (L, D_MODEL)),
        "w_router": nrm(ks[17], (L, D_MODEL, N_EXPERTS), D_MODEL ** -0.5),
        "b_router": nrm(ks[18], (L, N_EXPERTS), 0.01),
        "w_gate_up": nrm(ks[19], (L, N_EXPERTS, D_MODEL, 2 * D_FF), D_MODEL ** -0.5),
        "b_gate_up": nrm(ks[20], (L, N_EXPERTS, 2 * D_FF), 0.01),
        "w_down": nrm(ks[21], (L, N_EXPERTS, D_FF, D_MODEL), D_FF ** -0.5),
        "b_down": nrm(ks[22], (L, N_EXPERTS, D_MODEL), 0.01),
        "norm_final": gain(ks[23], (D_MODEL,)),
    }


def reference(x, mem, positions, norm_mix, w_in, kv_norm, idx_k_norm, w_uk, w_uv, rel_bias,
              w_out, norm_cross, norm_mem, w_cq, w_ck, w_cv, w_co, norm_ffn, w_router,
              b_router, w_gate_up, b_gate_up, w_down, b_down, norm_final):
    B, S, _ = x.shape
    h = x
    for l in range(DEPTH):
        hn = rmsnorm(h, norm_mix[l])
        proj = hn @ w_in[l]
        q_att, c_kv, q_idx, k_idx, w_idx, q_ret, k_ret, v_ret, g_ret = jnp.split(proj, SPLIT_POINTS, axis=-1)
        c_kv = rmsnorm(c_kv, kv_norm[l])
        k_idx = rmsnorm(k_idx, idx_k_norm[l])
        y_att = dsa_attention(q_att.reshape(B, S, N_ATT_HEADS, ATT_HEAD_DIM), c_kv,
                              q_idx.reshape(B, S, IDX_HEADS, IDX_DIM), k_idx, w_idx,
                              positions, w_uk[l], w_uv[l], rel_bias)
        q_r = rotary(q_ret.reshape(B, S, N_RET_HEADS, RET_QK_DIM), positions)
        k_r = rotary(k_ret.reshape(B, S, N_RET_HEADS, RET_QK_DIM), positions)
        y_ret = retention(q_r, k_r, v_ret.reshape(B, S, N_RET_HEADS, RET_V_DIM), g_ret)
        h = h + jnp.concatenate([y_att, y_ret], axis=-1) @ w_out[l]
        h = h + memory_cross_attention(rmsnorm(h, norm_cross[l]), rmsnorm(mem, norm_mem[l]),
                                       w_cq[l], w_ck[l], w_cv[l], w_co[l])
        h = h + moe(rmsnorm(h, norm_ffn[l]), w_router[l], b_router[l], w_gate_up[l],
                    b_gate_up[l], w_down[l], b_down[l])
    return rmsnorm(h, norm_final)
```

```python
import functools
import math

import jax
import jax.numpy as jnp
import numpy as np
from jax import lax
from jax.experimental import pallas as pl
from jax.experimental.pallas import tpu as pltpu

F32 = jnp.float32
BF16 = jnp.bfloat16

D_MODEL = 2048
N_ATT_HEADS = 8
ATT_HEAD_DIM = 128
KV_LATENT = 256
IDX_HEADS = 16
IDX_DIM = 64
IDX_TOPK_MAX = 256
N_RET_HEADS = 4
RET_QK_DIM = 128
RET_V_DIM = 256
ROPE_BASE = 10000.0
N_BUCKETS = 32
MAX_DISTANCE = 128
N_CROSS_HEADS = 4
CROSS_HEAD_DIM = 128
N_EXPERTS = 32
TOP_K = 4
D_FF = D_MODEL
SWIGLU_LIMIT = 7.0
SWIGLU_ALPHA = 1.702
EPS = 1e-6

ATT_WIDTH = N_ATT_HEADS * ATT_HEAD_DIM
RET_WIDTH = N_RET_HEADS * RET_V_DIM
RET_QK_WIDTH = N_RET_HEADS * RET_QK_DIM
IDX_WIDTH = IDX_HEADS * IDX_DIM
CROSS_WIDTH = N_CROSS_HEADS * CROSS_HEAD_DIM

LANES = 128
VMEM_LIMIT = 56 * 1024 * 1024

COL_Q_ATT = 0
COL_Q_IDX = COL_Q_ATT + ATT_WIDTH
COL_V_RET = COL_Q_IDX + IDX_WIDTH
COL_G_RET = COL_V_RET + RET_WIDTH
COL_Q_RET = COL_G_RET + RET_WIDTH
COL_K_RET = COL_Q_RET + RET_QK_WIDTH
COL_C_KV = COL_K_RET + RET_QK_WIDTH
COL_KW = COL_C_KV + KV_LATENT
PROJ_TN = 512
PROJ_WIDTH = 5632

Q_BLOCK = 128
KEY_TILE = 512
KEY_PAD = KEY_TILE - Q_BLOCK
NEG_BIG = -1e30

RET_CHUNK = 256

MOE_ROWS = 512
MOE_TF = 1024


def _cparams(sem, vmem=VMEM_LIMIT):
    return pltpu.CompilerParams(dimension_semantics=sem, vmem_limit_bytes=vmem)


def _rms(x, g):
    return x * lax.rsqrt(jnp.mean(x * x, axis=-1, keepdims=True) + EPS) * g


def _proj_body(x_ref, g_ref, w_ref, o_ref, xn_ref):
    @pl.when(pl.program_id(1) == 0)
    def _():
        xn_ref[...] = _rms(x_ref[...], g_ref[...]).astype(BF16)

    o_ref[...] = jnp.dot(xn_ref[...], w_ref[...], preferred_element_type=F32).astype(o_ref.dtype)


def _proj(x2, gain, w_p, tm):
    t, d = x2.shape
    n = w_p.shape[1]
    return pl.pallas_call(
        _proj_body,
        out_shape=jax.ShapeDtypeStruct((t, n), BF16),
        grid=(t // tm, n // PROJ_TN),
        in_specs=[
            pl.BlockSpec((tm, d), lambda i, j: (i, 0)),
            pl.BlockSpec((1, d), lambda i, j: (0, 0)),
            pl.BlockSpec((d, PROJ_TN), lambda i, j: (0, j)),
        ],
        out_specs=pl.BlockSpec((tm, PROJ_TN), lambda i, j: (i, j)),
        scratch_shapes=[pltpu.VMEM((tm, d), BF16)],
        compiler_params=_cparams(("arbitrary", "arbitrary")),
        name="proj",
    )(x2, gain, w_p)


def _prep_body(c_ref, kw_ref, q_ref, k_ref, cos_ref, sin_ref, gkv_ref, gk_ref,
               c_out, kidx_out, widx_out, q_out, k_out):
    c_out[...] = _rms(c_ref[...].astype(F32), gkv_ref[...]).astype(BF16)

    kw = kw_ref[...].astype(F32)
    lane = lax.broadcasted_iota(jnp.int32, kw.shape, 1)
    is_k = lane < IDX_DIM
    kk = jnp.where(is_k, kw, 0.0)
    ms = jnp.sum(kk * kk, axis=-1, keepdims=True) * (1.0 / IDX_DIM)
    kidx_out[...] = (kk * lax.rsqrt(ms + EPS) * gk_ref[...]).astype(BF16)
    widx_out[...] = kw * (IDX_HEADS ** -0.5 * IDX_DIM ** -0.5)

    cos2 = cos_ref[...]
    sin2 = sin_ref[...]
    for h in range(N_RET_HEADS):
        sl = slice(h * RET_QK_DIM, (h + 1) * RET_QK_DIM)
        qh = q_ref[:, sl].astype(F32)
        kh = k_ref[:, sl].astype(F32)
        q_out[:, sl] = (qh * cos2 + pltpu.roll(qh, RET_QK_DIM // 2, 1) * sin2).astype(BF16)
        k_out[:, sl] = ((kh * cos2 + pltpu.roll(kh, RET_QK_DIM // 2, 1) * sin2)
                        * (RET_QK_DIM ** -0.5)).astype(BF16)


def _prep(proj, cos2, sin2, g_kv, g_k, tm):
    t = proj.shape[0]
    row = lambda i: (i, 0)
    return pl.pallas_call(
        _prep_body,
        out_shape=(
            jax.ShapeDtypeStruct((t, KV_LATENT), BF16),
            jax.ShapeDtypeStruct((t, LANES), BF16),
            jax.ShapeDtypeStruct((t, LANES), F32),
            jax.ShapeDtypeStruct((t, RET_QK_WIDTH), BF16),
            jax.ShapeDtypeStruct((t, RET_QK_WIDTH), BF16),
        ),
        grid=(t // tm,),
        in_specs=[
            pl.BlockSpec((tm, KV_LATENT), lambda i: (i, COL_C_KV // KV_LATENT)),
            pl.BlockSpec((tm, LANES), lambda i: (i, COL_KW // LANES)),
            pl.BlockSpec((tm, RET_QK_WIDTH), lambda i: (i, COL_Q_RET // RET_QK_WIDTH)),
            pl.BlockSpec((tm, RET_QK_WIDTH), lambda i: (i, COL_K_RET // RET_QK_WIDTH)),
            pl.BlockSpec((tm, LANES), row),
            pl.BlockSpec((tm, LANES), row),
            pl.BlockSpec((1, KV_LATENT), lambda i: (0, 0)),
            pl.BlockSpec((1, LANES), lambda i: (0, 0)),
        ],
        out_specs=(
            pl.BlockSpec((tm, KV_LATENT), row),
            pl.BlockSpec((tm, LANES), row),
            pl.BlockSpec((tm, LANES), row),
            pl.BlockSpec((tm, RET_QK_WIDTH), row),
            pl.BlockSpec((tm, RET_QK_WIDTH), row),
        ),
        compiler_params=_cparams(("arbitrary",)),
        name="prep",
    )(proj, proj, proj, proj, cos2, sin2, g_kv, g_k)


def _ret_body(q_ref, kt_ref, v_ref, g_ref, dec_ref, cross_ref, state_ref, cd_ref, o_ref, r_ref):
    @pl.when(pl.program_id(1) == 0)
    def _():
        r_ref[...] = jnp.zeros_like(r_ref)

    for h in range(N_RET_HEADS):
        qs = slice(h * RET_QK_DIM, (h + 1) * RET_QK_DIM)
        vs = slice(h * RET_V_DIM, (h + 1) * RET_V_DIM)
        q = q_ref[0, :, qs]
        kt = kt_ref[0, qs, :]
        v = v_ref[:, vs]
        r_old = r_ref[h]
        inner = jnp.dot(q, kt, preferred_element_type=F32) * dec_ref[h]
        qc = (q.astype(F32) * cross_ref[h]).astype(BF16)
        o = (jnp.dot(inner.astype(BF16), v, preferred_element_type=F32)
             + jnp.dot(qc, r_old.astype(BF16), preferred_element_type=F32))
        ks = (kt.astype(F32) * state_ref[h]).astype(BF16)
        r_ref[h] = r_old * cd_ref[h] + jnp.dot(ks, v, preferred_element_type=F32)
        o = o * lax.rsqrt(jnp.mean(o * o, axis=-1, keepdims=True) + EPS)
        g = g_ref[:, vs].astype(F32)
        o_ref[:, vs] = (g * jax.nn.sigmoid(g) * o).astype(o_ref.dtype)


def _retention(q_rot, kt_rot, proj, b, s):
    c = RET_CHUNK
    n = s // c
    hh = N_RET_HEADS
    log_gamma = jnp.log(1.0 - 2.0 ** (-5.0 - jnp.arange(hh, dtype=F32)))
    j = jnp.arange(c, dtype=F32)
    diff = j[:, None] - j[None, :]
    dec = jnp.where(diff >= 0, jnp.exp(log_gamma[:, None, None] * jnp.maximum(diff, 0.0)), 0.0)
    cross = jnp.broadcast_to(jnp.exp(log_gamma[:, None] * (j + 1.0))[:, :, None], (hh, c, RET_QK_DIM))
    state = jnp.exp(log_gamma[:, None] * (c - 1.0 - j))[:, None, :]
    cdec = jnp.broadcast_to(jnp.exp(log_gamma * c)[:, None, None], (hh, 1, RET_V_DIM))
    const = lambda bi, ci: (0, 0, 0)
    return pl.pallas_call(
        _ret_body,
        out_shape=jax.ShapeDtypeStruct((b * s, RET_WIDTH), BF16),
        grid=(b, n),
        in_specs=[
            pl.BlockSpec((1, c, RET_QK_WIDTH), lambda bi, ci: (bi, ci, 0)),
            pl.BlockSpec((1, RET_QK_WIDTH, c), lambda bi, ci: (bi, 0, ci)),
            pl.BlockSpec((c, RET_WIDTH), lambda bi, ci: (bi * n + ci, COL_V_RET // RET_WIDTH)),
            pl.BlockSpec((c, RET_WIDTH), lambda bi, ci: (bi * n + ci, COL_G_RET // RET_WIDTH)),
            pl.BlockSpec((hh, c, c), const),
            pl.BlockSpec((hh, c, RET_QK_DIM), const),
            pl.BlockSpec((hh, 1, c), const),
            pl.BlockSpec((hh, 1, RET_V_DIM), const),
        ],
        out_specs=pl.BlockSpec((c, RET_WIDTH), lambda bi, ci: (bi * n + ci, 0)),
        scratch_shapes=[pltpu.VMEM((hh, RET_QK_DIM, RET_V_DIM), F32)],
        compiler_params=_cparams(("arbitrary", "arbitrary")),
        name="ret",
    )(q_rot, kt_rot, proj, proj, dec, cross, state, cdec)


def _dsa_body(topk, kidx_ref, qit_ref, w_ref, c_ref, ct_ref, qt_ref, wuk_ref, wuvt_ref, bias_ref, y_ref,
              sc_ref, qlt_ref, acc_ref, m_ref, l_ref):
    qb = pl.program_id(1)
    tk = KEY_TILE
    half = tk // 2
    e_pad = (qb + 1) * Q_BLOCK + KEY_PAD
    n_tiles = e_pad // tk
    nh = N_ATT_HEADS
    lane_q = qb * Q_BLOCK + lax.broadcasted_iota(jnp.int32, (1, Q_BLOCK), 1)

    def tile_start(j):
        return pl.multiple_of(e_pad - (j + 1) * tk, Q_BLOCK)

    for h in range(nh):
        sl = slice(h * Q_BLOCK, (h + 1) * Q_BLOCK)
        hd = slice(h * ATT_HEAD_DIM, (h + 1) * ATT_HEAD_DIM)
        qlt_ref[:, sl] = (jnp.dot(wuk_ref[h], qt_ref[0, 0, hd, :], preferred_element_type=F32)
                          * (ATT_HEAD_DIM ** -0.5)).astype(BF16)

    def score_tile(j, carry):
        r0 = tile_start(j)
        for part in range(2):
            rs = pl.multiple_of(r0 + part * half, Q_BLOCK)
            z = jnp.dot(kidx_ref[0, pl.ds(rs, half), :], qit_ref[0, 0], preferred_element_type=F32)
            tot = jnp.zeros((half, Q_BLOCK), F32)
            for h in range(IDX_HEADS):
                sl = slice(h * Q_BLOCK, (h + 1) * Q_BLOCK)
                tot = tot + jnp.maximum(z[:, sl], 0.0) * w_ref[0, 0, :, sl]
            key = rs - KEY_PAD + lax.broadcasted_iota(jnp.int32, (half, Q_BLOCK), 0)
            ok = (key >= 0) & (key <= lane_q)
            sc_ref[pl.ds(rs, half), :] = jnp.where(ok, tot, NEG_BIG)
        return carry
    lax.fori_loop(0, n_tiles, score_tile, 0)

    def minmax_tile(j, carry):
        lo, hi = carry
        s = sc_ref[pl.ds(tile_start(j), tk), :]
        hi = jnp.maximum(hi, jnp.max(s, axis=0, keepdims=True))
        lo = jnp.minimum(lo, jnp.min(jnp.where(s > 0.5 * NEG_BIG, s, -NEG_BIG), axis=0, keepdims=True))
        return lo, hi
    lo0, hi0 = lax.fori_loop(0, n_tiles, minmax_tile,
                             (jnp.full((1, Q_BLOCK), -NEG_BIG, F32), jnp.full((1, Q_BLOCK), NEG_BIG, F32)))
    search = lane_q + 1 > topk

    def count_ge(x):
        def body(j, cnt):
            s = sc_ref[pl.ds(tile_start(j), tk), :]
            return cnt + jnp.sum(jnp.where(s >= x, 1.0, 0.0), axis=0, keepdims=True)
        return lax.fori_loop(0, n_tiles, body, jnp.zeros((1, Q_BLOCK), F32))

    def bis_cond(st):
        it, _, _, _, active = st
        return (it < 40) & (active > 0.0)

    def bis_body(st):
        it, lo, hi, done, _ = st
        mid = lo + 0.5 * (hi - lo)
        cnt = count_ge(mid)
        ge = cnt >= topk
        open_ = done < 0.5
        lo = jnp.where(open_ & ge, mid, lo)
        hi = jnp.where(open_ & (~ge), mid, hi)
        done = jnp.where(cnt == topk, 1.0, done)
        return it + 1, lo, hi, done, jnp.max(1.0 - done)

    done0 = jnp.where(search, 0.0, 1.0)
    _, lo_f, _, _, _ = lax.while_loop(bis_cond, bis_body, (jnp.int32(0), lo0, hi0, done0, jnp.max(1.0 - done0)))
    thr = jnp.where(search, lo_f, 0.5 * NEG_BIG)

    m_ref[...] = jnp.full(m_ref.shape, NEG_BIG, F32)
    l_ref[...] = jnp.zeros(l_ref.shape, F32)
    acc_ref[...] = jnp.zeros(acc_ref.shape, F32)

    def attend(rs, n, bias):
        st = jnp.dot(c_ref[0, pl.ds(rs, n), :], qlt_ref[...], preferred_element_type=F32)
        if bias is not None:
            st = st + bias
        pen = jnp.where(sc_ref[pl.ds(rs, n), :] >= thr, 0.0, NEG_BIG)
        st = st + jnp.concatenate([pen] * nh, axis=1)
        m_old = m_ref[...]
        m_new = jnp.maximum(m_old, jnp.max(st, axis=0, keepdims=True))
        alpha = jnp.exp(m_old - m_new)
        p = jnp.exp(st - m_new)
        l_ref[...] = alpha * l_ref[...] + jnp.sum(p, axis=0, keepdims=True)
        acc_ref[...] = acc_ref[...] * alpha + jnp.dot(ct_ref[0, :, pl.ds(rs, n)], p.astype(BF16),
                                                      preferred_element_type=F32)
        m_ref[...] = m_new

    r_first = tile_start(0)
    attend(pl.multiple_of(r_first + half, Q_BLOCK), half, bias_ref[...])
    attend(r_first, half, None)

    def attend_tile(j, carry):
        attend(tile_start(j), tk, None)
        return carry
    lax.fori_loop(1, n_tiles, attend_tile, 0)

    ot = acc_ref[...] / l_ref[...]
    for h in range(nh):
        sl = slice(h * Q_BLOCK, (h + 1) * Q_BLOCK)
        y_ref[0, h * ATT_HEAD_DIM:(h + 1) * ATT_HEAD_DIM, :] = jnp.dot(
            wuvt_ref[h], ot[:, sl].astype(BF16), preferred_element_type=F32).astype(y_ref.dtype)


def _dsa(kidx_p, qit, widx_t, c_p, ct_p, qt, wuk_h, wuvt_h, bias_t, b, s, topk):
    nqb = s // Q_BLOCK
    sp = s + KEY_PAD
    nh = N_ATT_HEADS
    per_b = lambda bi, qi: (bi, 0, 0)
    per_q = lambda bi, qi: (bi, qi, 0, 0)
    c3 = lambda bi, qi: (0, 0, 0)
    return pl.pallas_call(
        functools.partial(_dsa_body, topk),
        out_shape=jax.ShapeDtypeStruct((b, ATT_WIDTH, s), BF16),
        grid=(b, nqb),
        in_specs=[
            pl.BlockSpec((1, sp, LANES), per_b),
            pl.BlockSpec((1, 1, LANES, IDX_HEADS * Q_BLOCK), per_q),
            pl.BlockSpec((1, 1, 1, IDX_HEADS * Q_BLOCK), per_q),
            pl.BlockSpec((1, sp, KV_LATENT), per_b),
            pl.BlockSpec((1, KV_LATENT, sp), per_b),
            pl.BlockSpec((1, 1, ATT_WIDTH, Q_BLOCK), per_q),
            pl.BlockSpec((nh, KV_LATENT, ATT_HEAD_DIM), c3),
            pl.BlockSpec((nh, ATT_HEAD_DIM, KV_LATENT), c3),
            pl.BlockSpec((2 * Q_BLOCK, nh * Q_BLOCK), lambda bi, qi: (0, 0)),
        ],
        out_specs=pl.BlockSpec((1, ATT_WIDTH, Q_BLOCK), lambda bi, qi: (bi, 0, qi)),
        scratch_shapes=[
            pltpu.VMEM((sp, Q_BLOCK), F32),
            pltpu.VMEM((KV_LATENT, nh * Q_BLOCK), BF16),
            pltpu.VMEM((KV_LATENT, nh * Q_BLOCK), F32),
            pltpu.VMEM((1, nh * Q_BLOCK), F32),
            pltpu.VMEM((1, nh * Q_BLOCK), F32),
        ],
        compiler_params=_cparams(("arbitrary", "arbitrary")),
        name="dsa",
    )(kidx_p, qit, widx_t, c_p, ct_p, qt, wuk_h, wuvt_h, bias_t)


def _t5_bucket(dist):
    n = jnp.maximum(dist, 0)
    max_exact = N_BUCKETS // 2
    nf = jnp.maximum(n, 1).astype(F32)
    large = max_exact + (jnp.log(nf / max_exact) / math.log(MAX_DISTANCE / max_exact)
                         * (N_BUCKETS - max_exact)).astype(jnp.int32)
    return jnp.where(n < max_exact, n, jnp.minimum(large, N_BUCKETS - 1))


def _dsa_stage(st, w_uk, w_uv, rel_bias, b, s):
    nqb = s // Q_BLOCK
    nh = N_ATT_HEADS
    proj = st["proj"].reshape(b, nqb, Q_BLOCK, PROJ_WIDTH)
    qt = jnp.swapaxes(proj[..., COL_Q_ATT:COL_Q_ATT + ATT_WIDTH], 2, 3)
    qi = proj[..., COL_Q_IDX:COL_Q_IDX + IDX_WIDTH].reshape(b, nqb, Q_BLOCK, IDX_HEADS, IDX_DIM)
    qit = jnp.transpose(qi, (0, 1, 4, 3, 2)).reshape(b, nqb, IDX_DIM, IDX_HEADS * Q_BLOCK)
    qit = jnp.pad(qit, ((0, 0), (0, 0), (0, LANES - IDX_DIM), (0, 0)))
    wi = st["widx"].reshape(b, nqb, Q_BLOCK, LANES)[..., IDX_DIM:IDX_DIM + IDX_HEADS]
    widx_t = jnp.swapaxes(wi, 2, 3).reshape(b, nqb, 1, IDX_HEADS * Q_BLOCK)
    front = ((0, 0), (KEY_PAD, 0), (0, 0))
    kidx_p = jnp.pad(st["kidx"].reshape(b, s, LANES), front)
    c_p = jnp.pad(st["c"].reshape(b, s, KV_LATENT), front)
    ct_p = jnp.swapaxes(c_p, 1, 2)
    wuk_h = jnp.transpose(w_uk, (1, 0, 2)).astype(BF16)
    wuvt_h = jnp.transpose(w_uv, (1, 2, 0)).astype(BF16)
    i = jnp.arange(2 * Q_BLOCK)[:, None]
    j = jnp.arange(Q_BLOCK)[None, :]
    rb = rel_bias - rel_bias[N_BUCKETS - 1]
    bias = rb[_t5_bucket(j - i + Q_BLOCK)]
    bias_t = jnp.transpose(bias, (0, 2, 1)).reshape(2 * Q_BLOCK, nh * Q_BLOCK)
    topk = min(IDX_TOPK_MAX, s // 4)
    yt = _dsa(kidx_p, qit, widx_t, c_p, ct_p, qt, wuk_h, wuvt_h, bias_t, b, s, topk)
    return jnp.swapaxes(yt, 1, 2).reshape(b * s, ATT_WIDTH)


def _memkv_body(m_ref, g_ref, wk_ref, wv_ref, k_ref, v_ref):
    mn = _rms(m_ref[0], g_ref[...]).astype(BF16)
    k_ref[0] = jnp.dot(mn, wk_ref[...], preferred_element_type=F32).astype(BF16)
    v_ref[0] = jnp.dot(mn, wv_ref[...], preferred_element_type=F32).astype(BF16)


def _memkv(mem, gain, w_ck, w_cv):
    b, m, d = mem.shape
    w = w_ck.shape[1]
    const = lambda i: (0, 0)
    out = jax.ShapeDtypeStruct((b, m, w), BF16)
    return pl.pallas_call(
        _memkv_body,
        out_shape=(out, out),
        grid=(b,),
        in_specs=[
            pl.BlockSpec((1, m, d), lambda i: (i, 0, 0)),
            pl.BlockSpec((1, d), const),
            pl.BlockSpec((d, w), const),
            pl.BlockSpec((d, w), const),
        ],
        out_specs=(pl.BlockSpec((1, m, w), lambda i: (i, 0, 0)), pl.BlockSpec((1, m, w), lambda i: (i, 0, 0))),
        compiler_params=_cparams(("arbitrary",)),
        name="memkv",
    )(mem, gain, w_ck, w_cv)


def _mid_body(x_ref, ya_ref, yr_ref, woa_ref, wob_ref, gc_ref, wq_ref, km_ref, vm_ref, wo_ref,
              gf_ref, wr_ref, br_ref, h2_ref, xn_ref, idx_ref, gate_ref):
    h1 = (x_ref[...]
          + jnp.dot(ya_ref[...], woa_ref[...], preferred_element_type=F32)
          + jnp.dot(yr_ref[...], wob_ref[...], preferred_element_type=F32))
    hn = _rms(h1, gc_ref[...]).astype(BF16)
    q = jnp.dot(hn, wq_ref[...], preferred_element_type=F32).astype(BF16)
    heads = []
    for h in range(N_CROSS_HEADS):
        sl = slice(h * CROSS_HEAD_DIM, (h + 1) * CROSS_HEAD_DIM)
        s = lax.dot_general(q[:, sl], km_ref[0, :, sl], (((1,), (1,)), ((), ())),
                            preferred_element_type=F32) * (CROSS_HEAD_DIM ** -0.5)
        p = jnp.exp(s - jnp.max(s, axis=-1, keepdims=True))
        p = p / jnp.sum(p, axis=-1, keepdims=True)
        heads.append(jnp.dot(p.astype(BF16), vm_ref[0, :, sl], preferred_element_type=F32).astype(BF16))
    o = jnp.concatenate(heads, axis=-1)
    h2 = h1 + jnp.dot(o, wo_ref[...], preferred_element_type=F32)
    h2_ref[...] = h2

    xn = _rms(h2, gf_ref[...])
    xn_ref[...] = xn
    logit = lax.dot_general(wr_ref[...], xn.astype(BF16), (((1,), (1,)), ((), ())),
                            preferred_element_type=F32) + br_ref[...]
    eid = lax.broadcasted_iota(jnp.int32, logit.shape, 0)
    vals, ids = [], []
    for _ in range(TOP_K):
        mx = jnp.max(logit, axis=0, keepdims=True)
        sel = jnp.min(jnp.where(logit == mx, eid, N_EXPERTS), axis=0, keepdims=True)
        vals.append(mx)
        ids.append(sel)
        logit = jnp.where(eid == sel, -jnp.inf, logit)
    top = jnp.concatenate(vals, axis=0)
    e = jnp.exp(top - vals[0])
    gate_ref[...] = e / jnp.sum(e, axis=0, keepdims=True)
    idx_ref[...] = jnp.concatenate(ids, axis=0)


def _mid(x2, ya, yr, w_out_a, w_out_b, g_cross, w_cq, km, vm, w_co, g_ffn, w_rt, b_r, b, s, tm):
    t, d = x2.shape
    nt = s // tm
    row = lambda bi, i: (bi * nt + i, 0)
    const = lambda bi, i: (0, 0)
    once = pl.Buffered(1)
    m = km.shape[1]
    return pl.pallas_call(
        _mid_body,
        out_shape=(
            jax.ShapeDtypeStruct((t, d), F32),
            jax.ShapeDtypeStruct((t, d), F32),
            jax.ShapeDtypeStruct((TOP_K, t), jnp.int32),
            jax.ShapeDtypeStruct((TOP_K, t), F32),
        ),
        grid=(b, nt),
        in_specs=[
            pl.BlockSpec((tm, d), row),
            pl.BlockSpec((tm, ATT_WIDTH), row),
            pl.BlockSpec((tm, RET_WIDTH), row),
            pl.BlockSpec((ATT_WIDTH, d), const, pipeline_mode=once),
            pl.BlockSpec((RET_WIDTH, d), const, pipeline_mode=once),
            pl.BlockSpec((1, d), const),
            pl.BlockSpec((d, CROSS_WIDTH), const, pipeline_mode=once),
            pl.BlockSpec((1, m, CROSS_WIDTH), lambda bi, i: (bi, 0, 0)),
            pl.BlockSpec((1, m, CROSS_WIDTH), lambda bi, i: (bi, 0, 0)),
            pl.BlockSpec((CROSS_WIDTH, d), const, pipeline_mode=once),
            pl.BlockSpec((1, d), const),
            pl.BlockSpec((N_EXPERTS, d), const),
            pl.BlockSpec((N_EXPERTS, 1), const),
        ],
        out_specs=(
            pl.BlockSpec((tm, d), row),
            pl.BlockSpec((tm, d), row),
            pl.BlockSpec((TOP_K, tm), lambda bi, i: (0, bi * nt + i)),
            pl.BlockSpec((TOP_K, tm), lambda bi, i: (0, bi * nt + i)),
        ),
        compiler_params=_cparams(("arbitrary", "arbitrary")),
        name="mid",
    )(x2, ya, yr, w_out_a, w_out_b, g_cross, w_cq, km, vm, w_co, g_ffn, w_rt, b_r)


def _experts_body(be_ref, nu_ref, idx_hbm, gate_ref, xn_hbm, wg_ref, wu_ref, bg_ref, bu_ref, wd_ref, bd_ref,
                  y_hbm, idx_smem, xbuf, xb_ref, acc_ref, sem_idx, sem_g, sem_s):
    i = pl.program_id(0)
    j = pl.program_id(1)
    n_used = nu_ref[0]
    rows = xb_ref.shape[0]

    def idx_copy(blk, slot):
        return pltpu.make_async_copy(idx_hbm.at[blk], idx_smem.at[slot], sem_idx.at[slot])

    def gather_start(islot, xslot):
        def body(r, carry):
            tok = idx_smem[islot, 0, r]
            pltpu.make_async_copy(xn_hbm.at[pl.ds(tok, 1)], xbuf.at[xslot, pl.ds(r, 1)], sem_g.at[xslot]).start()
            return carry
        lax.fori_loop(0, rows, body, 0, unroll=8)

    def gather_wait(xslot):
        pltpu.make_async_copy(xn_hbm.at[pl.ds(0, rows)], xbuf.at[xslot], sem_g.at[xslot]).wait()

    @pl.when(j == 0)
    def _():
        @pl.when(i == 0)
        def _():
            acc_ref[...] = jnp.zeros_like(acc_ref)
            dump = pltpu.make_async_copy(acc_ref, y_hbm.at[pl.ds(y_hbm.shape[0] - rows, rows)], sem_s)
            dump.start()
            dump.wait()
            idx_copy(0, 0).start()

            @pl.when(n_used > 1)
            def _():
                idx_copy(1, 1).start()

            idx_copy(0, 0).wait()
            gather_start(0, 0)

        @pl.when(i + 1 < n_used)
        def _():
            idx_copy(i + 1, (i + 1) % 3).wait()
            gather_start((i + 1) % 3, (i + 1) % 2)

        @pl.when(i + 2 < n_used)
        def _():
            idx_copy(i + 2, (i + 2) % 3).start()

        @pl.when(i < n_used)
        def _():
            gather_wait(i % 2)
            xb_ref[...] = xbuf[i % 2].astype(BF16)

    @pl.when(i < n_used)
    def _():
        xb = xb_ref[...]
        g = jnp.dot(xb, wg_ref[0], preferred_element_type=F32) + bg_ref[0]
        u = jnp.dot(xb, wu_ref[0], preferred_element_type=F32) + bu_ref[0]
        gt = jnp.minimum(g, SWIGLU_LIMIT)
        up = jnp.clip(u, -SWIGLU_LIMIT, SWIGLU_LIMIT)
        hm = ((up + 1.0) * (gt * jax.nn.sigmoid(SWIGLU_ALPHA * gt))).astype(BF16)
        part = jnp.dot(hm, wd_ref[0], preferred_element_type=F32)

        @pl.when(j == 0)
        def _():
            acc_ref[...] = part

        @pl.when(j > 0)
        def _():
            acc_ref[...] += part

        @pl.when(j == pl.num_programs(1) - 1)
        def _():
            acc_ref[...] = (acc_ref[...] + bd_ref[0]) * gate_ref[0]
            islot = i % 3

            def body(r, carry):
                dst = idx_smem[islot, 1, r]
                pltpu.make_async_copy(acc_ref.at[pl.ds(r, 1)], y_hbm.at[pl.ds(dst, 1)], sem_s).start()
                return carry
            lax.fori_loop(0, rows, body, 0, unroll=8)
            pltpu.make_async_copy(acc_ref, y_hbm.at[pl.ds(0, rows)], sem_s).wait()


def _experts(block_expert, n_used, slot_idx, slot_gate, xn, w_gu, b_gu, w_down, b_down):
    nb, _, rows = slot_idx.shape
    t, d = xn.shape
    e, _, ff2 = w_gu.shape
    ff = ff2 // 2
    tf = MOE_TF
    nj = ff // tf
    a_rows = t * TOP_K + rows
    grid_spec = pltpu.PrefetchScalarGridSpec(
        num_scalar_prefetch=2,
        grid=(nb, nj),
        in_specs=[
            pl.BlockSpec(memory_space=pl.ANY),
            pl.BlockSpec((1, rows, 1), lambda i, j, be, nu: (i, 0, 0)),
            pl.BlockSpec(memory_space=pl.ANY),
            pl.BlockSpec((1, d, tf), lambda i, j, be, nu: (be[i], 0, j)),
            pl.BlockSpec((1, d, tf), lambda i, j, be, nu: (be[i], 0, j + nj)),
            pl.BlockSpec((1, 1, tf), lambda i, j, be, nu: (be[i], 0, j)),
            pl.BlockSpec((1, 1, tf), lambda i, j, be, nu: (be[i], 0, j + nj)),
            pl.BlockSpec((1, tf, d), lambda i, j, be, nu: (be[i], j, 0)),
            pl.BlockSpec((1, 1, d), lambda i, j, be, nu: (be[i], 0, 0)),
        ],
        out_specs=pl.BlockSpec(memory_space=pl.ANY),
        scratch_shapes=[
            pltpu.SMEM((3, 2, rows), jnp.int32),
            pltpu.VMEM((2, rows, d), F32),
            pltpu.VMEM((rows, d), BF16),
            pltpu.VMEM((rows, d), F32),
            pltpu.SemaphoreType.DMA((3,)),
            pltpu.SemaphoreType.DMA((2,)),
            pltpu.SemaphoreType.DMA,
        ],
    )
    return pl.pallas_call(
        _experts_body,
        out_shape=jax.ShapeDtypeStruct((a_rows, d), F32),
        grid_spec=grid_spec,
        compiler_params=_cparams(("arbitrary", "arbitrary")),
        name="experts",
    )(block_expert, n_used, slot_idx, slot_gate, xn, w_gu, w_gu, b_gu, b_gu, w_down, b_down)


def _combine_body(final_norm, h_ref, y_ref, g_ref, o_ref):
    d = h_ref.shape[1]
    acc = h_ref[...]
    for k in range(TOP_K):
        acc = acc + y_ref[:, k * d:(k + 1) * d]
    o_ref[...] = _rms(acc, g_ref[...]) if final_norm else acc


def _combine(h2, y4, gain, tm):
    t, d = h2.shape
    final_norm = gain is not None
    if gain is None:
        gain = jnp.ones((d,), F32)
    gain = gain.reshape(1, d)
    return pl.pallas_call(
        functools.partial(_combine_body, final_norm),
        out_shape=jax.ShapeDtypeStruct((t, d), F32),
        grid=(t // tm,),
        in_specs=[
            pl.BlockSpec((tm, d), lambda i: (i, 0)),
            pl.BlockSpec((tm, TOP_K * d), lambda i: (i, 0)),
            pl.BlockSpec((1, d), lambda i: (0, 0)),
        ],
        out_specs=pl.BlockSpec((tm, d), lambda i: (i, 0)),
        compiler_params=_cparams(("arbitrary",)),
        name="combine",
    )(h2, y4, gain)


def _routing(top_idx, gates, rows):
    k, t = top_idx.shape
    a = t * k
    flat_e = top_idx.T.reshape(-1)
    flat_g = gates.T.reshape(-1)
    order = jnp.argsort(flat_e).astype(jnp.int32)
    counts = jnp.zeros((N_EXPERTS,), jnp.int32).at[flat_e].add(1)
    start = jnp.cumsum(counts) - counts
    padded = (counts + rows - 1) // rows * rows
    padded_end = jnp.cumsum(padded)
    padded_start = padded_end - padded
    nb = -(-a // rows) + N_EXPERTS
    n_used = (padded_end[-1] // rows).astype(jnp.int32)
    blk = jnp.arange(nb, dtype=jnp.int32)
    bexp = jnp.minimum(jnp.searchsorted(padded_end, blk * rows, side="right"), N_EXPERTS - 1).astype(jnp.int32)
    bexp = jnp.where(blk < n_used, bexp, bexp[jnp.maximum(n_used - 1, 0)])
    p = jnp.arange(nb * rows, dtype=jnp.int32)
    pe = jnp.repeat(bexp, rows)
    r = p - padded_start[pe]
    valid = (r < counts[pe]) & (p < padded_end[-1])
    asg = order[jnp.clip(start[pe] + r, 0, a - 1)]
    slot_src = jnp.where(valid, asg // k, 0)
    slot_dst = jnp.where(valid, asg, a + p % rows)
    slot_gate = jnp.where(valid, flat_g[asg], 0.0)
    slot_idx = jnp.stack([slot_src.reshape(nb, rows), slot_dst.reshape(nb, rows)], axis=1)
    return bexp, n_used.reshape(1), slot_idx, slot_gate.reshape(nb, rows, 1)


def _moe_stage(h2, xn, top_idx, gates, w_gate_up, b_gate_up, w_down, b_down, norm_final):
    t, d = h2.shape
    bexp, n_used, slot_idx, slot_gate = _routing(top_idx, gates, MOE_ROWS)
    e = w_gate_up.shape[0]
    y_tok = _experts(bexp, n_used, slot_idx, slot_gate, xn, w_gate_up.astype(BF16),
                     b_gate_up.reshape(e, 1, -1), w_down.astype(BF16), b_down.reshape(e, 1, -1))
    y4 = y_tok.reshape(-1, TOP_K * d)
    return _combine(h2, y4, norm_final, min(256, t))


def _permute_w_in(w_in):
    sp = np.cumsum([0, ATT_WIDTH, KV_LATENT, IDX_WIDTH, IDX_DIM, IDX_HEADS, RET_QK_WIDTH, RET_QK_WIDTH,
                    RET_WIDTH, RET_WIDTH])
    piece = lambda i: w_in[:, int(sp[i]):int(sp[i + 1])]
    d = w_in.shape[0]
    cols = [piece(0), piece(2), piece(7), piece(8), piece(5), piece(6), piece(1), piece(3), piece(4)]
    used = sum(c.shape[1] for c in cols)
    cols.append(jnp.zeros((d, PROJ_WIDTH - used), w_in.dtype))
    return jnp.concatenate(cols, axis=1).astype(BF16)


def _rotary_tables(positions):
    half = RET_QK_DIM // 2
    inv = ROPE_BASE ** (-jnp.arange(0, RET_QK_DIM, 2, dtype=F32) / RET_QK_DIM)
    ang = positions.astype(F32).reshape(-1, 1) * inv
    cos, sin = jnp.cos(ang), jnp.sin(ang)
    return jnp.concatenate([cos, cos], axis=1), jnp.concatenate([-sin, sin], axis=1)


def _front(x, positions, norm_mix, w_in, kv_norm, idx_k_norm):
    b, s, d = x.shape
    t = b * s
    tm = min(1024, t)
    proj = _proj(x.reshape(t, d), norm_mix.reshape(1, d), _permute_w_in(w_in), tm)
    cos2, sin2 = _rotary_tables(positions)
    g_k = jnp.concatenate([idx_k_norm, jnp.zeros((LANES - IDX_DIM,), F32)]).reshape(1, LANES)
    c, kidx, widx, q_rot, k_rot = _prep(proj, cos2, sin2, kv_norm.reshape(1, KV_LATENT), g_k, tm)
    return dict(proj=proj, c=c, kidx=kidx, widx=widx, q_rot=q_rot, k_rot=k_rot)


def _mid_stage(x, mem, y_att, y_ret, w_out, norm_cross, norm_mem, w_cq, w_ck, w_cv, w_co, norm_ffn,
               w_router, b_router):
    b, s, d = x.shape
    km, vm = _memkv(mem, norm_mem.reshape(1, d), w_ck.astype(BF16), w_cv.astype(BF16))
    w_out_b16 = w_out.astype(BF16)
    return _mid(x.reshape(b * s, d), y_att, y_ret, w_out_b16[:ATT_WIDTH], w_out_b16[ATT_WIDTH:],
                norm_cross.reshape(1, d), w_cq.astype(BF16), km, vm, w_co.astype(BF16),
                norm_ffn.reshape(1, d), w_router.T.astype(BF16), b_router.reshape(N_EXPERTS, 1),
                b, s, min(512, s))


def _ret_stage(st, b, s):
    q_rot = st["q_rot"].reshape(b, s, RET_QK_WIDTH)
    kt_rot = jnp.swapaxes(st["k_rot"].reshape(b, s, RET_QK_WIDTH), 1, 2)
    return _retention(q_rot, kt_rot, st["proj"], b, s)


def kernel(x, mem, positions, norm_mix, w_in, kv_norm, idx_k_norm, w_uk, w_uv, rel_bias, w_out, norm_cross,
           norm_mem, w_cq, w_ck, w_cv, w_co, norm_ffn, w_router, b_router, w_gate_up, b_gate_up, w_down, b_down,
           norm_final):
    b, s, d = x.shape
    depth = norm_mix.shape[0]
    h = x
    for l in range(depth):
        st = _front(h, positions, norm_mix[l], w_in[l], kv_norm[l], idx_k_norm[l])
        y_att = _dsa_stage(st, w_uk[l], w_uv[l], rel_bias, b, s)
        y_ret = _ret_stage(st, b, s)
        h2, xn, top_idx, gates = _mid_stage(h, mem, y_att, y_ret, w_out[l], norm_cross[l], norm_mem[l], w_cq[l],
                                            w_ck[l], w_cv[l], w_co[l], norm_ffn[l], w_router[l], b_router[l])
        gain = norm_final if l == depth - 1 else None
        h = _moe_stage(h2, xn, top_idx, gates, w_gate_up[l], b_gate_up[l], w_down[l], b_down[l], gain).reshape(b, s, d)
    return h
```

```python
import functools
import math

import jax
import jax.numpy as jnp
import numpy as np
from jax import lax
from jax.experimental import pallas as pl
from jax.experimental.pallas import tpu as pltpu

F32 = jnp.float32
BF16 = jnp.bfloat16

D_MODEL = 2048
N_ATT_HEADS = 8
ATT_HEAD_DIM = 128
KV_LATENT = 256
IDX_HEADS = 16
IDX_DIM = 64
IDX_TOPK_MAX = 256
N_RET_HEADS = 4
RET_QK_DIM = 128
RET_V_DIM = 256
ROPE_BASE = 10000.0
N_BUCKETS = 32
MAX_DISTANCE = 128
N_CROSS_HEADS = 4
CROSS_HEAD_DIM = 128
N_EXPERTS = 32
TOP_K = 4
D_FF = D_MODEL
SWIGLU_LIMIT = 7.0
SWIGLU_ALPHA = 1.702
EPS = 1e-6

ATT_WIDTH = N_ATT_HEADS * ATT_HEAD_DIM
RET_WIDTH = N_RET_HEADS * RET_V_DIM
RET_QK_WIDTH = N_RET_HEADS * RET_QK_DIM
IDX_WIDTH = IDX_HEADS * IDX_DIM
CROSS_WIDTH = N_CROSS_HEADS * CROSS_HEAD_DIM

LANES = 128
VMEM_LIMIT = 56 * 1024 * 1024

COL_Q_ATT = 0
COL_Q_IDX = COL_Q_ATT + ATT_WIDTH
COL_V_RET = COL_Q_IDX + IDX_WIDTH
COL_G_RET = COL_V_RET + RET_WIDTH
COL_Q_RET = COL_G_RET + RET_WIDTH
COL_K_RET = COL_Q_RET + RET_QK_WIDTH
COL_C_KV = COL_K_RET + RET_QK_WIDTH
COL_KW = COL_C_KV + KV_LATENT
PROJ_TN = 512
PROJ_WIDTH = 5632

Q_BLOCK = 128
KEY_TILE = 512
KEY_PAD = KEY_TILE - Q_BLOCK
NEG_BIG = -1e30

RET_CHUNK = 256

MOE_ROWS = 512
MOE_TF = 1024


def _cparams(sem, vmem=VMEM_LIMIT):
    return pltpu.CompilerParams(dimension_semantics=sem, vmem_limit_bytes=vmem)


def _rms(x, g):
    return x * lax.rsqrt(jnp.mean(x * x, axis=-1, keepdims=True) + EPS) * g


def _proj_body(x_ref, g_ref, w_ref, o_ref, xn_ref):
    @pl.when(pl.program_id(1) == 0)
    def _():
        xn_ref[...] = _rms(x_ref[...], g_ref[...]).astype(BF16)

    o_ref[...] = jnp.dot(xn_ref[...], w_ref[...], preferred_element_type=F32).astype(o_ref.dtype)


def _proj(x2, gain, w_p, tm):
    t, d = x2.shape
    n = w_p.shape[1]
    return pl.pallas_call(
        _proj_body,
        out_shape=jax.ShapeDtypeStruct((t, n), BF16),
        grid=(t // tm, n // PROJ_TN),
        in_specs=[
            pl.BlockSpec((tm, d), lambda i, j: (i, 0)),
            pl.BlockSpec((1, d), lambda i, j: (0, 0)),
            pl.BlockSpec((d, PROJ_TN), lambda i, j: (0, j)),
        ],
        out_specs=pl.BlockSpec((tm, PROJ_TN), lambda i, j: (i, j)),
        scratch_shapes=[pltpu.VMEM((tm, d), BF16)],
        compiler_params=_cparams(("arbitrary", "arbitrary")),
        name="proj",
    )(x2, gain, w_p)


def _prep_body(c_ref, kw_ref, q_ref, k_ref, cos_ref, sin_ref, gkv_ref, gk_ref,
               c_out, kidx_out, widx_out, q_out, k_out):
    c_out[...] = _rms(c_ref[...].astype(F32), gkv_ref[...]).astype(BF16)

    kw = kw_ref[...].astype(F32)
    lane = lax.broadcasted_iota(jnp.int32, kw.shape, 1)
    is_k = lane < IDX_DIM
    kk = jnp.where(is_k, kw, 0.0)
    ms = jnp.sum(kk * kk, axis=-1, keepdims=True) * (1.0 / IDX_DIM)
    kidx_out[...] = (kk * lax.rsqrt(ms + EPS) * gk_ref[...]).astype(BF16)
    widx_out[...] = kw * (IDX_HEADS ** -0.5 * IDX_DIM ** -0.5)

    cos2 = cos_ref[...]
    sin2 = sin_ref[...]
    for h in range(N_RET_HEADS):
        sl = slice(h * RET_QK_DIM, (h + 1) * RET_QK_DIM)
        qh = q_ref[:, sl].astype(F32)
        kh = k_ref[:, sl].astype(F32)
        q_out[:, sl] = (qh * cos2 + pltpu.roll(qh, RET_QK_DIM // 2, 1) * sin2).astype(BF16)
        k_out[:, sl] = ((kh * cos2 + pltpu.roll(kh, RET_QK_DIM // 2, 1) * sin2)
                        * (RET_QK_DIM ** -0.5)).astype(BF16)


def _prep(proj, cos2, sin2, g_kv, g_k, tm):
    t = proj.shape[0]
    row = lambda i: (i, 0)
    return pl.pallas_call(
        _prep_body,
        out_shape=(
            jax.ShapeDtypeStruct((t, KV_LATENT), BF16),
            jax.ShapeDtypeStruct((t, LANES), BF16),
            jax.ShapeDtypeStruct((t, LANES), F32),
            jax.ShapeDtypeStruct((t, RET_QK_WIDTH), BF16),
            jax.ShapeDtypeStruct((t, RET_QK_WIDTH), BF16),
        ),
        grid=(t // tm,),
        in_specs=[
            pl.BlockSpec((tm, KV_LATENT), lambda i: (i, COL_C_KV // KV_LATENT)),
            pl.BlockSpec((tm, LANES), lambda i: (i, COL_KW // LANES)),
            pl.BlockSpec((tm, RET_QK_WIDTH), lambda i: (i, COL_Q_RET // RET_QK_WIDTH)),
            pl.BlockSpec((tm, RET_QK_WIDTH), lambda i: (i, COL_K_RET // RET_QK_WIDTH)),
            pl.BlockSpec((tm, LANES), row),
            pl.BlockSpec((tm, LANES), row),
            pl.BlockSpec((1, KV_LATENT), lambda i: (0, 0)),
            pl.BlockSpec((1, LANES), lambda i: (0, 0)),
        ],
        out_specs=(
            pl.BlockSpec((tm, KV_LATENT), row),
            pl.BlockSpec((tm, LANES), row),
            pl.BlockSpec((tm, LANES), row),
            pl.BlockSpec((tm, RET_QK_WIDTH), row),
            pl.BlockSpec((tm, RET_QK_WIDTH), row),
        ),
        compiler_params=_cparams(("arbitrary",)),
        name="prep",
    )(proj, proj, proj, proj, cos2, sin2, g_kv, g_k)


def _ret_body(q_ref, kt_ref, v_ref, g_ref, dec_ref, cross_ref, state_ref, cd_ref, o_ref, r_ref):
    @pl.when(pl.program_id(1) == 0)
    def _():
        r_ref[...] = jnp.zeros_like(r_ref)

    for h in range(N_RET_HEADS):
        qs = slice(h * RET_QK_DIM, (h + 1) * RET_QK_DIM)
        vs = slice(h * RET_V_DIM, (h + 1) * RET_V_DIM)
        q = q_ref[0, :, qs]
        kt = kt_ref[0, qs, :]
        v = v_ref[:, vs]
        r_old = r_ref[h]
        inner = jnp.dot(q, kt, preferred_element_type=F32) * dec_ref[h]
        qc = (q.astype(F32) * cross_ref[h]).astype(BF16)
        o = (jnp.dot(inner.astype(BF16), v, preferred_element_type=F32)
             + jnp.dot(qc, r_old.astype(BF16), preferred_element_type=F32))
        ks = (kt.astype(F32) * state_ref[h]).astype(BF16)
        r_ref[h] = r_old * cd_ref[h] + jnp.dot(ks, v, preferred_element_type=F32)
        o = o * lax.rsqrt(jnp.mean(o * o, axis=-1, keepdims=True) + EPS)
        g = g_ref[:, vs].astype(F32)
        o_ref[:, vs] = (g * jax.nn.sigmoid(g) * o).astype(o_ref.dtype)


def _retention(q_rot, kt_rot, proj, b, s):
    c = RET_CHUNK
    n = s // c
    hh = N_RET_HEADS
    log_gamma = jnp.log(1.0 - 2.0 ** (-5.0 - jnp.arange(hh, dtype=F32)))
    j = jnp.arange(c, dtype=F32)
    diff = j[:, None] - j[None, :]
    dec = jnp.where(diff >= 0, jnp.exp(log_gamma[:, None, None] * jnp.maximum(diff, 0.0)), 0.0)
    cross = jnp.broadcast_to(jnp.exp(log_gamma[:, None] * (j + 1.0))[:, :, None], (hh, c, RET_QK_DIM))
    state = jnp.exp(log_gamma[:, None] * (c - 1.0 - j))[:, None, :]
    cdec = jnp.broadcast_to(jnp.exp(log_gamma * c)[:, None, None], (hh, 1, RET_V_DIM))
    const = lambda bi, ci: (0, 0, 0)
    return pl.pallas_call(
        _ret_body,
        out_shape=jax.ShapeDtypeStruct((b * s, RET_WIDTH), BF16),
        grid=(b, n),
        in_specs=[
            pl.BlockSpec((1, c, RET_QK_WIDTH), lambda bi, ci: (bi, ci, 0)),
            pl.BlockSpec((1, RET_QK_WIDTH, c), lambda bi, ci: (bi, 0, ci)),
            pl.BlockSpec((c, RET_WIDTH), lambda bi, ci: (bi * n + ci, COL_V_RET // RET_WIDTH)),
            pl.BlockSpec((c, RET_WIDTH), lambda bi, ci: (bi * n + ci, COL_G_RET // RET_WIDTH)),
            pl.BlockSpec((hh, c, c), const),
            pl.BlockSpec((hh, c, RET_QK_DIM), const),
            pl.BlockSpec((hh, 1, c), const),
            pl.BlockSpec((hh, 1, RET_V_DIM), const),
        ],
        out_specs=pl.BlockSpec((c, RET_WIDTH), lambda bi, ci: (bi * n + ci, 0)),
        scratch_shapes=[pltpu.VMEM((hh, RET_QK_DIM, RET_V_DIM), F32)],
        compiler_params=_cparams(("arbitrary", "arbitrary")),
        name="ret",
    )(q_rot, kt_rot, proj, proj, dec, cross, state, cdec)


def _dsa_body(topk, kidx_ref, qit_ref, w_ref, c_ref, ct_ref, qt_ref, wuk_ref, wuvt_ref, bias_ref, y_ref,
              sc_ref, qlt_ref, acc_ref, m_ref, l_ref):
    qb = pl.program_id(1)
    tk = KEY_TILE
    half = tk // 2
    e_pad = (qb + 1) * Q_BLOCK + KEY_PAD
    n_tiles = e_pad // tk
    nh = N_ATT_HEADS
    lane_q = qb * Q_BLOCK + lax.broadcasted_iota(jnp.int32, (1, Q_BLOCK), 1)

    def tile_start(j):
        return pl.multiple_of(e_pad - (j + 1) * tk, Q_BLOCK)

    for h in range(nh):
        sl = slice(h * Q_BLOCK, (h + 1) * Q_BLOCK)
        hd = slice(h * ATT_HEAD_DIM, (h + 1) * ATT_HEAD_DIM)
        qlt_ref[:, sl] = (jnp.dot(wuk_ref[h], qt_ref[0, 0, hd, :], preferred_element_type=F32)
                          * (ATT_HEAD_DIM ** -0.5)).astype(BF16)

    def score_tile(j, carry):
        r0 = tile_start(j)
        for part in range(2):
            rs = pl.multiple_of(r0 + part * half, Q_BLOCK)
            z = jnp.dot(kidx_ref[0, pl.ds(rs, half), :], qit_ref[0, 0], preferred_element_type=F32)
            tot = jnp.zeros((half, Q_BLOCK), F32)
            for h in range(IDX_HEADS):
                sl = slice(h * Q_BLOCK, (h + 1) * Q_BLOCK)
                tot = tot + jnp.maximum(z[:, sl], 0.0) * w_ref[0, 0, :, sl]
            key = rs - KEY_PAD + lax.broadcasted_iota(jnp.int32, (half, Q_BLOCK), 0)
            ok = (key >= 0) & (key <= lane_q)
            sc_ref[pl.ds(rs, half), :] = jnp.where(ok, tot, NEG_BIG)
        return carry
    lax.fori_loop(0, n_tiles, score_tile, 0)

    def minmax_tile(j, carry):
        lo, hi = carry
        s = sc_ref[pl.ds(tile_start(j), tk), :]
        hi = jnp.maximum(hi, jnp.max(s, axis=0, keepdims=True))
        lo = jnp.minimum(lo, jnp.min(jnp.where(s > 0.5 * NEG_BIG, s, -NEG_BIG), axis=0, keepdims=True))
        return lo, hi
    lo0, hi0 = lax.fori_loop(0, n_tiles, minmax_tile,
                             (jnp.full((1, Q_BLOCK), -NEG_BIG, F32), jnp.full((1, Q_BLOCK), NEG_BIG, F32)))
    search = lane_q + 1 > topk

    n_chain = 4
    rows_chain = tk // n_chain

    def count_ge(x):
        def body(j, cnts):
            r0 = tile_start(j)
            out = []
            for k in range(n_chain):
                s = sc_ref[pl.ds(pl.multiple_of(r0 + k * rows_chain, Q_BLOCK), rows_chain), :]
                ind = jnp.where(s >= x, 1.0, 0.0).reshape(rows_chain // 8, 8, Q_BLOCK)
                out.append(cnts[k] + jnp.sum(ind, axis=0))
            return tuple(out)
        cnts = lax.fori_loop(0, n_tiles, body, tuple(jnp.zeros((8, Q_BLOCK), F32) for _ in range(n_chain)))
        return jnp.sum((cnts[0] + cnts[1]) + (cnts[2] + cnts[3]), axis=0, keepdims=True)

    def bis_cond(st):
        it, _, _, _, active = st
        return (it < 40) & (active > 0.0)

    def bis_body(st):
        it, lo, hi, done, _ = st
        mid = lo + 0.5 * (hi - lo)
        cnt = count_ge(mid)
        ge = cnt >= topk
        open_ = done < 0.5
        lo = jnp.where(open_ & ge, mid, lo)
        hi = jnp.where(open_ & (~ge), mid, hi)
        done = jnp.where(cnt == topk, 1.0, done)
        return it + 1, lo, hi, done, jnp.max(1.0 - done)

    done0 = jnp.where(search, 0.0, 1.0)
    _, lo_f, _, _, _ = lax.while_loop(bis_cond, bis_body, (jnp.int32(0), lo0, hi0, done0, jnp.max(1.0 - done0)))
    thr = jnp.where(search, lo_f, 0.5 * NEG_BIG)

    m_ref[...] = jnp.full(m_ref.shape, NEG_BIG, F32)
    l_ref[...] = jnp.zeros(l_ref.shape, F32)
    acc_ref[...] = jnp.zeros(acc_ref.shape, F32)

    n_grp = 2
    hg = nh // n_grp

    def attend(rs, n, with_bias):
        c_tile = c_ref[0, pl.ds(rs, n), :]
        ct_tile = ct_ref[0, :, pl.ds(rs, n)]
        pen = jnp.where(sc_ref[pl.ds(rs, n), :] >= thr, 0.0, NEG_BIG)
        pen = jnp.concatenate([pen] * hg, axis=1)
        for g in range(n_grp):
            ls = slice(g * hg * Q_BLOCK, (g + 1) * hg * Q_BLOCK)
            st = jnp.dot(c_tile, qlt_ref[:, ls], preferred_element_type=F32) + pen
            if with_bias:
                st = st + bias_ref[:, ls]
            m_old = m_ref[:, ls]
            m_new = jnp.maximum(m_old, jnp.max(st, axis=0, keepdims=True))
            alpha = jnp.exp(m_old - m_new)
            p = jnp.exp(st - m_new)
            l_ref[:, ls] = alpha * l_ref[:, ls] + jnp.sum(p, axis=0, keepdims=True)
            acc_ref[:, ls] = acc_ref[:, ls] * alpha + jnp.dot(ct_tile, p.astype(BF16), preferred_element_type=F32)
            m_ref[:, ls] = m_new

    r_first = tile_start(0)
    attend(pl.multiple_of(r_first + half, Q_BLOCK), half, True)
    attend(r_first, half, False)

    def attend_tile(j, carry):
        attend(tile_start(j), tk, False)
        return carry
    lax.fori_loop(1, n_tiles, attend_tile, 0)

    ot = acc_ref[...] / l_ref[...]
    for h in range(nh):
        sl = slice(h * Q_BLOCK, (h + 1) * Q_BLOCK)
        y_ref[0, h * ATT_HEAD_DIM:(h + 1) * ATT_HEAD_DIM, :] = jnp.dot(
            wuvt_ref[h], ot[:, sl].astype(BF16), preferred_element_type=F32).astype(y_ref.dtype)


def _dsa(kidx_p, qit, widx_t, c_p, ct_p, qt, wuk_h, wuvt_h, bias_t, b, s, topk):
    nqb = s // Q_BLOCK
    sp = s + KEY_PAD
    nh = N_ATT_HEADS
    per_b = lambda bi, qi: (bi, 0, 0)
    per_q = lambda bi, qi: (bi, qi, 0, 0)
    c3 = lambda bi, qi: (0, 0, 0)
    return pl.pallas_call(
        functools.partial(_dsa_body, topk),
        out_shape=jax.ShapeDtypeStruct((b, ATT_WIDTH, s), BF16),
        grid=(b, nqb),
        in_specs=[
            pl.BlockSpec((1, sp, LANES), per_b),
            pl.BlockSpec((1, 1, LANES, IDX_HEADS * Q_BLOCK), per_q),
            pl.BlockSpec((1, 1, 1, IDX_HEADS * Q_BLOCK), per_q),
            pl.BlockSpec((1, sp, KV_LATENT), per_b),
            pl.BlockSpec((1, KV_LATENT, sp), per_b),
            pl.BlockSpec((1, 1, ATT_WIDTH, Q_BLOCK), per_q),
            pl.BlockSpec((nh, KV_LATENT, ATT_HEAD_DIM), c3),
            pl.BlockSpec((nh, ATT_HEAD_DIM, KV_LATENT), c3),
            pl.BlockSpec((2 * Q_BLOCK, nh * Q_BLOCK), lambda bi, qi: (0, 0)),
        ],
        out_specs=pl.BlockSpec((1, ATT_WIDTH, Q_BLOCK), lambda bi, qi: (bi, 0, qi)),
        scratch_shapes=[
            pltpu.VMEM((sp, Q_BLOCK), F32),
            pltpu.VMEM((KV_LATENT, nh * Q_BLOCK), BF16),
            pltpu.VMEM((KV_LATENT, nh * Q_BLOCK), F32),
            pltpu.VMEM((1, nh * Q_BLOCK), F32),
            pltpu.VMEM((1, nh * Q_BLOCK), F32),
        ],
        compiler_params=_cparams(("arbitrary", "arbitrary")),
        name="dsa",
    )(kidx_p, qit, widx_t, c_p, ct_p, qt, wuk_h, wuvt_h, bias_t)


def _t5_bucket(dist):
    n = jnp.maximum(dist, 0)
    max_exact = N_BUCKETS // 2
    nf = jnp.maximum(n, 1).astype(F32)
    large = max_exact + (jnp.log(nf / max_exact) / math.log(MAX_DISTANCE / max_exact)
                         * (N_BUCKETS - max_exact)).astype(jnp.int32)
    return jnp.where(n < max_exact, n, jnp.minimum(large, N_BUCKETS - 1))


def _dsa_stage(st, w_uk, w_uv, rel_bias, b, s):
    nqb = s // Q_BLOCK
    nh = N_ATT_HEADS
    proj = st["proj"].reshape(b, nqb, Q_BLOCK, PROJ_WIDTH)
    qt = jnp.swapaxes(proj[..., COL_Q_ATT:COL_Q_ATT + ATT_WIDTH], 2, 3)
    qi = proj[..., COL_Q_IDX:COL_Q_IDX + IDX_WIDTH].reshape(b, nqb, Q_BLOCK, IDX_HEADS, IDX_DIM)
    qit = jnp.transpose(qi, (0, 1, 4, 3, 2)).reshape(b, nqb, IDX_DIM, IDX_HEADS * Q_BLOCK)
    qit = jnp.pad(qit, ((0, 0), (0, 0), (0, LANES - IDX_DIM), (0, 0)))
    wi = st["widx"].reshape(b, nqb, Q_BLOCK, LANES)[..., IDX_DIM:IDX_DIM + IDX_HEADS]
    widx_t = jnp.swapaxes(wi, 2, 3).reshape(b, nqb, 1, IDX_HEADS * Q_BLOCK)
    front = ((0, 0), (KEY_PAD, 0), (0, 0))
    kidx_p = jnp.pad(st["kidx"].reshape(b, s, LANES), front)
    c_p = jnp.pad(st["c"].reshape(b, s, KV_LATENT), front)
    ct_p = jnp.swapaxes(c_p, 1, 2)
    wuk_h = jnp.transpose(w_uk, (1, 0, 2)).astype(BF16)
    wuvt_h = jnp.transpose(w_uv, (1, 2, 0)).astype(BF16)
    i = jnp.arange(2 * Q_BLOCK)[:, None]
    j = jnp.arange(Q_BLOCK)[None, :]
    rb = rel_bias - rel_bias[N_BUCKETS - 1]
    bias = rb[_t5_bucket(j - i + Q_BLOCK)]
    bias_t = jnp.transpose(bias, (0, 2, 1)).reshape(2 * Q_BLOCK, nh * Q_BLOCK)
    topk = min(IDX_TOPK_MAX, s // 4)
    yt = _dsa(kidx_p, qit, widx_t, c_p, ct_p, qt, wuk_h, wuvt_h, bias_t, b, s, topk)
    return jnp.swapaxes(yt, 1, 2).reshape(b * s, ATT_WIDTH)


def _memkv_body(m_ref, g_ref, wk_ref, wv_ref, k_ref, v_ref):
    mn = _rms(m_ref[0], g_ref[...]).astype(BF16)
    k_ref[0] = jnp.dot(mn, wk_ref[...], preferred_element_type=F32).astype(BF16)
    v_ref[0] = jnp.dot(mn, wv_ref[...], preferred_element_type=F32).astype(BF16)


def _memkv(mem, gain, w_ck, w_cv):
    b, m, d = mem.shape
    w = w_ck.shape[1]
    const = lambda i: (0, 0)
    out = jax.ShapeDtypeStruct((b, m, w), BF16)
    return pl.pallas_call(
        _memkv_body,
        out_shape=(out, out),
        grid=(b,),
        in_specs=[
            pl.BlockSpec((1, m, d), lambda i: (i, 0, 0)),
            pl.BlockSpec((1, d), const),
            pl.BlockSpec((d, w), const),
            pl.BlockSpec((d, w), const),
        ],
        out_specs=(pl.BlockSpec((1, m, w), lambda i: (i, 0, 0)), pl.BlockSpec((1, m, w), lambda i: (i, 0, 0))),
        compiler_params=_cparams(("arbitrary",)),
        name="memkv",
    )(mem, gain, w_ck, w_cv)


def _mid_body(x_ref, ya_ref, yr_ref, woa_ref, wob_ref, gc_ref, wq_ref, km_ref, vm_ref, wo_ref,
              gf_ref, wr_ref, br_ref, h2_ref, xn_ref, idx_ref, gate_ref):
    h1 = (x_ref[...]
          + jnp.dot(ya_ref[...], woa_ref[...], preferred_element_type=F32)
          + jnp.dot(yr_ref[...], wob_ref[...], preferred_element_type=F32))
    hn = _rms(h1, gc_ref[...]).astype(BF16)
    q = jnp.dot(hn, wq_ref[...], preferred_element_type=F32).astype(BF16)
    heads = []
    for h in range(N_CROSS_HEADS):
        sl = slice(h * CROSS_HEAD_DIM, (h + 1) * CROSS_HEAD_DIM)
        s = lax.dot_general(q[:, sl], km_ref[0, :, sl], (((1,), (1,)), ((), ())),
                            preferred_element_type=F32) * (CROSS_HEAD_DIM ** -0.5)
        p = jnp.exp(s - jnp.max(s, axis=-1, keepdims=True))
        p = p / jnp.sum(p, axis=-1, keepdims=True)
        heads.append(jnp.dot(p.astype(BF16), vm_ref[0, :, sl], preferred_element_type=F32).astype(BF16))
    o = jnp.concatenate(heads, axis=-1)
    h2 = h1 + jnp.dot(o, wo_ref[...], preferred_element_type=F32)
    h2_ref[...] = h2

    xn = _rms(h2, gf_ref[...])
    xn_ref[...] = xn
    logit = lax.dot_general(wr_ref[...], xn.astype(BF16), (((1,), (1,)), ((), ())),
                            preferred_element_type=F32) + br_ref[...]
    eid = lax.broadcasted_iota(jnp.int32, logit.shape, 0)
    vals, ids = [], []
    for _ in range(TOP_K):
        mx = jnp.max(logit, axis=0, keepdims=True)
        sel = jnp.min(jnp.where(logit == mx, eid, N_EXPERTS), axis=0, keepdims=True)
        vals.append(mx)
        ids.append(sel)
        logit = jnp.where(eid == sel, -jnp.inf, logit)
    top = jnp.concatenate(vals, axis=0)
    e = jnp.exp(top - vals[0])
    gate_ref[...] = e / jnp.sum(e, axis=0, keepdims=True)
    idx_ref[...] = jnp.concatenate(ids, axis=0)


def _mid(x2, ya, yr, w_out_a, w_out_b, g_cross, w_cq, km, vm, w_co, g_ffn, w_rt, b_r, b, s, tm):
    t, d = x2.shape
    nt = s // tm
    row = lambda bi, i: (bi * nt + i, 0)
    const = lambda bi, i: (0, 0)
    once = pl.Buffered(1)
    m = km.shape[1]
    return pl.pallas_call(
        _mid_body,
        out_shape=(
            jax.ShapeDtypeStruct((t, d), F32),
            jax.ShapeDtypeStruct((t, d), F32),
            jax.ShapeDtypeStruct((TOP_K, t), jnp.int32),
            jax.ShapeDtypeStruct((TOP_K, t), F32),
        ),
        grid=(b, nt),
        in_specs=[
            pl.BlockSpec((tm, d), row),
            pl.BlockSpec((tm, ATT_WIDTH), row),
            pl.BlockSpec((tm, RET_WIDTH), row),
            pl.BlockSpec((ATT_WIDTH, d), const, pipeline_mode=once),
            pl.BlockSpec((RET_WIDTH, d), const, pipeline_mode=once),
            pl.BlockSpec((1, d), const),
            pl.BlockSpec((d, CROSS_WIDTH), const, pipeline_mode=once),
            pl.BlockSpec((1, m, CROSS_WIDTH), lambda bi, i: (bi, 0, 0)),
            pl.BlockSpec((1, m, CROSS_WIDTH), lambda bi, i: (bi, 0, 0)),
            pl.BlockSpec((CROSS_WIDTH, d), const, pipeline_mode=once),
            pl.BlockSpec((1, d), const),
            pl.BlockSpec((N_EXPERTS, d), const),
            pl.BlockSpec((N_EXPERTS, 1), const),
        ],
        out_specs=(
            pl.BlockSpec((tm, d), row),
            pl.BlockSpec((tm, d), row),
            pl.BlockSpec((TOP_K, tm), lambda bi, i: (0, bi * nt + i)),
            pl.BlockSpec((TOP_K, tm), lambda bi, i: (0, bi * nt + i)),
        ),
        compiler_params=_cparams(("arbitrary", "arbitrary")),
        name="mid",
    )(x2, ya, yr, w_out_a, w_out_b, g_cross, w_cq, km, vm, w_co, g_ffn, w_rt, b_r)


def _experts_body(be_ref, nu_ref, idx_hbm, gate_ref, xn_hbm, wg_ref, wu_ref, bg_ref, bu_ref, wd_ref, bd_ref,
                  y_hbm, idx_smem, xbuf, xb_ref, acc_ref, sem_idx, sem_g, sem_s):
    i = pl.program_id(0)
    j = pl.program_id(1)
    n_used = nu_ref[0]
    rows = xb_ref.shape[0]

    def idx_copy(blk, slot):
        return pltpu.make_async_copy(idx_hbm.at[blk], idx_smem.at[slot], sem_idx.at[slot])

    def gather_start(islot, xslot):
        def body(r, carry):
            tok = idx_smem[islot, 0, r]
            pltpu.make_async_copy(xn_hbm.at[pl.ds(tok, 1)], xbuf.at[xslot, pl.ds(r, 1)], sem_g.at[xslot]).start()
            return carry
        lax.fori_loop(0, rows, body, 0, unroll=8)

    def gather_wait(xslot):
        pltpu.make_async_copy(xn_hbm.at[pl.ds(0, rows)], xbuf.at[xslot], sem_g.at[xslot]).wait()

    @pl.when(j == 0)
    def _():
        @pl.when(i == 0)
        def _():
            acc_ref[...] = jnp.zeros_like(acc_ref)
            dump = pltpu.make_async_copy(acc_ref, y_hbm.at[pl.ds(y_hbm.shape[0] - rows, rows)], sem_s)
            dump.start()
            dump.wait()
            idx_copy(0, 0).start()

            @pl.when(n_used > 1)
            def _():
                idx_copy(1, 1).start()

            idx_copy(0, 0).wait()
            gather_start(0, 0)

        @pl.when(i + 1 < n_used)
        def _():
            idx_copy(i + 1, (i + 1) % 3).wait()
            gather_start((i + 1) % 3, (i + 1) % 2)

        @pl.when(i + 2 < n_used)
        def _():
            idx_copy(i + 2, (i + 2) % 3).start()

        @pl.when(i < n_used)
        def _():
            gather_wait(i % 2)
            xb_ref[...] = xbuf[i % 2].astype(BF16)

    @pl.when(i < n_used)
    def _():
        xb = xb_ref[...]
        g = jnp.dot(xb, wg_ref[0], preferred_element_type=F32) + bg_ref[0]
        u = jnp.dot(xb, wu_ref[0], preferred_element_type=F32) + bu_ref[0]
        gt = jnp.minimum(g, SWIGLU_LIMIT)
        up = jnp.clip(u, -SWIGLU_LIMIT, SWIGLU_LIMIT)
        hm = ((up + 1.0) * (gt * jax.nn.sigmoid(SWIGLU_ALPHA * gt))).astype(BF16)
        part = jnp.dot(hm, wd_ref[0], preferred_element_type=F32)

        @pl.when(j == 0)
        def _():
            acc_ref[...] = part

        @pl.when(j > 0)
        def _():
            acc_ref[...] += part

        @pl.when(j == pl.num_programs(1) - 1)
        def _():
            acc_ref[...] = (acc_ref[...] + bd_ref[0]) * gate_ref[0]
            islot = i % 3

            def body(r, carry):
                dst = idx_smem[islot, 1, r]
                pltpu.make_async_copy(acc_ref.at[pl.ds(r, 1)], y_hbm.at[pl.ds(dst, 1)], sem_s).start()
                return carry
            lax.fori_loop(0, rows, body, 0, unroll=8)
            pltpu.make_async_copy(acc_ref, y_hbm.at[pl.ds(0, rows)], sem_s).wait()


def _experts(block_expert, n_used, slot_idx, slot_gate, xn, w_gu, b_gu, w_down, b_down):
    nb, _, rows = slot_idx.shape
    t, d = xn.shape
    e, _, ff2 = w_gu.shape
    ff = ff2 // 2
    tf = MOE_TF
    nj = ff // tf
    a_rows = t * TOP_K + rows
    grid_spec = pltpu.PrefetchScalarGridSpec(
        num_scalar_prefetch=2,
        grid=(nb, nj),
        in_specs=[
            pl.BlockSpec(memory_space=pl.ANY),
            pl.BlockSpec((1, rows, 1), lambda i, j, be, nu: (i, 0, 0)),
            pl.BlockSpec(memory_space=pl.ANY),
            pl.BlockSpec((1, d, tf), lambda i, j, be, nu: (be[i], 0, j)),
            pl.BlockSpec((1, d, tf), lambda i, j, be, nu: (be[i], 0, j + nj)),
            pl.BlockSpec((1, 1, tf), lambda i, j, be, nu: (be[i], 0, j)),
            pl.BlockSpec((1, 1, tf), lambda i, j, be, nu: (be[i], 0, j + nj)),
            pl.BlockSpec((1, tf, d), lambda i, j, be, nu: (be[i], j, 0)),
            pl.BlockSpec((1, 1, d), lambda i, j, be, nu: (be[i], 0, 0)),
        ],
        out_specs=pl.BlockSpec(memory_space=pl.ANY),
        scratch_shapes=[
            pltpu.SMEM((3, 2, rows), jnp.int32),
            pltpu.VMEM((2, rows, d), F32),
            pltpu.VMEM((rows, d), BF16),
            pltpu.VMEM((rows, d), F32),
            pltpu.SemaphoreType.DMA((3,)),
            pltpu.SemaphoreType.DMA((2,)),
            pltpu.SemaphoreType.DMA,
        ],
    )
    return pl.pallas_call(
        _experts_body,
        out_shape=jax.ShapeDtypeStruct((a_rows, d), F32),
        grid_spec=grid_spec,
        compiler_params=_cparams(("arbitrary", "arbitrary")),
        name="experts",
    )(block_expert, n_used, slot_idx, slot_gate, xn, w_gu, w_gu, b_gu, b_gu, w_down, b_down)


def _combine_body(final_norm, h_ref, y0_ref, y1_ref, y2_ref, y3_ref, g_ref, o_ref):
    acc = h_ref[...] + y0_ref[...] + y1_ref[...] + y2_ref[...] + y3_ref[...]
    o_ref[...] = _rms(acc, g_ref[...]) if final_norm else acc


def _combine(h2, y_tok, gain, tm):
    t, d = h2.shape
    nt = t // tm
    final_norm = gain is not None
    if gain is None:
        gain = jnp.ones((d,), F32)
    gain = gain.reshape(1, d)
    plane = lambda k: pl.BlockSpec((tm, d), lambda i: (k * nt + i, 0))
    return pl.pallas_call(
        functools.partial(_combine_body, final_norm),
        out_shape=jax.ShapeDtypeStruct((t, d), F32),
        grid=(nt,),
        in_specs=[pl.BlockSpec((tm, d), lambda i: (i, 0))] + [plane(k) for k in range(TOP_K)]
                 + [pl.BlockSpec((1, d), lambda i: (0, 0))],
        out_specs=pl.BlockSpec((tm, d), lambda i: (i, 0)),
        compiler_params=_cparams(("arbitrary",)),
        name="combine",
    )(h2, y_tok, y_tok, y_tok, y_tok, gain)


def _routing(top_idx, gates, rows):
    k, t = top_idx.shape
    a = t * k
    flat_e = top_idx.T.reshape(-1)
    flat_g = gates.T.reshape(-1)
    order = jnp.argsort(flat_e).astype(jnp.int32)
    counts = jnp.zeros((N_EXPERTS,), jnp.int32).at[flat_e].add(1)
    start = jnp.cumsum(counts) - counts
    padded = (counts + rows - 1) // rows * rows
    padded_end = jnp.cumsum(padded)
    padded_start = padded_end - padded
    nb = -(-a // rows) + N_EXPERTS
    n_used = (padded_end[-1] // rows).astype(jnp.int32)
    blk = jnp.arange(nb, dtype=jnp.int32)
    bexp = jnp.minimum(jnp.searchsorted(padded_end, blk * rows, side="right"), N_EXPERTS - 1).astype(jnp.int32)
    bexp = jnp.where(blk < n_used, bexp, bexp[jnp.maximum(n_used - 1, 0)])
    p = jnp.arange(nb * rows, dtype=jnp.int32)
    pe = jnp.repeat(bexp, rows)
    r = p - padded_start[pe]
    valid = (r < counts[pe]) & (p < padded_end[-1])
    asg = order[jnp.clip(start[pe] + r, 0, a - 1)]
    slot_src = jnp.where(valid, asg // k, 0)
    slot_dst = jnp.where(valid, (asg % k) * t + asg // k, a + p % rows)
    slot_gate = jnp.where(valid, flat_g[asg], 0.0)
    slot_idx = jnp.stack([slot_src.reshape(nb, rows), slot_dst.reshape(nb, rows)], axis=1)
    return bexp, n_used.reshape(1), slot_idx, slot_gate.reshape(nb, rows, 1)


def _moe_stage(h2, xn, top_idx, gates, w_gate_up, b_gate_up, w_down, b_down, norm_final):
    t, d = h2.shape
    bexp, n_used, slot_idx, slot_gate = _routing(top_idx, gates, MOE_ROWS)
    e = w_gate_up.shape[0]
    y_tok = _experts(bexp, n_used, slot_idx, slot_gate, xn, w_gate_up.astype(BF16),
                     b_gate_up.reshape(e, 1, -1), w_down.astype(BF16), b_down.reshape(e, 1, -1))
    return _combine(h2, y_tok, norm_final, min(256, t))


def _permute_w_in(w_in):
    sp = np.cumsum([0, ATT_WIDTH, KV_LATENT, IDX_WIDTH, IDX_DIM, IDX_HEADS, RET_QK_WIDTH, RET_QK_WIDTH,
                    RET_WIDTH, RET_WIDTH])
    piece = lambda i: w_in[:, int(sp[i]):int(sp[i + 1])]
    d = w_in.shape[0]
    cols = [piece(0), piece(2), piece(7), piece(8), piece(5), piece(6), piece(1), piece(3), piece(4)]
    used = sum(c.shape[1] for c in cols)
    cols.append(jnp.zeros((d, PROJ_WIDTH - used), w_in.dtype))
    return jnp.concatenate(cols, axis=1).astype(BF16)


def _rotary_tables(positions):
    half = RET_QK_DIM // 2
    inv = ROPE_BASE ** (-jnp.arange(0, RET_QK_DIM, 2, dtype=F32) / RET_QK_DIM)
    ang = positions.astype(F32).reshape(-1, 1) * inv
    cos, sin = jnp.cos(ang), jnp.sin(ang)
    return jnp.concatenate([cos, cos], axis=1), jnp.concatenate([-sin, sin], axis=1)


def _front(x, positions, norm_mix, w_in, kv_norm, idx_k_norm):
    b, s, d = x.shape
    t = b * s
    tm = min(1024, t)
    proj = _proj(x.reshape(t, d), norm_mix.reshape(1, d), _permute_w_in(w_in), tm)
    cos2, sin2 = _rotary_tables(positions)
    g_k = jnp.concatenate([idx_k_norm, jnp.zeros((LANES - IDX_DIM,), F32)]).reshape(1, LANES)
    c, kidx, widx, q_rot, k_rot = _prep(proj, cos2, sin2, kv_norm.reshape(1, KV_LATENT), g_k, tm)
    return dict(proj=proj, c=c, kidx=kidx, widx=widx, q_rot=q_rot, k_rot=k_rot)


def _mid_stage(x, mem, y_att, y_ret, w_out, norm_cross, norm_mem, w_cq, w_ck, w_cv, w_co, norm_ffn,
               w_router, b_router):
    b, s, d = x.shape
    km, vm = _memkv(mem, norm_mem.reshape(1, d), w_ck.astype(BF16), w_cv.astype(BF16))
    w_out_b16 = w_out.astype(BF16)
    return _mid(x.reshape(b * s, d), y_att, y_ret, w_out_b16[:ATT_WIDTH], w_out_b16[ATT_WIDTH:],
                norm_cross.reshape(1, d), w_cq.astype(BF16), km, vm, w_co.astype(BF16),
                norm_ffn.reshape(1, d), w_router.T.astype(BF16), b_router.reshape(N_EXPERTS, 1),
                b, s, min(512, s))


def _ret_stage(st, b, s):
    q_rot = st["q_rot"].reshape(b, s, RET_QK_WIDTH)
    kt_rot = jnp.swapaxes(st["k_rot"].reshape(b, s, RET_QK_WIDTH), 1, 2)
    return _retention(q_rot, kt_rot, st["proj"], b, s)


def kernel(x, mem, positions, norm_mix, w_in, kv_norm, idx_k_norm, w_uk, w_uv, rel_bias, w_out, norm_cross,
           norm_mem, w_cq, w_ck, w_cv, w_co, norm_ffn, w_router, b_router, w_gate_up, b_gate_up, w_down, b_down,
           norm_final):
    b, s, d = x.shape
    depth = norm_mix.shape[0]
    h = x
    for l in range(depth):
        st = _front(h, positions, norm_mix[l], w_in[l], kv_norm[l], idx_k_norm[l])
        y_att = _dsa_stage(st, w_uk[l], w_uv[l], rel_bias, b, s)
        y_ret = _ret_stage(st, b, s)
        h2, xn, top_idx, gates = _mid_stage(h, mem, y_att, y_ret, w_out[l], norm_cross[l], norm_mem[l], w_cq[l],
                                            w_ck[l], w_cv[l], w_co[l], norm_ffn[l], w_router[l], b_router[l])
        gain = norm_final if l == depth - 1 else None
        h = _moe_stage(h2, xn, top_idx, gates, w_gate_up[l], b_gate_up[l], w_down[l], b_down[l], gain).reshape(b, s, d)
    return h
```

```python
import functools
import math

import jax
import jax.numpy as jnp
import numpy as np
from jax import lax
from jax.experimental import pallas as pl
from jax.experimental.pallas import tpu as pltpu

F32 = jnp.float32
BF16 = jnp.bfloat16

D_MODEL = 2048
N_ATT_HEADS = 8
ATT_HEAD_DIM = 128
KV_LATENT = 256
IDX_HEADS = 16
IDX_DIM = 64
IDX_TOPK_MAX = 256
N_RET_HEADS = 4
RET_QK_DIM = 128
RET_V_DIM = 256
ROPE_BASE = 10000.0
N_BUCKETS = 32
MAX_DISTANCE = 128
N_CROSS_HEADS = 4
CROSS_HEAD_DIM = 128
N_EXPERTS = 32
TOP_K = 4
D_FF = D_MODEL
SWIGLU_LIMIT = 7.0
SWIGLU_ALPHA = 1.702
EPS = 1e-6

ATT_WIDTH = N_ATT_HEADS * ATT_HEAD_DIM
RET_WIDTH = N_RET_HEADS * RET_V_DIM
RET_QK_WIDTH = N_RET_HEADS * RET_QK_DIM
IDX_WIDTH = IDX_HEADS * IDX_DIM
CROSS_WIDTH = N_CROSS_HEADS * CROSS_HEAD_DIM

LANES = 128
VMEM_LIMIT = 56 * 1024 * 1024

COL_Q_ATT = 0
COL_Q_IDX = COL_Q_ATT + ATT_WIDTH
COL_V_RET = COL_Q_IDX + IDX_WIDTH
COL_G_RET = COL_V_RET + RET_WIDTH
COL_Q_RET = COL_G_RET + RET_WIDTH
COL_K_RET = COL_Q_RET + RET_QK_WIDTH
COL_C_KV = COL_K_RET + RET_QK_WIDTH
COL_KW = COL_C_KV + KV_LATENT
PROJ_TN = 512
PROJ_WIDTH = 5632

Q_BLOCK = 128
KEY_TILE = 512
KEY_PAD = KEY_TILE - Q_BLOCK
NEG_BIG = -1e30
LOG2E = math.log2(math.e)

RET_CHUNK = 256

MOE_ROWS = 512
MOE_TF = 1024


def _cparams(sem, vmem=VMEM_LIMIT):
    return pltpu.CompilerParams(dimension_semantics=sem, vmem_limit_bytes=vmem)


def _rms(x, g):
    return x * lax.rsqrt(jnp.mean(x * x, axis=-1, keepdims=True) + EPS) * g


ROW_PITCH = 20


def _store_rows(ref, x, pitch=None):
    n, d = x.shape
    w = d // LANES
    for a in range(w):
        ref[pl.ds(a, n, stride=pitch or w), :] = x[:, a * LANES:(a + 1) * LANES].astype(ref.dtype)


def _load_rows(ref, n, d, pitch=None):
    w = d // LANES
    return jnp.concatenate([ref[pl.ds(a, n, stride=pitch or w), :] for a in range(w)], axis=1)


def _proj_body(x_ref, g_ref, w_ref, o_ref, xn_ref):
    @pl.when(pl.program_id(1) == 0)
    def _():
        xn_ref[...] = _rms(x_ref[...], g_ref[...]).astype(BF16)

    o_ref[...] = jnp.dot(xn_ref[...], w_ref[...], preferred_element_type=F32).astype(o_ref.dtype)


def _proj(x2, gain, w_p, tm):
    t, d = x2.shape
    n = w_p.shape[1]
    return pl.pallas_call(
        _proj_body,
        out_shape=jax.ShapeDtypeStruct((t, n), BF16),
        grid=(t // tm, n // PROJ_TN),
        in_specs=[
            pl.BlockSpec((tm, d), lambda i, j: (i, 0)),
            pl.BlockSpec((1, d), lambda i, j: (0, 0)),
            pl.BlockSpec((d, PROJ_TN), lambda i, j: (0, j)),
        ],
        out_specs=pl.BlockSpec((tm, PROJ_TN), lambda i, j: (i, j)),
        scratch_shapes=[pltpu.VMEM((tm, d), BF16)],
        compiler_params=_cparams(("arbitrary", "arbitrary")),
        name="proj",
    )(x2, gain, w_p)


def _prep_body(c_ref, kw_ref, q_ref, k_ref, cos_ref, sin_ref, gkv_ref, gk_ref,
               c_out, kidx_out, widx_out, q_out, k_out):
    c_out[...] = _rms(c_ref[...].astype(F32), gkv_ref[...]).astype(BF16)

    kw = kw_ref[...].astype(F32)
    lane = lax.broadcasted_iota(jnp.int32, kw.shape, 1)
    is_k = lane < IDX_DIM
    kk = jnp.where(is_k, kw, 0.0)
    ms = jnp.sum(kk * kk, axis=-1, keepdims=True) * (1.0 / IDX_DIM)
    kidx_out[...] = (kk * lax.rsqrt(ms + EPS) * gk_ref[...]).astype(BF16)
    widx_out[...] = kw * (IDX_HEADS ** -0.5 * IDX_DIM ** -0.5)

    cos2 = cos_ref[...]
    sin2 = sin_ref[...]
    for h in range(N_RET_HEADS):
        sl = slice(h * RET_QK_DIM, (h + 1) * RET_QK_DIM)
        qh = q_ref[:, sl].astype(F32)
        kh = k_ref[:, sl].astype(F32)
        q_out[:, sl] = (qh * cos2 + pltpu.roll(qh, RET_QK_DIM // 2, 1) * sin2).astype(BF16)
        k_out[:, sl] = ((kh * cos2 + pltpu.roll(kh, RET_QK_DIM // 2, 1) * sin2)
                        * (RET_QK_DIM ** -0.5)).astype(BF16)


def _prep(proj, cos2, sin2, g_kv, g_k, tm):
    t = proj.shape[0]
    row = lambda i: (i, 0)
    return pl.pallas_call(
        _prep_body,
        out_shape=(
            jax.ShapeDtypeStruct((t, KV_LATENT), BF16),
            jax.ShapeDtypeStruct((t, LANES), BF16),
            jax.ShapeDtypeStruct((t, LANES), F32),
            jax.ShapeDtypeStruct((t, RET_QK_WIDTH), BF16),
            jax.ShapeDtypeStruct((t, RET_QK_WIDTH), BF16),
        ),
        grid=(t // tm,),
        in_specs=[
            pl.BlockSpec((tm, KV_LATENT), lambda i: (i, COL_C_KV // KV_LATENT)),
            pl.BlockSpec((tm, LANES), lambda i: (i, COL_KW // LANES)),
            pl.BlockSpec((tm, RET_QK_WIDTH), lambda i: (i, COL_Q_RET // RET_QK_WIDTH)),
            pl.BlockSpec((tm, RET_QK_WIDTH), lambda i: (i, COL_K_RET // RET_QK_WIDTH)),
            pl.BlockSpec((tm, LANES), row),
            pl.BlockSpec((tm, LANES), row),
            pl.BlockSpec((1, KV_LATENT), lambda i: (0, 0)),
            pl.BlockSpec((1, LANES), lambda i: (0, 0)),
        ],
        out_specs=(
            pl.BlockSpec((tm, KV_LATENT), row),
            pl.BlockSpec((tm, LANES), row),
            pl.BlockSpec((tm, LANES), row),
            pl.BlockSpec((tm, RET_QK_WIDTH), row),
            pl.BlockSpec((tm, RET_QK_WIDTH), row),
        ),
        compiler_params=_cparams(("arbitrary",)),
        name="prep",
    )(proj, proj, proj, proj, cos2, sin2, g_kv, g_k)


def _ret_body(q_ref, kt_ref, v_ref, g_ref, dec_ref, cross_ref, state_ref, cd_ref, o_ref, r_ref):
    @pl.when(pl.program_id(1) == 0)
    def _():
        r_ref[...] = jnp.zeros_like(r_ref)

    for h in range(N_RET_HEADS):
        qs = slice(h * RET_QK_DIM, (h + 1) * RET_QK_DIM)
        vs = slice(h * RET_V_DIM, (h + 1) * RET_V_DIM)
        q = q_ref[0, :, qs]
        kt = kt_ref[0, qs, :]
        v = v_ref[:, vs]
        r_old = r_ref[h]
        inner = jnp.dot(q, kt, preferred_element_type=F32) * dec_ref[h]
        qc = (q.astype(F32) * cross_ref[h]).astype(BF16)
        o = (jnp.dot(inner.astype(BF16), v, preferred_element_type=F32)
             + jnp.dot(qc, r_old.astype(BF16), preferred_element_type=F32))
        ks = (kt.astype(F32) * state_ref[h]).astype(BF16)
        r_ref[h] = r_old * cd_ref[h] + jnp.dot(ks, v, preferred_element_type=F32)
        o = o * lax.rsqrt(jnp.mean(o * o, axis=-1, keepdims=True) + EPS)
        g = g_ref[:, vs].astype(F32)
        o_ref[:, vs] = (g * jax.nn.sigmoid(g) * o).astype(o_ref.dtype)


def _retention(q_rot, kt_rot, proj, b, s):
    c = RET_CHUNK
    n = s // c
    hh = N_RET_HEADS
    log_gamma = jnp.log(1.0 - 2.0 ** (-5.0 - jnp.arange(hh, dtype=F32)))
    j = jnp.arange(c, dtype=F32)
    diff = j[:, None] - j[None, :]
    dec = jnp.where(diff >= 0, jnp.exp(log_gamma[:, None, None] * jnp.maximum(diff, 0.0)), 0.0)
    cross = jnp.broadcast_to(jnp.exp(log_gamma[:, None] * (j + 1.0))[:, :, None], (hh, c, RET_QK_DIM))
    state = jnp.exp(log_gamma[:, None] * (c - 1.0 - j))[:, None, :]
    cdec = jnp.broadcast_to(jnp.exp(log_gamma * c)[:, None, None], (hh, 1, RET_V_DIM))
    const = lambda bi, ci: (0, 0, 0)
    return pl.pallas_call(
        _ret_body,
        out_shape=jax.ShapeDtypeStruct((b * s, RET_WIDTH), BF16),
        grid=(b, n),
        in_specs=[
            pl.BlockSpec((1, c, RET_QK_WIDTH), lambda bi, ci: (bi, ci, 0)),
            pl.BlockSpec((1, RET_QK_WIDTH, c), lambda bi, ci: (bi, 0, ci)),
            pl.BlockSpec((c, RET_WIDTH), lambda bi, ci: (bi * n + ci, COL_V_RET // RET_WIDTH)),
            pl.BlockSpec((c, RET_WIDTH), lambda bi, ci: (bi * n + ci, COL_G_RET // RET_WIDTH)),
            pl.BlockSpec((hh, c, c), const),
            pl.BlockSpec((hh, c, RET_QK_DIM), const),
            pl.BlockSpec((hh, 1, c), const),
            pl.BlockSpec((hh, 1, RET_V_DIM), const),
        ],
        out_specs=pl.BlockSpec((c, RET_WIDTH), lambda bi, ci: (bi * n + ci, 0)),
        scratch_shapes=[pltpu.VMEM((hh, RET_QK_DIM, RET_V_DIM), F32)],
        compiler_params=_cparams(("arbitrary", "arbitrary")),
        name="ret",
    )(q_rot, kt_rot, proj, proj, dec, cross, state, cdec)


def _dsa_body(topk, kidx_ref, qit_ref, w_ref, c_ref, ct_ref, qt_ref, wuk_ref, wuvt_ref, bias_ref, y_ref,
              sc_ref, qlt_ref, acc_ref, m_ref, l_ref):
    qb = pl.program_id(1)
    tk = KEY_TILE
    half = tk // 2
    e_pad = (qb + 1) * Q_BLOCK + KEY_PAD
    n_tiles = e_pad // tk
    nh = N_ATT_HEADS
    lane_q = qb * Q_BLOCK + lax.broadcasted_iota(jnp.int32, (1, Q_BLOCK), 1)

    def tile_start(j):
        return pl.multiple_of(e_pad - (j + 1) * tk, Q_BLOCK)

    for h in range(nh):
        sl = slice(h * Q_BLOCK, (h + 1) * Q_BLOCK)
        hd = slice(h * ATT_HEAD_DIM, (h + 1) * ATT_HEAD_DIM)
        qlt_ref[:, sl] = (jnp.dot(wuk_ref[h], qt_ref[0, 0, hd, :], preferred_element_type=F32)
                          * (ATT_HEAD_DIM ** -0.5 * LOG2E)).astype(BF16)

    def score_tile(j, carry):
        r0 = tile_start(j)
        for part in range(2):
            rs = pl.multiple_of(r0 + part * half, Q_BLOCK)
            z = jnp.dot(kidx_ref[0, pl.ds(rs, half), :], qit_ref[0, 0], preferred_element_type=F32)
            tot = jnp.zeros((half, Q_BLOCK), F32)
            for h in range(IDX_HEADS):
                sl = slice(h * Q_BLOCK, (h + 1) * Q_BLOCK)
                tot = tot + jnp.maximum(z[:, sl], 0.0) * w_ref[0, 0, :, sl]
            key = rs - KEY_PAD + lax.broadcasted_iota(jnp.int32, (half, Q_BLOCK), 0)
            ok = (key >= 0) & (key <= lane_q)
            sc_ref[pl.ds(rs, half), :] = jnp.where(ok, tot, NEG_BIG)
        return carry
    lax.fori_loop(0, n_tiles, score_tile, 0)

    def minmax_tile(j, carry):
        lo, hi = carry
        s = sc_ref[pl.ds(tile_start(j), tk), :]
        hi = jnp.maximum(hi, jnp.max(s, axis=0, keepdims=True))
        lo = jnp.minimum(lo, jnp.min(jnp.where(s > 0.5 * NEG_BIG, s, -NEG_BIG), axis=0, keepdims=True))
        return lo, hi
    lo0, hi0 = lax.fori_loop(0, n_tiles, minmax_tile,
                             (jnp.full((1, Q_BLOCK), -NEG_BIG, F32), jnp.full((1, Q_BLOCK), NEG_BIG, F32)))
    search = lane_q + 1 > topk

    n_chain = 4
    rows_chain = tk // n_chain

    def count_ge(x):
        def body(j, cnts):
            r0 = tile_start(j)
            out = []
            for k in range(n_chain):
                s = sc_ref[pl.ds(pl.multiple_of(r0 + k * rows_chain, Q_BLOCK), rows_chain), :]
                ind = jnp.where(s >= x, 1.0, 0.0).reshape(rows_chain // 8, 8, Q_BLOCK)
                out.append(cnts[k] + jnp.sum(ind, axis=0))
            return tuple(out)
        cnts = lax.fori_loop(0, n_tiles, body, tuple(jnp.zeros((8, Q_BLOCK), F32) for _ in range(n_chain)))
        return jnp.sum((cnts[0] + cnts[1]) + (cnts[2] + cnts[3]), axis=0, keepdims=True)

    def bis_cond(st):
        it, _, _, _, active = st
        return (it < 40) & (active > 0.0)

    def bis_body(st):
        it, lo, hi, done, _ = st
        still_open = jnp.max(1.0 - done)
        mid = lo + 0.5 * (hi - lo)
        cnt = count_ge(mid)
        ge = cnt >= topk
        open_ = done < 0.5
        lo = jnp.where(open_ & ge, mid, lo)
        hi = jnp.where(open_ & (~ge), mid, hi)
        done = jnp.where(cnt == topk, 1.0, done)
        return it + 1, lo, hi, done, still_open

    done0 = jnp.where(search, 0.0, 1.0)
    _, lo_f, _, _, _ = lax.while_loop(bis_cond, bis_body, (jnp.int32(0), lo0, hi0, done0, jnp.max(1.0 - done0)))
    thr = jnp.where(search, lo_f, 0.5 * NEG_BIG)

    m_ref[...] = jnp.full(m_ref.shape, NEG_BIG, F32)
    l_ref[...] = jnp.zeros(l_ref.shape, F32)
    acc_ref[...] = jnp.zeros(acc_ref.shape, F32)

    n_grp = 4
    hg = nh // n_grp

    def attend(rs, n, with_bias):
        c_tile = c_ref[0, pl.ds(rs, n), :]
        ct_tile = ct_ref[0, :, pl.ds(rs, n)]
        pen = jnp.where(sc_ref[pl.ds(rs, n), :] >= thr, 0.0, NEG_BIG)
        pen = jnp.concatenate([pen] * hg, axis=1)
        scores = []
        for g in range(n_grp):
            ls = slice(g * hg * Q_BLOCK, (g + 1) * hg * Q_BLOCK)
            st = jnp.dot(c_tile, qlt_ref[:, ls], preferred_element_type=F32) + pen
            if with_bias:
                st = st + bias_ref[:, ls]
            scores.append(st)
        for g in range(n_grp):
            ls = slice(g * hg * Q_BLOCK, (g + 1) * hg * Q_BLOCK)
            st = scores[g]
            m_old = m_ref[:, ls]
            m_new = jnp.maximum(m_old, jnp.max(st, axis=0, keepdims=True))
            alpha = jnp.exp2(m_old - m_new)
            p = jnp.exp2(st - m_new)
            l_ref[:, ls] = alpha * l_ref[:, ls] + jnp.sum(p, axis=0, keepdims=True)
            acc_ref[:, ls] = acc_ref[:, ls] * alpha + jnp.dot(ct_tile, p.astype(BF16), preferred_element_type=F32)
            m_ref[:, ls] = m_new

    r_first = tile_start(0)
    attend(pl.multiple_of(r_first + half, Q_BLOCK), half, True)
    attend(r_first, half, False)

    def attend_tile(j, carry):
        attend(tile_start(j), tk, False)
        return carry
    lax.fori_loop(1, n_tiles, attend_tile, 0)

    ot = acc_ref[...] / l_ref[...]
    for h in range(nh):
        sl = slice(h * Q_BLOCK, (h + 1) * Q_BLOCK)
        y_ref[0, h * ATT_HEAD_DIM:(h + 1) * ATT_HEAD_DIM, :] = jnp.dot(
            wuvt_ref[h], ot[:, sl].astype(BF16), preferred_element_type=F32).astype(y_ref.dtype)


def _dsa(kidx_p, qit, widx_t, c_p, ct_p, qt, wuk_h, wuvt_h, bias_t, b, s, topk):
    nqb = s // Q_BLOCK
    sp = s + KEY_PAD
    nh = N_ATT_HEADS
    per_b = lambda bi, qi: (bi, 0, 0)
    per_q = lambda bi, qi: (bi, qi, 0, 0)
    c3 = lambda bi, qi: (0, 0, 0)
    return pl.pallas_call(
        functools.partial(_dsa_body, topk),
        out_shape=jax.ShapeDtypeStruct((b, ATT_WIDTH, s), BF16),
        grid=(b, nqb),
        in_specs=[
            pl.BlockSpec((1, sp, LANES), per_b),
            pl.BlockSpec((1, 1, LANES, IDX_HEADS * Q_BLOCK), per_q),
            pl.BlockSpec((1, 1, 1, IDX_HEADS * Q_BLOCK), per_q),
            pl.BlockSpec((1, sp, KV_LATENT), per_b),
            pl.BlockSpec((1, KV_LATENT, sp), per_b),
            pl.BlockSpec((1, 1, ATT_WIDTH, Q_BLOCK), per_q),
            pl.BlockSpec((nh, KV_LATENT, ATT_HEAD_DIM), c3),
            pl.BlockSpec((nh, ATT_HEAD_DIM, KV_LATENT), c3),
            pl.BlockSpec((2 * Q_BLOCK, nh * Q_BLOCK), lambda bi, qi: (0, 0)),
        ],
        out_specs=pl.BlockSpec((1, ATT_WIDTH, Q_BLOCK), lambda bi, qi: (bi, 0, qi)),
        scratch_shapes=[
            pltpu.VMEM((sp, Q_BLOCK), F32),
            pltpu.VMEM((KV_LATENT, nh * Q_BLOCK), BF16),
            pltpu.VMEM((KV_LATENT, nh * Q_BLOCK), F32),
            pltpu.VMEM((1, nh * Q_BLOCK), F32),
            pltpu.VMEM((1, nh * Q_BLOCK), F32),
        ],
        compiler_params=_cparams(("arbitrary", "arbitrary")),
        name="dsa",
    )(kidx_p, qit, widx_t, c_p, ct_p, qt, wuk_h, wuvt_h, bias_t)


def _t5_bucket(dist):
    n = jnp.maximum(dist, 0)
    max_exact = N_BUCKETS // 2
    nf = jnp.maximum(n, 1).astype(F32)
    large = max_exact + (jnp.log(nf / max_exact) / math.log(MAX_DISTANCE / max_exact)
                         * (N_BUCKETS - max_exact)).astype(jnp.int32)
    return jnp.where(n < max_exact, n, jnp.minimum(large, N_BUCKETS - 1))


def _dsa_stage(st, w_uk, w_uv, rel_bias, b, s):
    nqb = s // Q_BLOCK
    nh = N_ATT_HEADS
    proj = st["proj"].reshape(b, nqb, Q_BLOCK, PROJ_WIDTH)
    qt = jnp.swapaxes(proj[..., COL_Q_ATT:COL_Q_ATT + ATT_WIDTH], 2, 3)
    qi = proj[..., COL_Q_IDX:COL_Q_IDX + IDX_WIDTH].reshape(b, nqb, Q_BLOCK, IDX_HEADS, IDX_DIM)
    qit = jnp.transpose(qi, (0, 1, 4, 3, 2)).reshape(b, nqb, IDX_DIM, IDX_HEADS * Q_BLOCK)
    qit = jnp.pad(qit, ((0, 0), (0, 0), (0, LANES - IDX_DIM), (0, 0)))
    wi = st["widx"].reshape(b, nqb, Q_BLOCK, LANES)[..., IDX_DIM:IDX_DIM + IDX_HEADS]
    widx_t = jnp.swapaxes(wi, 2, 3).reshape(b, nqb, 1, IDX_HEADS * Q_BLOCK)
    front = ((0, 0), (KEY_PAD, 0), (0, 0))
    kidx_p = jnp.pad(st["kidx"].reshape(b, s, LANES), front)
    c_p = jnp.pad(st["c"].reshape(b, s, KV_LATENT), front)
    ct_p = jnp.swapaxes(c_p, 1, 2)
    wuk_h = jnp.transpose(w_uk, (1, 0, 2)).astype(BF16)
    wuvt_h = jnp.transpose(w_uv, (1, 2, 0)).astype(BF16)
    i = jnp.arange(2 * Q_BLOCK)[:, None]
    j = jnp.arange(Q_BLOCK)[None, :]
    rb = (rel_bias - rel_bias[N_BUCKETS - 1]) * LOG2E
    bias = rb[_t5_bucket(j - i + Q_BLOCK)]
    bias_t = jnp.transpose(bias, (0, 2, 1)).reshape(2 * Q_BLOCK, nh * Q_BLOCK)
    topk = min(IDX_TOPK_MAX, s // 4)
    yt = _dsa(kidx_p, qit, widx_t, c_p, ct_p, qt, wuk_h, wuvt_h, bias_t, b, s, topk)
    return jnp.swapaxes(yt, 1, 2).reshape(b * s, ATT_WIDTH)


def _memkv_body(m_ref, g_ref, wk_ref, wv_ref, k_ref, v_ref):
    mn = _rms(m_ref[0], g_ref[...]).astype(BF16)
    k_ref[0] = jnp.dot(mn, wk_ref[...], preferred_element_type=F32).astype(BF16)
    v_ref[0] = jnp.dot(mn, wv_ref[...], preferred_element_type=F32).astype(BF16)


def _memkv(mem, gain, w_ck, w_cv):
    b, m, d = mem.shape
    w = w_ck.shape[1]
    const = lambda i: (0, 0)
    out = jax.ShapeDtypeStruct((b, m, w), BF16)
    return pl.pallas_call(
        _memkv_body,
        out_shape=(out, out),
        grid=(b,),
        in_specs=[
            pl.BlockSpec((1, m, d), lambda i: (i, 0, 0)),
            pl.BlockSpec((1, d), const),
            pl.BlockSpec((d, w), const),
            pl.BlockSpec((d, w), const),
        ],
        out_specs=(pl.BlockSpec((1, m, w), lambda i: (i, 0, 0)), pl.BlockSpec((1, m, w), lambda i: (i, 0, 0))),
        compiler_params=_cparams(("arbitrary",)),
        name="memkv",
    )(mem, gain, w_ck, w_cv)


def _mid_body(x_ref, ya_ref, yr_ref, woa_ref, wob_ref, gc_ref, wq_ref, km_ref, vm_ref, wo_ref,
              gf_ref, wr_ref, br_ref, h2_ref, xn_ref, idx_ref, gate_ref):
    h1 = (x_ref[...]
          + jnp.dot(ya_ref[...], woa_ref[...], preferred_element_type=F32)
          + jnp.dot(yr_ref[...], wob_ref[...], preferred_element_type=F32))
    hn = _rms(h1, gc_ref[...]).astype(BF16)
    q = jnp.dot(hn, wq_ref[...], preferred_element_type=F32).astype(BF16)
    heads = []
    for h in range(N_CROSS_HEADS):
        sl = slice(h * CROSS_HEAD_DIM, (h + 1) * CROSS_HEAD_DIM)
        s = lax.dot_general(q[:, sl], km_ref[0, :, sl], (((1,), (1,)), ((), ())),
                            preferred_element_type=F32) * (CROSS_HEAD_DIM ** -0.5)
        p = jnp.exp(s - jnp.max(s, axis=-1, keepdims=True))
        p = p / jnp.sum(p, axis=-1, keepdims=True)
        heads.append(jnp.dot(p.astype(BF16), vm_ref[0, :, sl], preferred_element_type=F32).astype(BF16))
    o = jnp.concatenate(heads, axis=-1)
    h2 = h1 + jnp.dot(o, wo_ref[...], preferred_element_type=F32)
    h2_ref[...] = h2

    xn = _rms(h2, gf_ref[...])
    _store_rows(xn_ref, xn)
    logit = lax.dot_general(wr_ref[...], xn.astype(BF16), (((1,), (1,)), ((), ())),
                            preferred_element_type=F32) + br_ref[...]
    eid = lax.broadcasted_iota(jnp.int32, logit.shape, 0)
    vals, ids = [], []
    for _ in range(TOP_K):
        mx = jnp.max(logit, axis=0, keepdims=True)
        sel = jnp.min(jnp.where(logit == mx, eid, N_EXPERTS), axis=0, keepdims=True)
        vals.append(mx)
        ids.append(sel)
        logit = jnp.where(eid == sel, -jnp.inf, logit)
    top = jnp.concatenate(vals, axis=0)
    e = jnp.exp(top - vals[0])
    gate_ref[...] = e / jnp.sum(e, axis=0, keepdims=True)
    idx_ref[...] = jnp.concatenate(ids, axis=0)


def _mid(x2, ya, yr, w_out_a, w_out_b, g_cross, w_cq, km, vm, w_co, g_ffn, w_rt, b_r, b, s, tm):
    t, d = x2.shape
    nt = s // tm
    row = lambda bi, i: (bi * nt + i, 0)
    const = lambda bi, i: (0, 0)
    once = pl.Buffered(1)
    m = km.shape[1]
    return pl.pallas_call(
        _mid_body,
        out_shape=(
            jax.ShapeDtypeStruct((t, d), F32),
            jax.ShapeDtypeStruct((t * (d // LANES), LANES), F32),
            jax.ShapeDtypeStruct((TOP_K, t), jnp.int32),
            jax.ShapeDtypeStruct((TOP_K, t), F32),
        ),
        grid=(b, nt),
        in_specs=[
            pl.BlockSpec((tm, d), row),
            pl.BlockSpec((tm, ATT_WIDTH), row),
            pl.BlockSpec((tm, RET_WIDTH), row),
            pl.BlockSpec((ATT_WIDTH, d), const, pipeline_mode=once),
            pl.BlockSpec((RET_WIDTH, d), const, pipeline_mode=once),
            pl.BlockSpec((1, d), const),
            pl.BlockSpec((d, CROSS_WIDTH), const, pipeline_mode=once),
            pl.BlockSpec((1, m, CROSS_WIDTH), lambda bi, i: (bi, 0, 0)),
            pl.BlockSpec((1, m, CROSS_WIDTH), lambda bi, i: (bi, 0, 0)),
            pl.BlockSpec((CROSS_WIDTH, d), const, pipeline_mode=once),
            pl.BlockSpec((1, d), const),
            pl.BlockSpec((N_EXPERTS, d), const),
            pl.BlockSpec((N_EXPERTS, 1), const),
        ],
        out_specs=(
            pl.BlockSpec((tm, d), row),
            pl.BlockSpec((tm * (d // LANES), LANES), row),
            pl.BlockSpec((TOP_K, tm), lambda bi, i: (0, bi * nt + i)),
            pl.BlockSpec((TOP_K, tm), lambda bi, i: (0, bi * nt + i)),
        ),
        compiler_params=_cparams(("arbitrary", "arbitrary")),
        name="mid",
    )(x2, ya, yr, w_out_a, w_out_b, g_cross, w_cq, km, vm, w_co, g_ffn, w_rt, b_r)


def _experts_body(be_ref, nu_ref, idx_hbm, gate_ref, xn_hbm, wg_ref, wu_ref, bg_ref, bu_ref, wd_ref, bd_ref,
                  y_hbm, idx_smem, xbuf, xb_ref, acc_ref, ybuf, sem_idx, sem_g, sem_s):
    i = pl.program_id(0)
    j = pl.program_id(1)
    n_used = nu_ref[0]
    rows, d = xb_ref.shape
    w = d // LANES

    def idx_copy(blk, slot):
        return pltpu.make_async_copy(idx_hbm.at[blk],
                                     idx_smem.at[pl.ds(pl.multiple_of(slot * 2 * rows, 2 * rows), 2 * rows)],
                                     sem_idx.at[slot])

    def gather_start(islot, xslot):
        def body(r, carry):
            src = pl.multiple_of(idx_smem[islot * 2 * rows + r], w)
            pltpu.make_async_copy(xn_hbm.at[pl.ds(src, w)], xbuf.at[xslot, pl.ds(r * ROW_PITCH, w)],
                                  sem_g.at[xslot]).start()
            return carry
        lax.fori_loop(0, rows, body, 0, unroll=8)

    def gather_wait(xslot):
        pltpu.make_async_copy(xn_hbm.at[pl.ds(0, rows * w)], xbuf.at[xslot, pl.ds(0, rows * w)],
                              sem_g.at[xslot]).wait()

    def scatter_wait():
        pltpu.make_async_copy(ybuf.at[pl.ds(0, rows * w)], y_hbm.at[pl.ds(0, rows * w)], sem_s).wait()

    @pl.when(j == 0)
    def _():
        @pl.when(i == 0)
        def _():
            ybuf[...] = jnp.zeros_like(ybuf)
            dump = pltpu.make_async_copy(ybuf.at[pl.ds(0, rows * w)],
                                         y_hbm.at[pl.ds(y_hbm.shape[0] - rows * w, rows * w)], sem_s)
            dump.start()
            dump.wait()
            idx_copy(0, 0).start()

            @pl.when(n_used > 1)
            def _():
                idx_copy(1, 1).start()

            idx_copy(0, 0).wait()
            gather_start(0, 0)

        @pl.when(i + 1 < n_used)
        def _():
            idx_copy(i + 1, (i + 1) % 3).wait()
            gather_start((i + 1) % 3, (i + 1) % 2)

        @pl.when(i + 2 < n_used)
        def _():
            idx_copy(i + 2, (i + 2) % 3).start()

        @pl.when(i < n_used)
        def _():
            gather_wait(i % 2)
            xb_ref[...] = _load_rows(xbuf.at[i % 2], rows, d, ROW_PITCH).astype(BF16)

    @pl.when(i < n_used)
    def _():
        xb = xb_ref[...]
        g = jnp.dot(xb, wg_ref[0], preferred_element_type=F32) + bg_ref[0]
        u = jnp.dot(xb, wu_ref[0], preferred_element_type=F32) + bu_ref[0]
        gt = jnp.minimum(g, SWIGLU_LIMIT)
        up = jnp.clip(u, -SWIGLU_LIMIT, SWIGLU_LIMIT)
        hm = ((up + 1.0) * (gt * jax.nn.sigmoid(SWIGLU_ALPHA * gt))).astype(BF16)
        part = jnp.dot(hm, wd_ref[0], preferred_element_type=F32)

        @pl.when(j == 0)
        def _():
            acc_ref[...] = part

        @pl.when(j > 0)
        def _():
            acc_ref[...] += part

        @pl.when(j == pl.num_programs(1) - 1)
        def _():
            y = (acc_ref[...] + bd_ref[0]) * gate_ref[0]

            @pl.when(i > 0)
            def _():
                scatter_wait()

            _store_rows(ybuf, y, ROW_PITCH)
            islot = i % 3

            def body(r, carry):
                dst = pl.multiple_of(idx_smem[islot * 2 * rows + rows + r], w)
                pltpu.make_async_copy(ybuf.at[pl.ds(r * ROW_PITCH, w)], y_hbm.at[pl.ds(dst, w)], sem_s).start()
                return carry
            lax.fori_loop(0, rows, body, 0, unroll=8)

            @pl.when(i == n_used - 1)
            def _():
                scatter_wait()


def _experts(block_expert, n_used, slot_idx, slot_gate, xn, w_gu, b_gu, w_down, b_down):
    nb = slot_idx.shape[0]
    rows = slot_idx.shape[1] // 2
    e, d, ff2 = w_gu.shape
    w = d // LANES
    t = xn.shape[0] // w
    ff = ff2 // 2
    tf = MOE_TF
    nj = ff // tf
    a_rows = t * TOP_K + rows
    grid_spec = pltpu.PrefetchScalarGridSpec(
        num_scalar_prefetch=2,
        grid=(nb, nj),
        in_specs=[
            pl.BlockSpec(memory_space=pl.ANY),
            pl.BlockSpec((1, rows, 1), lambda i, j, be, nu: (i, 0, 0)),
            pl.BlockSpec(memory_space=pl.ANY),
            pl.BlockSpec((1, d, tf), lambda i, j, be, nu: (be[i], 0, j)),
            pl.BlockSpec((1, d, tf), lambda i, j, be, nu: (be[i], 0, j + nj)),
            pl.BlockSpec((1, 1, tf), lambda i, j, be, nu: (be[i], 0, j)),
            pl.BlockSpec((1, 1, tf), lambda i, j, be, nu: (be[i], 0, j + nj)),
            pl.BlockSpec((1, tf, d), lambda i, j, be, nu: (be[i], j, 0)),
            pl.BlockSpec((1, 1, d), lambda i, j, be, nu: (be[i], 0, 0)),
        ],
        out_specs=pl.BlockSpec(memory_space=pl.ANY),
        scratch_shapes=[
            pltpu.SMEM((3 * 2 * rows,), jnp.int32),
            pltpu.VMEM((2, rows * ROW_PITCH, LANES), F32),
            pltpu.VMEM((rows, d), BF16),
            pltpu.VMEM((rows, d), F32),
            pltpu.VMEM((rows * ROW_PITCH, LANES), F32),
            pltpu.SemaphoreType.DMA((3,)),
            pltpu.SemaphoreType.DMA((2,)),
            pltpu.SemaphoreType.DMA,
        ],
    )
    return pl.pallas_call(
        _experts_body,
        out_shape=jax.ShapeDtypeStruct((a_rows * w, LANES), F32),
        grid_spec=grid_spec,
        compiler_params=_cparams(("arbitrary", "arbitrary")),
        name="experts",
    )(block_expert, n_used, slot_idx, slot_gate, xn, w_gu, w_gu, b_gu, b_gu, w_down, b_down)


def _combine_body(final_norm, h_ref, y0_ref, y1_ref, y2_ref, y3_ref, g_ref, o_ref):
    n, d = h_ref.shape
    acc = h_ref[...] + _load_rows(y0_ref, n, d) + _load_rows(y1_ref, n, d) + _load_rows(y2_ref, n, d) \
        + _load_rows(y3_ref, n, d)
    o_ref[...] = _rms(acc, g_ref[...]) if final_norm else acc


def _combine(h2, y_tok, gain, tm):
    t, d = h2.shape
    w = d // LANES
    nt = t // tm
    final_norm = gain is not None
    if gain is None:
        gain = jnp.ones((d,), F32)
    gain = gain.reshape(1, d)
    plane = lambda k: pl.BlockSpec((tm * w, LANES), lambda i: (k * nt + i, 0))
    return pl.pallas_call(
        functools.partial(_combine_body, final_norm),
        out_shape=jax.ShapeDtypeStruct((t, d), F32),
        grid=(nt,),
        in_specs=[pl.BlockSpec((tm, d), lambda i: (i, 0))] + [plane(k) for k in range(TOP_K)]
                 + [pl.BlockSpec((1, d), lambda i: (0, 0))],
        out_specs=pl.BlockSpec((tm, d), lambda i: (i, 0)),
        compiler_params=_cparams(("arbitrary",)),
        name="combine",
    )(h2, y_tok, y_tok, y_tok, y_tok, gain)


def _routing(top_idx, gates, rows, row_width):
    k, t = top_idx.shape
    a = t * k
    flat_e = top_idx.T.reshape(-1)
    flat_g = gates.T.reshape(-1)
    order = jnp.argsort(flat_e).astype(jnp.int32)
    counts = jnp.zeros((N_EXPERTS,), jnp.int32).at[flat_e].add(1)
    start = jnp.cumsum(counts) - counts
    padded = (counts + rows - 1) // rows * rows
    padded_end = jnp.cumsum(padded)
    padded_start = padded_end - padded
    nb = -(-a // rows) + N_EXPERTS
    n_used = (padded_end[-1] // rows).astype(jnp.int32)
    blk = jnp.arange(nb, dtype=jnp.int32)
    bexp = jnp.minimum(jnp.searchsorted(padded_end, blk * rows, side="right"), N_EXPERTS - 1).astype(jnp.int32)
    bexp = jnp.where(blk < n_used, bexp, bexp[jnp.maximum(n_used - 1, 0)])
    p = jnp.arange(nb * rows, dtype=jnp.int32)
    pe = jnp.repeat(bexp, rows)
    r = p - padded_start[pe]
    valid = (r < counts[pe]) & (p < padded_end[-1])
    asg = order[jnp.clip(start[pe] + r, 0, a - 1)]
    slot_src = jnp.where(valid, asg // k, 0)
    slot_dst = jnp.where(valid, (asg % k) * t + asg // k, a + p % rows)
    slot_gate = jnp.where(valid, flat_g[asg], 0.0)
    slot_idx = jnp.concatenate([slot_src.reshape(nb, rows), slot_dst.reshape(nb, rows)], axis=1) * row_width
    return bexp, n_used.reshape(1), slot_idx, slot_gate.reshape(nb, rows, 1)


def _moe_stage(h2, xn, top_idx, gates, w_gate_up, b_gate_up, w_down, b_down, norm_final):
    t, d = h2.shape
    bexp, n_used, slot_idx, slot_gate = _routing(top_idx, gates, MOE_ROWS, d // LANES)
    e = w_gate_up.shape[0]
    y_tok = _experts(bexp, n_used, slot_idx, slot_gate, xn, w_gate_up.astype(BF16),
                     b_gate_up.reshape(e, 1, -1), w_down.astype(BF16), b_down.reshape(e, 1, -1))
    return _combine(h2, y_tok, norm_final, min(256, t))


def _permute_w_in(w_in):
    sp = np.cumsum([0, ATT_WIDTH, KV_LATENT, IDX_WIDTH, IDX_DIM, IDX_HEADS, RET_QK_WIDTH, RET_QK_WIDTH,
                    RET_WIDTH, RET_WIDTH])
    piece = lambda i: w_in[:, int(sp[i]):int(sp[i + 1])]
    d = w_in.shape[0]
    cols = [piece(0), piece(2), piece(7), piece(8), piece(5), piece(6), piece(1), piece(3), piece(4)]
    used = sum(c.shape[1] for c in cols)
    cols.append(jnp.zeros((d, PROJ_WIDTH - used), w_in.dtype))
    return jnp.concatenate(cols, axis=1).astype(BF16)


def _rotary_tables(positions):
    half = RET_QK_DIM // 2
    inv = ROPE_BASE ** (-jnp.arange(0, RET_QK_DIM, 2, dtype=F32) / RET_QK_DIM)
    ang = positions.astype(F32).reshape(-1, 1) * inv
    cos, sin = jnp.cos(ang), jnp.sin(ang)
    return jnp.concatenate([cos, cos], axis=1), jnp.concatenate([-sin, sin], axis=1)


def _front(x, positions, norm_mix, w_in, kv_norm, idx_k_norm):
    b, s, d = x.shape
    t = b * s
    tm = min(1024, t)
    proj = _proj(x.reshape(t, d), norm_mix.reshape(1, d), _permute_w_in(w_in), tm)
    cos2, sin2 = _rotary_tables(positions)
    g_k = jnp.concatenate([idx_k_norm, jnp.zeros((LANES - IDX_DIM,), F32)]).reshape(1, LANES)
    c, kidx, widx, q_rot, k_rot = _prep(proj, cos2, sin2, kv_norm.reshape(1, KV_LATENT), g_k, tm)
    return dict(proj=proj, c=c, kidx=kidx, widx=widx, q_rot=q_rot, k_rot=k_rot)


def _mid_stage(x, mem, y_att, y_ret, w_out, norm_cross, norm_mem, w_cq, w_ck, w_cv, w_co, norm_ffn,
               w_router, b_router):
    b, s, d = x.shape
    km, vm = _memkv(mem, norm_mem.reshape(1, d), w_ck.astype(BF16), w_cv.astype(BF16))
    w_out_b16 = w_out.astype(BF16)
    return _mid(x.reshape(b * s, d), y_att, y_ret, w_out_b16[:ATT_WIDTH], w_out_b16[ATT_WIDTH:],
                norm_cross.reshape(1, d), w_cq.astype(BF16), km, vm, w_co.astype(BF16),
                norm_ffn.reshape(1, d), w_router.T.astype(BF16), b_router.reshape(N_EXPERTS, 1),
                b, s, min(512, s))


def _ret_stage(st, b, s):
    q_rot = st["q_rot"].reshape(b, s, RET_QK_WIDTH)
    kt_rot = jnp.swapaxes(st["k_rot"].reshape(b, s, RET_QK_WIDTH), 1, 2)
    return _retention(q_rot, kt_rot, st["proj"], b, s)


def kernel(x, mem, positions, norm_mix, w_in, kv_norm, idx_k_norm, w_uk, w_uv, rel_bias, w_out, norm_cross,
           norm_mem, w_cq, w_ck, w_cv, w_co, norm_ffn, w_router, b_router, w_gate_up, b_gate_up, w_down, b_down,
           norm_final):
    b, s, d = x.shape
    depth = norm_mix.shape[0]
    h = x
    for l in range(depth):
        st = _front(h, positions, norm_mix[l], w_in[l], kv_norm[l], idx_k_norm[l])
        y_att = _dsa_stage(st, w_uk[l], w_uv[l], rel_bias, b, s)
        y_ret = _ret_stage(st, b, s)
        h2, xn, top_idx, gates = _mid_stage(h, mem, y_att, y_ret, w_out[l], norm_cross[l], norm_mem[l], w_cq[l],
                                            w_ck[l], w_cv[l], w_co[l], norm_ffn[l], w_router[l], b_router[l])
        gain = norm_final if l == depth - 1 else None
        h = _moe_stage(h2, xn, top_idx, gates, w_gate_up[l], b_gate_up[l], w_down[l], b_down[l], gain).reshape(b, s, d)
    return h
```

```python
import functools
import math

import jax
import jax.numpy as jnp
import numpy as np
from jax import lax
from jax.experimental import pallas as pl
from jax.experimental.pallas import tpu as pltpu

F32 = jnp.float32
BF16 = jnp.bfloat16

D_MODEL = 2048
N_ATT_HEADS = 8
ATT_HEAD_DIM = 128
KV_LATENT = 256
IDX_HEADS = 16
IDX_DIM = 64
IDX_TOPK_MAX = 256
N_RET_HEADS = 4
RET_QK_DIM = 128
RET_V_DIM = 256
ROPE_BASE = 10000.0
N_BUCKETS = 32
MAX_DISTANCE = 128
N_CROSS_HEADS = 4
CROSS_HEAD_DIM = 128
N_EXPERTS = 32
TOP_K = 4
D_FF = D_MODEL
SWIGLU_LIMIT = 7.0
SWIGLU_ALPHA = 1.702
EPS = 1e-6

ATT_WIDTH = N_ATT_HEADS * ATT_HEAD_DIM
RET_WIDTH = N_RET_HEADS * RET_V_DIM
RET_QK_WIDTH = N_RET_HEADS * RET_QK_DIM
IDX_WIDTH = IDX_HEADS * IDX_DIM
CROSS_WIDTH = N_CROSS_HEADS * CROSS_HEAD_DIM

LANES = 128
VMEM_LIMIT = 56 * 1024 * 1024

COL_Q_ATT = 0
COL_Q_IDX = COL_Q_ATT + ATT_WIDTH
COL_V_RET = COL_Q_IDX + IDX_WIDTH
COL_G_RET = COL_V_RET + RET_WIDTH
COL_Q_RET = COL_G_RET + RET_WIDTH
COL_K_RET = COL_Q_RET + RET_QK_WIDTH
COL_C_KV = COL_K_RET + RET_QK_WIDTH
COL_KW = COL_C_KV + KV_LATENT
PROJ_TN = 512
PROJ_WIDTH = 5632

Q_BLOCK = 128
KEY_TILE = 512
KEY_PAD = KEY_TILE - Q_BLOCK
NEG_BIG = -1e30
LOG2E = math.log2(math.e)

RET_CHUNK = 256

MOE_ROWS = 512
MOE_TF = 1024


def _cparams(sem, vmem=VMEM_LIMIT):
    return pltpu.CompilerParams(dimension_semantics=sem, vmem_limit_bytes=vmem)


def _rms(x, g):
    return x * lax.rsqrt(jnp.mean(x * x, axis=-1, keepdims=True) + EPS) * g


ROW_PITCH = 20


def _store_rows(ref, x, pitch=None):
    n, d = x.shape
    w = d // LANES
    for a in range(w):
        ref[pl.ds(a, n, stride=pitch or w), :] = x[:, a * LANES:(a + 1) * LANES].astype(ref.dtype)


def _load_rows(ref, n, d, pitch=None):
    w = d // LANES
    return jnp.concatenate([ref[pl.ds(a, n, stride=pitch or w), :] for a in range(w)], axis=1)


def _proj_body(x_ref, g_ref, w_ref, o_ref, xn_ref):
    @pl.when(pl.program_id(1) == 0)
    def _():
        xn_ref[...] = _rms(x_ref[...], g_ref[...]).astype(BF16)

    o_ref[...] = jnp.dot(xn_ref[...], w_ref[...], preferred_element_type=F32).astype(o_ref.dtype)


def _proj(x2, gain, w_p, tm):
    t, d = x2.shape
    n = w_p.shape[1]
    return pl.pallas_call(
        _proj_body,
        out_shape=jax.ShapeDtypeStruct((t, n), BF16),
        grid=(t // tm, n // PROJ_TN),
        in_specs=[
            pl.BlockSpec((tm, d), lambda i, j: (i, 0)),
            pl.BlockSpec((1, d), lambda i, j: (0, 0)),
            pl.BlockSpec((d, PROJ_TN), lambda i, j: (0, j)),
        ],
        out_specs=pl.BlockSpec((tm, PROJ_TN), lambda i, j: (i, j)),
        scratch_shapes=[pltpu.VMEM((tm, d), BF16)],
        compiler_params=_cparams(("arbitrary", "arbitrary")),
        name="proj",
    )(x2, gain, w_p)


def _prep_body(c_ref, kw_ref, q_ref, k_ref, cos_ref, sin_ref, gkv_ref, gk_ref,
               c_out, kidx_out, widx_out, q_out, k_out):
    c_out[...] = _rms(c_ref[...].astype(F32), gkv_ref[...]).astype(BF16)

    kw = kw_ref[...].astype(F32)
    lane = lax.broadcasted_iota(jnp.int32, kw.shape, 1)
    is_k = lane < IDX_DIM
    kk = jnp.where(is_k, kw, 0.0)
    ms = jnp.sum(kk * kk, axis=-1, keepdims=True) * (1.0 / IDX_DIM)
    kidx_out[...] = (kk * lax.rsqrt(ms + EPS) * gk_ref[...]).astype(BF16)
    widx_out[...] = kw * (IDX_HEADS ** -0.5 * IDX_DIM ** -0.5)

    cos2 = cos_ref[...]
    sin2 = sin_ref[...]
    for h in range(N_RET_HEADS):
        sl = slice(h * RET_QK_DIM, (h + 1) * RET_QK_DIM)
        qh = q_ref[:, sl].astype(F32)
        kh = k_ref[:, sl].astype(F32)
        q_out[:, sl] = (qh * cos2 + pltpu.roll(qh, RET_QK_DIM // 2, 1) * sin2).astype(BF16)
        k_out[:, sl] = ((kh * cos2 + pltpu.roll(kh, RET_QK_DIM // 2, 1) * sin2)
                        * (RET_QK_DIM ** -0.5)).astype(BF16)


def _prep(proj, cos2, sin2, g_kv, g_k, tm):
    t = proj.shape[0]
    row = lambda i: (i, 0)
    return pl.pallas_call(
        _prep_body,
        out_shape=(
            jax.ShapeDtypeStruct((t, KV_LATENT), BF16),
            jax.ShapeDtypeStruct((t, LANES), BF16),
            jax.ShapeDtypeStruct((t, LANES), F32),
            jax.ShapeDtypeStruct((t, RET_QK_WIDTH), BF16),
            jax.ShapeDtypeStruct((t, RET_QK_WIDTH), BF16),
        ),
        grid=(t // tm,),
        in_specs=[
            pl.BlockSpec((tm, KV_LATENT), lambda i: (i, COL_C_KV // KV_LATENT)),
            pl.BlockSpec((tm, LANES), lambda i: (i, COL_KW // LANES)),
            pl.BlockSpec((tm, RET_QK_WIDTH), lambda i: (i, COL_Q_RET // RET_QK_WIDTH)),
            pl.BlockSpec((tm, RET_QK_WIDTH), lambda i: (i, COL_K_RET // RET_QK_WIDTH)),
            pl.BlockSpec((tm, LANES), row),
            pl.BlockSpec((tm, LANES), row),
            pl.BlockSpec((1, KV_LATENT), lambda i: (0, 0)),
            pl.BlockSpec((1, LANES), lambda i: (0, 0)),
        ],
        out_specs=(
            pl.BlockSpec((tm, KV_LATENT), row),
            pl.BlockSpec((tm, LANES), row),
            pl.BlockSpec((tm, LANES), row),
            pl.BlockSpec((tm, RET_QK_WIDTH), row),
            pl.BlockSpec((tm, RET_QK_WIDTH), row),
        ),
        compiler_params=_cparams(("arbitrary",)),
        name="prep",
    )(proj, proj, proj, proj, cos2, sin2, g_kv, g_k)


def _ret_body(q_ref, kt_ref, v_ref, g_ref, dec_ref, cross_ref, state_ref, cd_ref, o_ref, r_ref):
    @pl.when(pl.program_id(1) == 0)
    def _():
        r_ref[...] = jnp.zeros_like(r_ref)

    for h in range(N_RET_HEADS):
        qs = slice(h * RET_QK_DIM, (h + 1) * RET_QK_DIM)
        vs = slice(h * RET_V_DIM, (h + 1) * RET_V_DIM)
        q = q_ref[0, :, qs]
        kt = kt_ref[0, qs, :]
        v = v_ref[:, vs]
        r_old = r_ref[h]
        inner = jnp.dot(q, kt, preferred_element_type=F32) * dec_ref[h]
        qc = (q.astype(F32) * cross_ref[h]).astype(BF16)
        o = (jnp.dot(inner.astype(BF16), v, preferred_element_type=F32)
             + jnp.dot(qc, r_old.astype(BF16), preferred_element_type=F32))
        ks = (kt.astype(F32) * state_ref[h]).astype(BF16)
        r_ref[h] = r_old * cd_ref[h] + jnp.dot(ks, v, preferred_element_type=F32)
        o = o * lax.rsqrt(jnp.mean(o * o, axis=-1, keepdims=True) + EPS)
        g = g_ref[:, vs].astype(F32)
        o_ref[:, vs] = (g * jax.nn.sigmoid(g) * o).astype(o_ref.dtype)


def _retention(q_rot, kt_rot, proj, b, s):
    c = RET_CHUNK
    n = s // c
    hh = N_RET_HEADS
    log_gamma = jnp.log(1.0 - 2.0 ** (-5.0 - jnp.arange(hh, dtype=F32)))
    j = jnp.arange(c, dtype=F32)
    diff = j[:, None] - j[None, :]
    dec = jnp.where(diff >= 0, jnp.exp(log_gamma[:, None, None] * jnp.maximum(diff, 0.0)), 0.0)
    cross = jnp.broadcast_to(jnp.exp(log_gamma[:, None] * (j + 1.0))[:, :, None], (hh, c, RET_QK_DIM))
    state = jnp.exp(log_gamma[:, None] * (c - 1.0 - j))[:, None, :]
    cdec = jnp.broadcast_to(jnp.exp(log_gamma * c)[:, None, None], (hh, 1, RET_V_DIM))
    const = lambda bi, ci: (0, 0, 0)
    return pl.pallas_call(
        _ret_body,
        out_shape=jax.ShapeDtypeStruct((b * s, RET_WIDTH), BF16),
        grid=(b, n),
        in_specs=[
            pl.BlockSpec((1, c, RET_QK_WIDTH), lambda bi, ci: (bi, ci, 0)),
            pl.BlockSpec((1, RET_QK_WIDTH, c), lambda bi, ci: (bi, 0, ci)),
            pl.BlockSpec((c, RET_WIDTH), lambda bi, ci: (bi * n + ci, COL_V_RET // RET_WIDTH)),
            pl.BlockSpec((c, RET_WIDTH), lambda bi, ci: (bi * n + ci, COL_G_RET // RET_WIDTH)),
            pl.BlockSpec((hh, c, c), const),
            pl.BlockSpec((hh, c, RET_QK_DIM), const),
            pl.BlockSpec((hh, 1, c), const),
            pl.BlockSpec((hh, 1, RET_V_DIM), const),
        ],
        out_specs=pl.BlockSpec((c, RET_WIDTH), lambda bi, ci: (bi * n + ci, 0)),
        scratch_shapes=[pltpu.VMEM((hh, RET_QK_DIM, RET_V_DIM), F32)],
        compiler_params=_cparams(("arbitrary", "arbitrary")),
        name="ret",
    )(q_rot, kt_rot, proj, proj, dec, cross, state, cdec)


def _dsa_body(topk, kidx_ref, qit_ref, w_ref, c_ref, ct_ref, qt_ref, wuk_ref, wuvt_ref, bias_ref, y_ref,
              sc_ref, qlt_ref, acc_ref, m_ref, l_ref):
    qb = pl.program_id(1)
    tk = KEY_TILE
    half = tk // 2
    e_pad = (qb + 1) * Q_BLOCK + KEY_PAD
    n_tiles = e_pad // tk
    nh = N_ATT_HEADS
    lane_q = qb * Q_BLOCK + lax.broadcasted_iota(jnp.int32, (1, Q_BLOCK), 1)

    def tile_start(j):
        return pl.multiple_of(e_pad - (j + 1) * tk, Q_BLOCK)

    for h in range(nh):
        sl = slice(h * Q_BLOCK, (h + 1) * Q_BLOCK)
        hd = slice(h * ATT_HEAD_DIM, (h + 1) * ATT_HEAD_DIM)
        qlt_ref[:, sl] = (jnp.dot(wuk_ref[h], qt_ref[0, 0, hd, :], preferred_element_type=F32)
                          * (ATT_HEAD_DIM ** -0.5 * LOG2E)).astype(BF16)

    def score_tile(j, carry):
        r0 = tile_start(j)
        for part in range(2):
            rs = pl.multiple_of(r0 + part * half, Q_BLOCK)
            z = jnp.dot(kidx_ref[0, pl.ds(rs, half), :], qit_ref[0, 0], preferred_element_type=F32)
            tot = jnp.zeros((half, Q_BLOCK), F32)
            for h in range(IDX_HEADS):
                sl = slice(h * Q_BLOCK, (h + 1) * Q_BLOCK)
                tot = tot + jnp.maximum(z[:, sl], 0.0) * w_ref[0, 0, :, sl]
            key = rs - KEY_PAD + lax.broadcasted_iota(jnp.int32, (half, Q_BLOCK), 0)
            ok = (key >= 0) & (key <= lane_q)
            sc_ref[pl.ds(rs, half), :] = jnp.where(ok, tot, NEG_BIG)
        return carry
    lax.fori_loop(0, n_tiles, score_tile, 0)

    def minmax_tile(j, carry):
        lo, hi = carry
        s = sc_ref[pl.ds(tile_start(j), tk), :]
        hi = jnp.maximum(hi, jnp.max(s, axis=0, keepdims=True))
        lo = jnp.minimum(lo, jnp.min(jnp.where(s > 0.5 * NEG_BIG, s, -NEG_BIG), axis=0, keepdims=True))
        return lo, hi
    lo0, hi0 = lax.fori_loop(0, n_tiles, minmax_tile,
                             (jnp.full((1, Q_BLOCK), -NEG_BIG, F32), jnp.full((1, Q_BLOCK), NEG_BIG, F32)))
    search = lane_q + 1 > topk

    n_chain = 4
    rows_chain = tk // n_chain

    def count_ge(x):
        def body(j, cnts):
            r0 = tile_start(j)
            out = []
            for k in range(n_chain):
                s = sc_ref[pl.ds(pl.multiple_of(r0 + k * rows_chain, Q_BLOCK), rows_chain), :]
                ind = jnp.where(s >= x, 1.0, 0.0).reshape(rows_chain // 8, 8, Q_BLOCK)
                out.append(cnts[k] + jnp.sum(ind, axis=0))
            return tuple(out)
        cnts = lax.fori_loop(0, n_tiles, body, tuple(jnp.zeros((8, Q_BLOCK), F32) for _ in range(n_chain)))
        return jnp.sum((cnts[0] + cnts[1]) + (cnts[2] + cnts[3]), axis=0, keepdims=True)

    def bis_cond(st):
        it, _, _, _, active = st
        return (it < 40) & (active > 0.0)

    def bis_body(st):
        it, lo, hi, done, _ = st
        still_open = jnp.max(1.0 - done)
        mid = lo + 0.5 * (hi - lo)
        cnt = count_ge(mid)
        ge = cnt >= topk
        open_ = done < 0.5
        lo = jnp.where(open_ & ge, mid, lo)
        hi = jnp.where(open_ & (~ge), mid, hi)
        done = jnp.where(cnt == topk, 1.0, done)
        return it + 1, lo, hi, done, still_open

    done0 = jnp.where(search, 0.0, 1.0)
    _, lo_f, _, _, _ = lax.while_loop(bis_cond, bis_body, (jnp.int32(0), lo0, hi0, done0, jnp.max(1.0 - done0)))
    thr = jnp.where(search, lo_f, 0.5 * NEG_BIG)

    m_ref[...] = jnp.full(m_ref.shape, NEG_BIG, F32)
    l_ref[...] = jnp.zeros(l_ref.shape, F32)
    acc_ref[...] = jnp.zeros(acc_ref.shape, F32)

    n_grp = 4
    hg = nh // n_grp

    def attend(segments):
        scores, ct_tiles = [], []
        for rs, n, with_bias in segments:
            c_tile = c_ref[0, pl.ds(rs, n), :]
            ct_tiles.append(ct_ref[0, :, pl.ds(rs, n)])
            pen = jnp.where(sc_ref[pl.ds(rs, n), :] >= thr, 0.0, NEG_BIG)
            pen = jnp.concatenate([pen] * hg, axis=1)
            for g in range(n_grp):
                ls = slice(g * hg * Q_BLOCK, (g + 1) * hg * Q_BLOCK)
                st = jnp.dot(c_tile, qlt_ref[:, ls], preferred_element_type=F32) + pen
                if with_bias:
                    st = st + bias_ref[:, ls]
                scores.append(st)
        for k, ct_tile in enumerate(ct_tiles):
            for g in range(n_grp):
                ls = slice(g * hg * Q_BLOCK, (g + 1) * hg * Q_BLOCK)
                st = scores[k * n_grp + g]
                m_old = m_ref[:, ls]
                m_new = jnp.maximum(m_old, jnp.max(st, axis=0, keepdims=True))
                alpha = jnp.exp2(m_old - m_new)
                p = jnp.exp2(st - m_new)
                l_ref[:, ls] = alpha * l_ref[:, ls] + jnp.sum(p, axis=0, keepdims=True)
                acc_ref[:, ls] = acc_ref[:, ls] * alpha + jnp.dot(ct_tile, p.astype(BF16),
                                                                  preferred_element_type=F32)
                m_ref[:, ls] = m_new

    r_first = tile_start(0)
    attend([(pl.multiple_of(r_first + half, Q_BLOCK), half, True), (r_first, half, False)])

    def attend_pair(i, carry):
        attend([(tile_start(2 * i + 1), tk, False), (tile_start(2 * i + 2), tk, False)])
        return carry
    n_rest = n_tiles - 1
    lax.fori_loop(0, n_rest // 2, attend_pair, 0)

    @pl.when(n_rest % 2 == 1)
    def _():
        attend([(tile_start(n_tiles - 1), tk, False)])

    ot = acc_ref[...] / l_ref[...]
    for h in range(nh):
        sl = slice(h * Q_BLOCK, (h + 1) * Q_BLOCK)
        y_ref[0, h * ATT_HEAD_DIM:(h + 1) * ATT_HEAD_DIM, :] = jnp.dot(
            wuvt_ref[h], ot[:, sl].astype(BF16), preferred_element_type=F32).astype(y_ref.dtype)


def _dsa(kidx_p, qit, widx_t, c_p, ct_p, qt, wuk_h, wuvt_h, bias_t, b, s, topk):
    nqb = s // Q_BLOCK
    sp = s + KEY_PAD
    nh = N_ATT_HEADS
    per_b = lambda bi, qi: (bi, 0, 0)
    per_q = lambda bi, qi: (bi, qi, 0, 0)
    c3 = lambda bi, qi: (0, 0, 0)
    return pl.pallas_call(
        functools.partial(_dsa_body, topk),
        out_shape=jax.ShapeDtypeStruct((b, ATT_WIDTH, s), BF16),
        grid=(b, nqb),
        in_specs=[
            pl.BlockSpec((1, sp, LANES), per_b),
            pl.BlockSpec((1, 1, LANES, IDX_HEADS * Q_BLOCK), per_q),
            pl.BlockSpec((1, 1, 1, IDX_HEADS * Q_BLOCK), per_q),
            pl.BlockSpec((1, sp, KV_LATENT), per_b),
            pl.BlockSpec((1, KV_LATENT, sp), per_b),
            pl.BlockSpec((1, 1, ATT_WIDTH, Q_BLOCK), per_q),
            pl.BlockSpec((nh, KV_LATENT, ATT_HEAD_DIM), c3),
            pl.BlockSpec((nh, ATT_HEAD_DIM, KV_LATENT), c3),
            pl.BlockSpec((2 * Q_BLOCK, nh * Q_BLOCK), lambda bi, qi: (0, 0)),
        ],
        out_specs=pl.BlockSpec((1, ATT_WIDTH, Q_BLOCK), lambda bi, qi: (bi, 0, qi)),
        scratch_shapes=[
            pltpu.VMEM((sp, Q_BLOCK), F32),
            pltpu.VMEM((KV_LATENT, nh * Q_BLOCK), BF16),
            pltpu.VMEM((KV_LATENT, nh * Q_BLOCK), F32),
            pltpu.VMEM((1, nh * Q_BLOCK), F32),
            pltpu.VMEM((1, nh * Q_BLOCK), F32),
        ],
        compiler_params=_cparams(("arbitrary", "arbitrary")),
        name="dsa",
    )(kidx_p, qit, widx_t, c_p, ct_p, qt, wuk_h, wuvt_h, bias_t)


def _t5_bucket(dist):
    n = jnp.maximum(dist, 0)
    max_exact = N_BUCKETS // 2
    nf = jnp.maximum(n, 1).astype(F32)
    large = max_exact + (jnp.log(nf / max_exact) / math.log(MAX_DISTANCE / max_exact)
                         * (N_BUCKETS - max_exact)).astype(jnp.int32)
    return jnp.where(n < max_exact, n, jnp.minimum(large, N_BUCKETS - 1))


def _dsa_stage(st, w_uk, w_uv, rel_bias, b, s):
    nqb = s // Q_BLOCK
    nh = N_ATT_HEADS
    proj = st["proj"].reshape(b, nqb, Q_BLOCK, PROJ_WIDTH)
    qt = jnp.swapaxes(proj[..., COL_Q_ATT:COL_Q_ATT + ATT_WIDTH], 2, 3)
    qi = proj[..., COL_Q_IDX:COL_Q_IDX + IDX_WIDTH].reshape(b, nqb, Q_BLOCK, IDX_HEADS, IDX_DIM)
    qit = jnp.transpose(qi, (0, 1, 4, 3, 2)).reshape(b, nqb, IDX_DIM, IDX_HEADS * Q_BLOCK)
    qit = jnp.pad(qit, ((0, 0), (0, 0), (0, LANES - IDX_DIM), (0, 0)))
    wi = st["widx"].reshape(b, nqb, Q_BLOCK, LANES)[..., IDX_DIM:IDX_DIM + IDX_HEADS]
    widx_t = jnp.swapaxes(wi, 2, 3).reshape(b, nqb, 1, IDX_HEADS * Q_BLOCK)
    front = ((0, 0), (KEY_PAD, 0), (0, 0))
    kidx_p = jnp.pad(st["kidx"].reshape(b, s, LANES), front)
    c_p = jnp.pad(st["c"].reshape(b, s, KV_LATENT), front)
    ct_p = jnp.swapaxes(c_p, 1, 2)
    wuk_h = jnp.transpose(w_uk, (1, 0, 2)).astype(BF16)
    wuvt_h = jnp.transpose(w_uv, (1, 2, 0)).astype(BF16)
    i = jnp.arange(2 * Q_BLOCK)[:, None]
    j = jnp.arange(Q_BLOCK)[None, :]
    rb = (rel_bias - rel_bias[N_BUCKETS - 1]) * LOG2E
    bias = rb[_t5_bucket(j - i + Q_BLOCK)]
    bias_t = jnp.transpose(bias, (0, 2, 1)).reshape(2 * Q_BLOCK, nh * Q_BLOCK)
    topk = min(IDX_TOPK_MAX, s // 4)
    yt = _dsa(kidx_p, qit, widx_t, c_p, ct_p, qt, wuk_h, wuvt_h, bias_t, b, s, topk)
    return jnp.swapaxes(yt, 1, 2).reshape(b * s, ATT_WIDTH)


def _memkv_body(m_ref, g_ref, wk_ref, wv_ref, k_ref, v_ref):
    mn = _rms(m_ref[0], g_ref[...]).astype(BF16)
    k_ref[0] = jnp.dot(mn, wk_ref[...], preferred_element_type=F32).astype(BF16)
    v_ref[0] = jnp.dot(mn, wv_ref[...], preferred_element_type=F32).astype(BF16)


def _memkv(mem, gain, w_ck, w_cv):
    b, m, d = mem.shape
    w = w_ck.shape[1]
    const = lambda i: (0, 0)
    out = jax.ShapeDtypeStruct((b, m, w), BF16)
    return pl.pallas_call(
        _memkv_body,
        out_shape=(out, out),
        grid=(b,),
        in_specs=[
            pl.BlockSpec((1, m, d), lambda i: (i, 0, 0)),
            pl.BlockSpec((1, d), const),
            pl.BlockSpec((d, w), const),
            pl.BlockSpec((d, w), const),
        ],
        out_specs=(pl.BlockSpec((1, m, w), lambda i: (i, 0, 0)), pl.BlockSpec((1, m, w), lambda i: (i, 0, 0))),
        compiler_params=_cparams(("arbitrary",)),
        name="memkv",
    )(mem, gain, w_ck, w_cv)


def _mid_body(x_ref, ya_ref, yr_ref, woa_ref, wob_ref, gc_ref, wq_ref, km_ref, vm_ref, wo_ref,
              gf_ref, wr_ref, br_ref, h2_ref, xn_ref, idx_ref, gate_ref):
    h1 = (x_ref[...]
          + jnp.dot(ya_ref[...], woa_ref[...], preferred_element_type=F32)
          + jnp.dot(yr_ref[...], wob_ref[...], preferred_element_type=F32))
    hn = _rms(h1, gc_ref[...]).astype(BF16)
    q = jnp.dot(hn, wq_ref[...], preferred_element_type=F32).astype(BF16)
    heads = []
    for h in range(N_CROSS_HEADS):
        sl = slice(h * CROSS_HEAD_DIM, (h + 1) * CROSS_HEAD_DIM)
        s = lax.dot_general(q[:, sl], km_ref[0, :, sl], (((1,), (1,)), ((), ())),
                            preferred_element_type=F32) * (CROSS_HEAD_DIM ** -0.5)
        p = jnp.exp(s - jnp.max(s, axis=-1, keepdims=True))
        p = p / jnp.sum(p, axis=-1, keepdims=True)
        heads.append(jnp.dot(p.astype(BF16), vm_ref[0, :, sl], preferred_element_type=F32).astype(BF16))
    o = jnp.concatenate(heads, axis=-1)
    h2 = h1 + jnp.dot(o, wo_ref[...], preferred_element_type=F32)
    h2_ref[...] = h2

    xn = _rms(h2, gf_ref[...])
    _store_rows(xn_ref, xn)
    logit = lax.dot_general(wr_ref[...], xn.astype(BF16), (((1,), (1,)), ((), ())),
                            preferred_element_type=F32) + br_ref[...]
    eid = lax.broadcasted_iota(jnp.int32, logit.shape, 0)
    vals, ids = [], []
    for _ in range(TOP_K):
        mx = jnp.max(logit, axis=0, keepdims=True)
        sel = jnp.min(jnp.where(logit == mx, eid, N_EXPERTS), axis=0, keepdims=True)
        vals.append(mx)
        ids.append(sel)
        logit = jnp.where(eid == sel, -jnp.inf, logit)
    top = jnp.concatenate(vals, axis=0)
    e = jnp.exp(top - vals[0])
    gate_ref[...] = e / jnp.sum(e, axis=0, keepdims=True)
    idx_ref[...] = jnp.concatenate(ids, axis=0)


def _mid(x2, ya, yr, w_out_a, w_out_b, g_cross, w_cq, km, vm, w_co, g_ffn, w_rt, b_r, b, s, tm):
    t, d = x2.shape
    nt = s // tm
    row = lambda bi, i: (bi * nt + i, 0)
    const = lambda bi, i: (0, 0)
    once = pl.Buffered(1)
    m = km.shape[1]
    return pl.pallas_call(
        _mid_body,
        out_shape=(
            jax.ShapeDtypeStruct((t, d), F32),
            jax.ShapeDtypeStruct((t * (d // LANES), LANES), F32),
            jax.ShapeDtypeStruct((TOP_K, t), jnp.int32),
            jax.ShapeDtypeStruct((TOP_K, t), F32),
        ),
        grid=(b, nt),
        in_specs=[
            pl.BlockSpec((tm, d), row),
            pl.BlockSpec((tm, ATT_WIDTH), row),
            pl.BlockSpec((tm, RET_WIDTH), row),
            pl.BlockSpec((ATT_WIDTH, d), const, pipeline_mode=once),
            pl.BlockSpec((RET_WIDTH, d), const, pipeline_mode=once),
            pl.BlockSpec((1, d), const),
            pl.BlockSpec((d, CROSS_WIDTH), const, pipeline_mode=once),
            pl.BlockSpec((1, m, CROSS_WIDTH), lambda bi, i: (bi, 0, 0)),
            pl.BlockSpec((1, m, CROSS_WIDTH), lambda bi, i: (bi, 0, 0)),
            pl.BlockSpec((CROSS_WIDTH, d), const, pipeline_mode=once),
            pl.BlockSpec((1, d), const),
            pl.BlockSpec((N_EXPERTS, d), const),
            pl.BlockSpec((N_EXPERTS, 1), const),
        ],
        out_specs=(
            pl.BlockSpec((tm, d), row),
            pl.BlockSpec((tm * (d // LANES), LANES), row),
            pl.BlockSpec((TOP_K, tm), lambda bi, i: (0, bi * nt + i)),
            pl.BlockSpec((TOP_K, tm), lambda bi, i: (0, bi * nt + i)),
        ),
        compiler_params=_cparams(("arbitrary", "arbitrary")),
        name="mid",
    )(x2, ya, yr, w_out_a, w_out_b, g_cross, w_cq, km, vm, w_co, g_ffn, w_rt, b_r)


def _experts_body(be_ref, nu_ref, idx_hbm, gate_ref, xn_hbm, wg_ref, wu_ref, bg_ref, bu_ref, wd_ref, bd_ref,
                  y_hbm, idx_smem, xbuf, xb_ref, acc_ref, ybuf, sem_idx, sem_g, sem_s):
    i = pl.program_id(0)
    j = pl.program_id(1)
    n_used = nu_ref[0]
    rows, d = xb_ref.shape
    w = d // LANES

    def idx_copy(blk, slot):
        return pltpu.make_async_copy(idx_hbm.at[blk],
                                     idx_smem.at[pl.ds(pl.multiple_of(slot * 2 * rows, 2 * rows), 2 * rows)],
                                     sem_idx.at[slot])

    def gather_start(islot, xslot):
        def body(r, carry):
            src = pl.multiple_of(idx_smem[islot * 2 * rows + r], w)
            pltpu.make_async_copy(xn_hbm.at[pl.ds(src, w)], xbuf.at[xslot, pl.ds(r * ROW_PITCH, w)],
                                  sem_g.at[xslot]).start()
            return carry
        lax.fori_loop(0, rows, body, 0, unroll=8)

    def gather_wait(xslot):
        pltpu.make_async_copy(xn_hbm.at[pl.ds(0, rows * w)], xbuf.at[xslot, pl.ds(0, rows * w)],
                              sem_g.at[xslot]).wait()

    def scatter_wait():
        pltpu.make_async_copy(ybuf.at[pl.ds(0, rows * w)], y_hbm.at[pl.ds(0, rows * w)], sem_s).wait()

    @pl.when(j == 0)
    def _():
        @pl.when(i == 0)
        def _():
            ybuf[...] = jnp.zeros_like(ybuf)
            dump = pltpu.make_async_copy(ybuf.at[pl.ds(0, rows * w)],
                                         y_hbm.at[pl.ds(y_hbm.shape[0] - rows * w, rows * w)], sem_s)
            dump.start()
            dump.wait()
            idx_copy(0, 0).start()

            @pl.when(n_used > 1)
            def _():
                idx_copy(1, 1).start()

            idx_copy(0, 0).wait()
            gather_start(0, 0)

        @pl.when(i + 1 < n_used)
        def _():
            idx_copy(i + 1, (i + 1) % 3).wait()
            gather_start((i + 1) % 3, (i + 1) % 2)

        @pl.when(i + 2 < n_used)
        def _():
            idx_copy(i + 2, (i + 2) % 3).start()

        @pl.when(i < n_used)
        def _():
            gather_wait(i % 2)
            xb_ref[...] = _load_rows(xbuf.at[i % 2], rows, d, ROW_PITCH).astype(BF16)

    @pl.when(i < n_used)
    def _():
        xb = xb_ref[...]
        g = jnp.dot(xb, wg_ref[0], preferred_element_type=F32) + bg_ref[0]
        u = jnp.dot(xb, wu_ref[0], preferred_element_type=F32) + bu_ref[0]
        gt = jnp.minimum(g, SWIGLU_LIMIT)
        up = jnp.clip(u, -SWIGLU_LIMIT, SWIGLU_LIMIT)
        hm = ((up + 1.0) * (gt * jax.nn.sigmoid(SWIGLU_ALPHA * gt))).astype(BF16)
        part = jnp.dot(hm, wd_ref[0], preferred_element_type=F32)

        @pl.when(j == 0)
        def _():
            acc_ref[...] = part

        @pl.when(j > 0)
        def _():
            acc_ref[...] += part

        @pl.when(j == pl.num_programs(1) - 1)
        def _():
            y = (acc_ref[...] + bd_ref[0]) * gate_ref[0]

            @pl.when(i > 0)
            def _():
                scatter_wait()

            _store_rows(ybuf, y, ROW_PITCH)
            islot = i % 3

            def body(r, carry):
                dst = pl.multiple_of(idx_smem[islot * 2 * rows + rows + r], w)
                pltpu.make_async_copy(ybuf.at[pl.ds(r * ROW_PITCH, w)], y_hbm.at[pl.ds(dst, w)], sem_s).start()
                return carry
            lax.fori_loop(0, rows, body, 0, unroll=8)

            @pl.when(i == n_used - 1)
            def _():
                scatter_wait()


def _experts(block_expert, n_used, slot_idx, slot_gate, xn, w_gu, b_gu, w_down, b_down):
    nb = slot_idx.shape[0]
    rows = slot_idx.shape[1] // 2
    e, d, ff2 = w_gu.shape
    w = d // LANES
    t = xn.shape[0] // w
    ff = ff2 // 2
    tf = MOE_TF
    nj = ff // tf
    a_rows = t * TOP_K + rows
    grid_spec = pltpu.PrefetchScalarGridSpec(
        num_scalar_prefetch=2,
        grid=(nb, nj),
        in_specs=[
            pl.BlockSpec(memory_space=pl.ANY),
            pl.BlockSpec((1, rows, 1), lambda i, j, be, nu: (i, 0, 0)),
            pl.BlockSpec(memory_space=pl.ANY),
            pl.BlockSpec((1, d, tf), lambda i, j, be, nu: (be[i], 0, j)),
            pl.BlockSpec((1, d, tf), lambda i, j, be, nu: (be[i], 0, j + nj)),
            pl.BlockSpec((1, 1, tf), lambda i, j, be, nu: (be[i], 0, j)),
            pl.BlockSpec((1, 1, tf), lambda i, j, be, nu: (be[i], 0, j + nj)),
            pl.BlockSpec((1, tf, d), lambda i, j, be, nu: (be[i], j, 0)),
            pl.BlockSpec((1, 1, d), lambda i, j, be, nu: (be[i], 0, 0)),
        ],
        out_specs=pl.BlockSpec(memory_space=pl.ANY),
        scratch_shapes=[
            pltpu.SMEM((3 * 2 * rows,), jnp.int32),
            pltpu.VMEM((2, rows * ROW_PITCH, LANES), F32),
            pltpu.VMEM((rows, d), BF16),
            pltpu.VMEM((rows, d), F32),
            pltpu.VMEM((rows * ROW_PITCH, LANES), F32),
            pltpu.SemaphoreType.DMA((3,)),
            pltpu.SemaphoreType.DMA((2,)),
            pltpu.SemaphoreType.DMA,
        ],
    )
    return pl.pallas_call(
        _experts_body,
        out_shape=jax.ShapeDtypeStruct((a_rows * w, LANES), F32),
        grid_spec=grid_spec,
        compiler_params=_cparams(("arbitrary", "arbitrary")),
        name="experts",
    )(block_expert, n_used, slot_idx, slot_gate, xn, w_gu, w_gu, b_gu, b_gu, w_down, b_down)


def _combine_body(final_norm, h_ref, y0_ref, y1_ref, y2_ref, y3_ref, g_ref, o_ref):
    n, d = h_ref.shape
    acc = h_ref[...] + _load_rows(y0_ref, n, d) + _load_rows(y1_ref, n, d) + _load_rows(y2_ref, n, d) \
        + _load_rows(y3_ref, n, d)
    o_ref[...] = _rms(acc, g_ref[...]) if final_norm else acc


def _combine(h2, y_tok, gain, tm):
    t, d = h2.shape
    w = d // LANES
    nt = t // tm
    final_norm = gain is not None
    if gain is None:
        gain = jnp.ones((d,), F32)
    gain = gain.reshape(1, d)
    plane = lambda k: pl.BlockSpec((tm * w, LANES), lambda i: (k * nt + i, 0))
    return pl.pallas_call(
        functools.partial(_combine_body, final_norm),
        out_shape=jax.ShapeDtypeStruct((t, d), F32),
        grid=(nt,),
        in_specs=[pl.BlockSpec((tm, d), lambda i: (i, 0))] + [plane(k) for k in range(TOP_K)]
                 + [pl.BlockSpec((1, d), lambda i: (0, 0))],
        out_specs=pl.BlockSpec((tm, d), lambda i: (i, 0)),
        compiler_params=_cparams(("arbitrary",)),
        name="combine",
    )(h2, y_tok, y_tok, y_tok, y_tok, gain)


def _routing(top_idx, gates, rows, row_width):
    k, t = top_idx.shape
    a = t * k
    flat_e = top_idx.reshape(-1)
    _, s_asg, s_gate = lax.sort((flat_e, jnp.arange(a, dtype=jnp.int32), gates.reshape(-1)), num_keys=1)
    experts = jnp.arange(N_EXPERTS, dtype=jnp.int32)
    counts = jnp.sum((flat_e[None, :] == experts[:, None]).astype(jnp.int32), axis=1)
    start = jnp.cumsum(counts) - counts
    padded = (counts + rows - 1) // rows * rows
    padded_end = jnp.cumsum(padded)
    padded_start = padded_end - padded
    nb = -(-a // rows) + N_EXPERTS
    n_used = (padded_end[-1] // rows).astype(jnp.int32)
    blk = jnp.arange(nb, dtype=jnp.int32)
    bexp = jnp.minimum(jnp.sum((padded_end[None, :] <= (blk * rows)[:, None]).astype(jnp.int32), axis=1),
                       N_EXPERTS - 1)
    bexp = jnp.where(blk < n_used, bexp, bexp[jnp.maximum(n_used - 1, 0)])
    off = blk * rows - padded_start[bexp]
    first = jnp.clip(start[bexp] + off, 0, a)
    nvalid = jnp.where(blk < n_used, jnp.clip(counts[bexp] - off, 0, rows), 0)
    take = lambda arr: jax.vmap(lambda f: lax.dynamic_slice(arr, (f,), (rows,)))(first)
    asg = take(jnp.pad(s_asg, (0, rows)))
    gate = take(jnp.pad(s_gate, (0, rows)))
    r = jnp.arange(rows, dtype=jnp.int32)[None, :]
    valid = r < nvalid[:, None]
    slot_src = jnp.where(valid, asg % t, 0)
    slot_dst = jnp.where(valid, asg, a + r)
    slot_gate = jnp.where(valid, gate, 0.0)
    slot_idx = jnp.concatenate([slot_src, slot_dst], axis=1) * row_width
    return bexp, n_used.reshape(1), slot_idx, slot_gate.reshape(nb, rows, 1)


def _moe_stage(h2, xn, top_idx, gates, w_gate_up, b_gate_up, w_down, b_down, norm_final):
    t, d = h2.shape
    bexp, n_used, slot_idx, slot_gate = _routing(top_idx, gates, MOE_ROWS, d // LANES)
    e = w_gate_up.shape[0]
    y_tok = _experts(bexp, n_used, slot_idx, slot_gate, xn, w_gate_up.astype(BF16),
                     b_gate_up.reshape(e, 1, -1), w_down.astype(BF16), b_down.reshape(e, 1, -1))
    return _combine(h2, y_tok, norm_final, min(256, t))


def _permute_w_in(w_in):
    sp = np.cumsum([0, ATT_WIDTH, KV_LATENT, IDX_WIDTH, IDX_DIM, IDX_HEADS, RET_QK_WIDTH, RET_QK_WIDTH,
                    RET_WIDTH, RET_WIDTH])
    piece = lambda i: w_in[:, int(sp[i]):int(sp[i + 1])]
    d = w_in.shape[0]
    cols = [piece(0), piece(2), piece(7), piece(8), piece(5), piece(6), piece(1), piece(3), piece(4)]
    used = sum(c.shape[1] for c in cols)
    cols.append(jnp.zeros((d, PROJ_WIDTH - used), w_in.dtype))
    return jnp.concatenate(cols, axis=1).astype(BF16)


def _rotary_tables(positions):
    half = RET_QK_DIM // 2
    inv = ROPE_BASE ** (-jnp.arange(0, RET_QK_DIM, 2, dtype=F32) / RET_QK_DIM)
    ang = positions.astype(F32).reshape(-1, 1) * inv
    cos, sin = jnp.cos(ang), jnp.sin(ang)
    return jnp.concatenate([cos, cos], axis=1), jnp.concatenate([-sin, sin], axis=1)


def _front(x, positions, norm_mix, w_in, kv_norm, idx_k_norm):
    b, s, d = x.shape
    t = b * s
    tm = min(1024, t)
    proj = _proj(x.reshape(t, d), norm_mix.reshape(1, d), _permute_w_in(w_in), tm)
    cos2, sin2 = _rotary_tables(positions)
    g_k = jnp.concatenate([idx_k_norm, jnp.zeros((LANES - IDX_DIM,), F32)]).reshape(1, LANES)
    c, kidx, widx, q_rot, k_rot = _prep(proj, cos2, sin2, kv_norm.reshape(1, KV_LATENT), g_k, tm)
    return dict(proj=proj, c=c, kidx=kidx, widx=widx, q_rot=q_rot, k_rot=k_rot)


def _mid_stage(x, mem, y_att, y_ret, w_out, norm_cross, norm_mem, w_cq, w_ck, w_cv, w_co, norm_ffn,
               w_router, b_router):
    b, s, d = x.shape
    km, vm = _memkv(mem, norm_mem.reshape(1, d), w_ck.astype(BF16), w_cv.astype(BF16))
    w_out_b16 = w_out.astype(BF16)
    return _mid(x.reshape(b * s, d), y_att, y_ret, w_out_b16[:ATT_WIDTH], w_out_b16[ATT_WIDTH:],
                norm_cross.reshape(1, d), w_cq.astype(BF16), km, vm, w_co.astype(BF16),
                norm_ffn.reshape(1, d), w_router.T.astype(BF16), b_router.reshape(N_EXPERTS, 1),
                b, s, min(512, s))


def _ret_stage(st, b, s):
    q_rot = st["q_rot"].reshape(b, s, RET_QK_WIDTH)
    kt_rot = jnp.swapaxes(st["k_rot"].reshape(b, s, RET_QK_WIDTH), 1, 2)
    return _retention(q_rot, kt_rot, st["proj"], b, s)


def kernel(x, mem, positions, norm_mix, w_in, kv_norm, idx_k_norm, w_uk, w_uv, rel_bias, w_out, norm_cross,
           norm_mem, w_cq, w_ck, w_cv, w_co, norm_ffn, w_router, b_router, w_gate_up, b_gate_up, w_down, b_down,
           norm_final):
    b, s, d = x.shape
    depth = norm_mix.shape[0]
    h = x
    for l in range(depth):
        st = _front(h, positions, norm_mix[l], w_in[l], kv_norm[l], idx_k_norm[l])
        y_att = _dsa_stage(st, w_uk[l], w_uv[l], rel_bias, b, s)
        y_ret = _ret_stage(st, b, s)
        h2, xn, top_idx, gates = _mid_stage(h, mem, y_att, y_ret, w_out[l], norm_cross[l], norm_mem[l], w_cq[l],
                                            w_ck[l], w_cv[l], w_co[l], norm_ffn[l], w_router[l], b_router[l])
        gain = norm_final if l == depth - 1 else None
        h = _moe_stage(h2, xn, top_idx, gates, w_gate_up[l], b_gate_up[l], w_down[l], b_down[l], gain).reshape(b, s, d)
    return h
```

```python
import functools
import math

import jax
import jax.numpy as jnp
import numpy as np
from jax import lax
from jax.experimental import pallas as pl
from jax.experimental.pallas import tpu as pltpu

F32 = jnp.float32
BF16 = jnp.bfloat16

D_MODEL = 2048
N_ATT_HEADS = 8
ATT_HEAD_DIM = 128
KV_LATENT = 256
IDX_HEADS = 16
IDX_DIM = 64
IDX_TOPK_MAX = 256
N_RET_HEADS = 4
RET_QK_DIM = 128
RET_V_DIM = 256
ROPE_BASE = 10000.0
N_BUCKETS = 32
MAX_DISTANCE = 128
N_CROSS_HEADS = 4
CROSS_HEAD_DIM = 128
N_EXPERTS = 32
TOP_K = 4
D_FF = D_MODEL
SWIGLU_LIMIT = 7.0
SWIGLU_ALPHA = 1.702
EPS = 1e-6

ATT_WIDTH = N_ATT_HEADS * ATT_HEAD_DIM
RET_WIDTH = N_RET_HEADS * RET_V_DIM
RET_QK_WIDTH = N_RET_HEADS * RET_QK_DIM
IDX_WIDTH = IDX_HEADS * IDX_DIM
CROSS_WIDTH = N_CROSS_HEADS * CROSS_HEAD_DIM

LANES = 128
VMEM_LIMIT = 56 * 1024 * 1024

COL_Q_ATT = 0
COL_Q_IDX = COL_Q_ATT + ATT_WIDTH
COL_V_RET = COL_Q_IDX + IDX_WIDTH
COL_G_RET = COL_V_RET + RET_WIDTH
COL_Q_RET = COL_G_RET + RET_WIDTH
COL_K_RET = COL_Q_RET + RET_QK_WIDTH
COL_C_KV = COL_K_RET + RET_QK_WIDTH
COL_KW = COL_C_KV + KV_LATENT
PROJ_TN = 512
PROJ_WIDTH = 5632

Q_BLOCK = 128
KEY_TILE = 512
KEY_PAD = KEY_TILE - Q_BLOCK
NEG_BIG = -1e30
LOG2E = math.log2(math.e)

RET_CHUNK = 256

MOE_ROWS = 512
IDX_RING = 3
MOE_TF = 1024


def _cparams(sem, vmem=VMEM_LIMIT):
    return pltpu.CompilerParams(dimension_semantics=sem, vmem_limit_bytes=vmem)


def _rms(x, g):
    return x * lax.rsqrt(jnp.mean(x * x, axis=-1, keepdims=True) + EPS) * g


ROW_PITCH = 20


def _store_rows(ref, x, pitch=None):
    n, d = x.shape
    w = d // LANES
    for a in range(w):
        ref[pl.ds(a, n, stride=pitch or w), :] = x[:, a * LANES:(a + 1) * LANES].astype(ref.dtype)


def _load_rows(ref, n, d, pitch=None):
    w = d // LANES
    return jnp.concatenate([ref[pl.ds(a, n, stride=pitch or w), :] for a in range(w)], axis=1)


def _proj_body(x_ref, g_ref, w_ref, o_ref, xn_ref):
    @pl.when(pl.program_id(1) == 0)
    def _():
        xn_ref[...] = _rms(x_ref[...], g_ref[...]).astype(BF16)

    o_ref[...] = jnp.dot(xn_ref[...], w_ref[...], preferred_element_type=F32).astype(o_ref.dtype)


def _proj(x2, gain, w_p, tm):
    t, d = x2.shape
    n = w_p.shape[1]
    return pl.pallas_call(
        _proj_body,
        out_shape=jax.ShapeDtypeStruct((t, n), BF16),
        grid=(t // tm, n // PROJ_TN),
        in_specs=[
            pl.BlockSpec((tm, d), lambda i, j: (i, 0)),
            pl.BlockSpec((1, d), lambda i, j: (0, 0)),
            pl.BlockSpec((d, PROJ_TN), lambda i, j: (0, j)),
        ],
        out_specs=pl.BlockSpec((tm, PROJ_TN), lambda i, j: (i, j)),
        scratch_shapes=[pltpu.VMEM((tm, d), BF16)],
        compiler_params=_cparams(("arbitrary", "arbitrary")),
        name="proj",
    )(x2, gain, w_p)


def _prep_body(c_ref, kw_ref, q_ref, k_ref, cos_ref, sin_ref, gkv_ref, gk_ref,
               c_out, kidx_out, widx_out, q_out, k_out):
    c_out[...] = _rms(c_ref[...].astype(F32), gkv_ref[...]).astype(BF16)

    kw = kw_ref[...].astype(F32)
    lane = lax.broadcasted_iota(jnp.int32, kw.shape, 1)
    is_k = lane < IDX_DIM
    kk = jnp.where(is_k, kw, 0.0)
    ms = jnp.sum(kk * kk, axis=-1, keepdims=True) * (1.0 / IDX_DIM)
    kidx_out[...] = (kk * lax.rsqrt(ms + EPS) * gk_ref[...]).astype(BF16)
    widx_out[...] = kw * (IDX_HEADS ** -0.5 * IDX_DIM ** -0.5)

    cos2 = cos_ref[...]
    sin2 = sin_ref[...]
    for h in range(N_RET_HEADS):
        sl = slice(h * RET_QK_DIM, (h + 1) * RET_QK_DIM)
        qh = q_ref[:, sl].astype(F32)
        kh = k_ref[:, sl].astype(F32)
        q_out[:, sl] = (qh * cos2 + pltpu.roll(qh, RET_QK_DIM // 2, 1) * sin2).astype(BF16)
        k_out[:, sl] = ((kh * cos2 + pltpu.roll(kh, RET_QK_DIM // 2, 1) * sin2)
                        * (RET_QK_DIM ** -0.5)).astype(BF16)


def _prep(proj, cos2, sin2, g_kv, g_k, tm):
    t = proj.shape[0]
    row = lambda i: (i, 0)
    return pl.pallas_call(
        _prep_body,
        out_shape=(
            jax.ShapeDtypeStruct((t, KV_LATENT), BF16),
            jax.ShapeDtypeStruct((t, LANES), BF16),
            jax.ShapeDtypeStruct((t, LANES), F32),
            jax.ShapeDtypeStruct((t, RET_QK_WIDTH), BF16),
            jax.ShapeDtypeStruct((t, RET_QK_WIDTH), BF16),
        ),
        grid=(t // tm,),
        in_specs=[
            pl.BlockSpec((tm, KV_LATENT), lambda i: (i, COL_C_KV // KV_LATENT)),
            pl.BlockSpec((tm, LANES), lambda i: (i, COL_KW // LANES)),
            pl.BlockSpec((tm, RET_QK_WIDTH), lambda i: (i, COL_Q_RET // RET_QK_WIDTH)),
            pl.BlockSpec((tm, RET_QK_WIDTH), lambda i: (i, COL_K_RET // RET_QK_WIDTH)),
            pl.BlockSpec((tm, LANES), row),
            pl.BlockSpec((tm, LANES), row),
            pl.BlockSpec((1, KV_LATENT), lambda i: (0, 0)),
            pl.BlockSpec((1, LANES), lambda i: (0, 0)),
        ],
        out_specs=(
            pl.BlockSpec((tm, KV_LATENT), row),
            pl.BlockSpec((tm, LANES), row),
            pl.BlockSpec((tm, LANES), row),
            pl.BlockSpec((tm, RET_QK_WIDTH), row),
            pl.BlockSpec((tm, RET_QK_WIDTH), row),
        ),
        compiler_params=_cparams(("arbitrary",)),
        name="prep",
    )(proj, proj, proj, proj, cos2, sin2, g_kv, g_k)


def _ret_body(q_ref, kt_ref, v_ref, g_ref, dec_ref, cross_ref, state_ref, cd_ref, o_ref, r_ref):
    @pl.when(pl.program_id(1) == 0)
    def _():
        r_ref[...] = jnp.zeros_like(r_ref)

    for h in range(N_RET_HEADS):
        qs = slice(h * RET_QK_DIM, (h + 1) * RET_QK_DIM)
        vs = slice(h * RET_V_DIM, (h + 1) * RET_V_DIM)
        q = q_ref[0, :, qs]
        kt = kt_ref[0, qs, :]
        v = v_ref[:, vs]
        r_old = r_ref[h]
        inner = jnp.dot(q, kt, preferred_element_type=F32) * dec_ref[h]
        qc = (q.astype(F32) * cross_ref[h]).astype(BF16)
        o = (jnp.dot(inner.astype(BF16), v, preferred_element_type=F32)
             + jnp.dot(qc, r_old.astype(BF16), preferred_element_type=F32))
        ks = (kt.astype(F32) * state_ref[h]).astype(BF16)
        r_ref[h] = r_old * cd_ref[h] + jnp.dot(ks, v, preferred_element_type=F32)
        o = o * lax.rsqrt(jnp.mean(o * o, axis=-1, keepdims=True) + EPS)
        g = g_ref[:, vs].astype(F32)
        o_ref[:, vs] = (g * jax.nn.sigmoid(g) * o).astype(o_ref.dtype)


def _retention(q_rot, kt_rot, proj, b, s):
    c = RET_CHUNK
    n = s // c
    hh = N_RET_HEADS
    log_gamma = jnp.log(1.0 - 2.0 ** (-5.0 - jnp.arange(hh, dtype=F32)))
    j = jnp.arange(c, dtype=F32)
    diff = j[:, None] - j[None, :]
    dec = jnp.where(diff >= 0, jnp.exp(log_gamma[:, None, None] * jnp.maximum(diff, 0.0)), 0.0)
    cross = jnp.broadcast_to(jnp.exp(log_gamma[:, None] * (j + 1.0))[:, :, None], (hh, c, RET_QK_DIM))
    state = jnp.exp(log_gamma[:, None] * (c - 1.0 - j))[:, None, :]
    cdec = jnp.broadcast_to(jnp.exp(log_gamma * c)[:, None, None], (hh, 1, RET_V_DIM))
    const = lambda bi, ci: (0, 0, 0)
    return pl.pallas_call(
        _ret_body,
        out_shape=jax.ShapeDtypeStruct((b * s, RET_WIDTH), BF16),
        grid=(b, n),
        in_specs=[
            pl.BlockSpec((1, c, RET_QK_WIDTH), lambda bi, ci: (bi, ci, 0)),
            pl.BlockSpec((1, RET_QK_WIDTH, c), lambda bi, ci: (bi, 0, ci)),
            pl.BlockSpec((c, RET_WIDTH), lambda bi, ci: (bi * n + ci, COL_V_RET // RET_WIDTH)),
            pl.BlockSpec((c, RET_WIDTH), lambda bi, ci: (bi * n + ci, COL_G_RET // RET_WIDTH)),
            pl.BlockSpec((hh, c, c), const),
            pl.BlockSpec((hh, c, RET_QK_DIM), const),
            pl.BlockSpec((hh, 1, c), const),
            pl.BlockSpec((hh, 1, RET_V_DIM), const),
        ],
        out_specs=pl.BlockSpec((c, RET_WIDTH), lambda bi, ci: (bi * n + ci, 0)),
        scratch_shapes=[pltpu.VMEM((hh, RET_QK_DIM, RET_V_DIM), F32)],
        compiler_params=_cparams(("arbitrary", "arbitrary")),
        name="ret",
    )(q_rot, kt_rot, proj, proj, dec, cross, state, cdec)


def _dsa_body(topk, kidx_ref, qit_ref, w_ref, c_ref, ct_ref, qt_ref, wuk_ref, wuvt_ref, bias_ref, y_ref,
              sc_ref, qlt_ref, acc_ref, m_ref, l_ref):
    qb = pl.program_id(1)
    tk = KEY_TILE
    half = tk // 2
    e_pad = (qb + 1) * Q_BLOCK + KEY_PAD
    n_tiles = e_pad // tk
    nh = N_ATT_HEADS
    lane_q = qb * Q_BLOCK + lax.broadcasted_iota(jnp.int32, (1, Q_BLOCK), 1)

    def tile_start(j):
        return pl.multiple_of(e_pad - (j + 1) * tk, Q_BLOCK)

    for h in range(nh):
        sl = slice(h * Q_BLOCK, (h + 1) * Q_BLOCK)
        hd = slice(h * ATT_HEAD_DIM, (h + 1) * ATT_HEAD_DIM)
        qlt_ref[:, sl] = (jnp.dot(wuk_ref[h], qt_ref[0, 0, hd, :], preferred_element_type=F32)
                          * (ATT_HEAD_DIM ** -0.5 * LOG2E)).astype(BF16)

    def score_tile(j, carry):
        r0 = tile_start(j)
        for part in range(2):
            rs = pl.multiple_of(r0 + part * half, Q_BLOCK)
            z = jnp.dot(kidx_ref[0, pl.ds(rs, half), :], qit_ref[0, 0], preferred_element_type=F32)
            tot = jnp.zeros((half, Q_BLOCK), F32)
            for h in range(IDX_HEADS):
                sl = slice(h * Q_BLOCK, (h + 1) * Q_BLOCK)
                tot = tot + jnp.maximum(z[:, sl], 0.0) * w_ref[0, 0, :, sl]
            key = rs - KEY_PAD + lax.broadcasted_iota(jnp.int32, (half, Q_BLOCK), 0)
            ok = (key >= 0) & (key <= lane_q)
            sc_ref[pl.ds(rs, half), :] = jnp.where(ok, tot, NEG_BIG)
        return carry
    lax.fori_loop(0, n_tiles, score_tile, 0)

    def minmax_tile(j, carry):
        lo, hi = carry
        s = sc_ref[pl.ds(tile_start(j), tk), :]
        hi = jnp.maximum(hi, jnp.max(s, axis=0, keepdims=True))
        lo = jnp.minimum(lo, jnp.min(jnp.where(s > 0.5 * NEG_BIG, s, -NEG_BIG), axis=0, keepdims=True))
        return lo, hi
    lo0, hi0 = lax.fori_loop(0, n_tiles, minmax_tile,
                             (jnp.full((1, Q_BLOCK), -NEG_BIG, F32), jnp.full((1, Q_BLOCK), NEG_BIG, F32)))
    search = lane_q + 1 > topk

    n_chain = 4
    rows_chain = tk // n_chain

    def count_ge(x):
        def body(j, cnts):
            r0 = tile_start(j)
            out = []
            for k in range(n_chain):
                s = sc_ref[pl.ds(pl.multiple_of(r0 + k * rows_chain, Q_BLOCK), rows_chain), :]
                ind = jnp.where(s >= x, 1.0, 0.0).reshape(rows_chain // 8, 8, Q_BLOCK)
                out.append(cnts[k] + jnp.sum(ind, axis=0))
            return tuple(out)
        cnts = lax.fori_loop(0, n_tiles, body, tuple(jnp.zeros((8, Q_BLOCK), F32) for _ in range(n_chain)))
        return jnp.sum((cnts[0] + cnts[1]) + (cnts[2] + cnts[3]), axis=0, keepdims=True)

    steps_per_check = 3

    def bis_cond(st):
        it, _, _, _, active = st
        return (it < 42) & (active > 0.0)

    def bis_step(_, st):
        lo, hi, done = st
        mid = lo + 0.5 * (hi - lo)
        cnt = count_ge(mid)
        ge = cnt >= topk
        open_ = done < 0.5
        lo = jnp.where(open_ & ge, mid, lo)
        hi = jnp.where(open_ & (~ge), mid, hi)
        done = jnp.where(cnt == topk, 1.0, done)
        return lo, hi, done

    def bis_body(st):
        it, lo, hi, done, _ = st
        lo, hi, done = lax.fori_loop(0, steps_per_check, bis_step, (lo, hi, done))
        return it + steps_per_check, lo, hi, done, jnp.max(1.0 - done)

    done0 = jnp.where(search, 0.0, 1.0)
    _, lo_f, _, _, _ = lax.while_loop(bis_cond, bis_body, (jnp.int32(0), lo0, hi0, done0, jnp.max(1.0 - done0)))
    thr = jnp.where(search, lo_f, 0.5 * NEG_BIG)

    m_ref[...] = jnp.full(m_ref.shape, NEG_BIG, F32)
    l_ref[...] = jnp.zeros(l_ref.shape, F32)
    acc_ref[...] = jnp.zeros(acc_ref.shape, F32)

    n_grp = 4
    hg = nh // n_grp

    def attend(segments):
        scores, ct_tiles = [], []
        for rs, n, with_bias in segments:
            c_tile = c_ref[0, pl.ds(rs, n), :]
            ct_tiles.append(ct_ref[0, :, pl.ds(rs, n)])
            pen = jnp.where(sc_ref[pl.ds(rs, n), :] >= thr, 0.0, NEG_BIG)
            pen = jnp.concatenate([pen] * hg, axis=1)
            for g in range(n_grp):
                ls = slice(g * hg * Q_BLOCK, (g + 1) * hg * Q_BLOCK)
                st = jnp.dot(c_tile, qlt_ref[:, ls], preferred_element_type=F32) + pen
                if with_bias:
                    st = st + bias_ref[:, ls]
                scores.append(st)
        for k, ct_tile in enumerate(ct_tiles):
            for g in range(n_grp):
                ls = slice(g * hg * Q_BLOCK, (g + 1) * hg * Q_BLOCK)
                st = scores[k * n_grp + g]
                m_old = m_ref[:, ls]
                m_new = jnp.maximum(m_old, jnp.max(st, axis=0, keepdims=True))
                alpha = jnp.exp2(m_old - m_new)
                p = jnp.exp2(st - m_new)
                l_ref[:, ls] = alpha * l_ref[:, ls] + jnp.sum(p, axis=0, keepdims=True)
                acc_ref[:, ls] = acc_ref[:, ls] * alpha + jnp.dot(ct_tile, p.astype(BF16),
                                                                  preferred_element_type=F32)
                m_ref[:, ls] = m_new

    r_first = tile_start(0)
    attend([(pl.multiple_of(r_first + half, Q_BLOCK), half, True), (r_first, half, False)])

    def attend_pair(i, carry):
        attend([(tile_start(2 * i + 1), tk, False), (tile_start(2 * i + 2), tk, False)])
        return carry
    n_rest = n_tiles - 1
    lax.fori_loop(0, n_rest // 2, attend_pair, 0)

    @pl.when(n_rest % 2 == 1)
    def _():
        attend([(tile_start(n_tiles - 1), tk, False)])

    ot = acc_ref[...] / l_ref[...]
    for h in range(nh):
        sl = slice(h * Q_BLOCK, (h + 1) * Q_BLOCK)
        y_ref[0, h * ATT_HEAD_DIM:(h + 1) * ATT_HEAD_DIM, :] = jnp.dot(
            wuvt_ref[h], ot[:, sl].astype(BF16), preferred_element_type=F32).astype(y_ref.dtype)


def _dsa(kidx_p, qit, widx_t, c_p, ct_p, qt, wuk_h, wuvt_h, bias_t, b, s, topk):
    nqb = s // Q_BLOCK
    sp = s + KEY_PAD
    nh = N_ATT_HEADS
    per_b = lambda bi, qi: (bi, 0, 0)
    per_q = lambda bi, qi: (bi, qi, 0, 0)
    c3 = lambda bi, qi: (0, 0, 0)
    return pl.pallas_call(
        functools.partial(_dsa_body, topk),
        out_shape=jax.ShapeDtypeStruct((b, ATT_WIDTH, s), BF16),
        grid=(b, nqb),
        in_specs=[
            pl.BlockSpec((1, sp, LANES), per_b),
            pl.BlockSpec((1, 1, LANES, IDX_HEADS * Q_BLOCK), per_q),
            pl.BlockSpec((1, 1, 1, IDX_HEADS * Q_BLOCK), per_q),
            pl.BlockSpec((1, sp, KV_LATENT), per_b),
            pl.BlockSpec((1, KV_LATENT, sp), per_b),
            pl.BlockSpec((1, 1, ATT_WIDTH, Q_BLOCK), per_q),
            pl.BlockSpec((nh, KV_LATENT, ATT_HEAD_DIM), c3),
            pl.BlockSpec((nh, ATT_HEAD_DIM, KV_LATENT), c3),
            pl.BlockSpec((2 * Q_BLOCK, nh * Q_BLOCK), lambda bi, qi: (0, 0)),
        ],
        out_specs=pl.BlockSpec((1, ATT_WIDTH, Q_BLOCK), lambda bi, qi: (bi, 0, qi)),
        scratch_shapes=[
            pltpu.VMEM((sp, Q_BLOCK), F32),
            pltpu.VMEM((KV_LATENT, nh * Q_BLOCK), BF16),
            pltpu.VMEM((KV_LATENT, nh * Q_BLOCK), F32),
            pltpu.VMEM((1, nh * Q_BLOCK), F32),
            pltpu.VMEM((1, nh * Q_BLOCK), F32),
        ],
        compiler_params=_cparams(("arbitrary", "arbitrary")),
        name="dsa",
    )(kidx_p, qit, widx_t, c_p, ct_p, qt, wuk_h, wuvt_h, bias_t)


def _t5_bucket(dist):
    n = jnp.maximum(dist, 0)
    max_exact = N_BUCKETS // 2
    nf = jnp.maximum(n, 1).astype(F32)
    large = max_exact + (jnp.log(nf / max_exact) / math.log(MAX_DISTANCE / max_exact)
                         * (N_BUCKETS - max_exact)).astype(jnp.int32)
    return jnp.where(n < max_exact, n, jnp.minimum(large, N_BUCKETS - 1))


def _dsa_stage(st, w_uk, w_uv, rel_bias, b, s):
    nqb = s // Q_BLOCK
    nh = N_ATT_HEADS
    proj = st["proj"].reshape(b, nqb, Q_BLOCK, PROJ_WIDTH)
    qt = jnp.swapaxes(proj[..., COL_Q_ATT:COL_Q_ATT + ATT_WIDTH], 2, 3)
    qi = proj[..., COL_Q_IDX:COL_Q_IDX + IDX_WIDTH].reshape(b, nqb, Q_BLOCK, IDX_HEADS, IDX_DIM)
    qit = jnp.transpose(qi, (0, 1, 4, 3, 2)).reshape(b, nqb, IDX_DIM, IDX_HEADS * Q_BLOCK)
    qit = jnp.pad(qit, ((0, 0), (0, 0), (0, LANES - IDX_DIM), (0, 0)))
    wi = st["widx"].reshape(b, nqb, Q_BLOCK, LANES)[..., IDX_DIM:IDX_DIM + IDX_HEADS]
    widx_t = jnp.swapaxes(wi, 2, 3).reshape(b, nqb, 1, IDX_HEADS * Q_BLOCK)
    front = ((0, 0), (KEY_PAD, 0), (0, 0))
    kidx_p = jnp.pad(st["kidx"].reshape(b, s, LANES), front)
    c_p = jnp.pad(st["c"].reshape(b, s, KV_LATENT), front)
    ct_p = jnp.swapaxes(c_p, 1, 2)
    wuk_h = jnp.transpose(w_uk, (1, 0, 2)).astype(BF16)
    wuvt_h = jnp.transpose(w_uv, (1, 2, 0)).astype(BF16)
    i = jnp.arange(2 * Q_BLOCK)[:, None]
    j = jnp.arange(Q_BLOCK)[None, :]
    rb = (rel_bias - rel_bias[N_BUCKETS - 1]) * LOG2E
    bias = rb[_t5_bucket(j - i + Q_BLOCK)]
    bias_t = jnp.transpose(bias, (0, 2, 1)).reshape(2 * Q_BLOCK, nh * Q_BLOCK)
    topk = min(IDX_TOPK_MAX, s // 4)
    yt = _dsa(kidx_p, qit, widx_t, c_p, ct_p, qt, wuk_h, wuvt_h, bias_t, b, s, topk)
    return jnp.swapaxes(yt, 1, 2).reshape(b * s, ATT_WIDTH)


def _memkv_body(m_ref, g_ref, wk_ref, wv_ref, k_ref, v_ref):
    mn = _rms(m_ref[0], g_ref[...]).astype(BF16)
    k_ref[0] = jnp.dot(mn, wk_ref[...], preferred_element_type=F32).astype(BF16)
    v_ref[0] = jnp.dot(mn, wv_ref[...], preferred_element_type=F32).astype(BF16)


def _memkv(mem, gain, w_ck, w_cv):
    b, m, d = mem.shape
    w = w_ck.shape[1]
    const = lambda i: (0, 0)
    out = jax.ShapeDtypeStruct((b, m, w), BF16)
    return pl.pallas_call(
        _memkv_body,
        out_shape=(out, out),
        grid=(b,),
        in_specs=[
            pl.BlockSpec((1, m, d), lambda i: (i, 0, 0)),
            pl.BlockSpec((1, d), const),
            pl.BlockSpec((d, w), const),
            pl.BlockSpec((d, w), const),
        ],
        out_specs=(pl.BlockSpec((1, m, w), lambda i: (i, 0, 0)), pl.BlockSpec((1, m, w), lambda i: (i, 0, 0))),
        compiler_params=_cparams(("arbitrary",)),
        name="memkv",
    )(mem, gain, w_ck, w_cv)


def _mid_body(x_ref, ya_ref, yr_ref, woa_ref, wob_ref, gc_ref, wq_ref, km_ref, vm_ref, wo_ref,
              gf_ref, wr_ref, br_ref, h2_ref, xn_ref, idx_ref, gate_ref):
    h1 = (x_ref[...]
          + jnp.dot(ya_ref[...], woa_ref[...], preferred_element_type=F32)
          + jnp.dot(yr_ref[...], wob_ref[...], preferred_element_type=F32))
    hn = _rms(h1, gc_ref[...]).astype(BF16)
    q = jnp.dot(hn, wq_ref[...], preferred_element_type=F32).astype(BF16)
    heads = []
    for h in range(N_CROSS_HEADS):
        sl = slice(h * CROSS_HEAD_DIM, (h + 1) * CROSS_HEAD_DIM)
        s = lax.dot_general(q[:, sl], km_ref[0, :, sl], (((1,), (1,)), ((), ())),
                            preferred_element_type=F32) * (CROSS_HEAD_DIM ** -0.5)
        p = jnp.exp(s - jnp.max(s, axis=-1, keepdims=True))
        p = p / jnp.sum(p, axis=-1, keepdims=True)
        heads.append(jnp.dot(p.astype(BF16), vm_ref[0, :, sl], preferred_element_type=F32).astype(BF16))
    o = jnp.concatenate(heads, axis=-1)
    h2 = h1 + jnp.dot(o, wo_ref[...], preferred_element_type=F32)
    h2_ref[...] = h2

    xn = _rms(h2, gf_ref[...])
    _store_rows(xn_ref, xn)
    logit = lax.dot_general(wr_ref[...], xn.astype(BF16), (((1,), (1,)), ((), ())),
                            preferred_element_type=F32) + br_ref[...]
    eid = lax.broadcasted_iota(jnp.int32, logit.shape, 0)
    vals, ids = [], []
    for _ in range(TOP_K):
        mx = jnp.max(logit, axis=0, keepdims=True)
        sel = jnp.min(jnp.where(logit == mx, eid, N_EXPERTS), axis=0, keepdims=True)
        vals.append(mx)
        ids.append(sel)
        logit = jnp.where(eid == sel, -jnp.inf, logit)
    top = jnp.concatenate(vals, axis=0)
    e = jnp.exp(top - vals[0])
    gate_ref[...] = e / jnp.sum(e, axis=0, keepdims=True)
    idx_ref[...] = jnp.concatenate(ids, axis=0)


def _mid(x2, ya, yr, w_out_a, w_out_b, g_cross, w_cq, km, vm, w_co, g_ffn, w_rt, b_r, b, s, tm):
    t, d = x2.shape
    nt = s // tm
    row = lambda bi, i: (bi * nt + i, 0)
    const = lambda bi, i: (0, 0)
    once = pl.Buffered(1)
    m = km.shape[1]
    return pl.pallas_call(
        _mid_body,
        out_shape=(
            jax.ShapeDtypeStruct((t, d), F32),
            jax.ShapeDtypeStruct((t * (d // LANES), LANES), F32),
            jax.ShapeDtypeStruct((TOP_K, t), jnp.int32),
            jax.ShapeDtypeStruct((TOP_K, t), F32),
        ),
        grid=(b, nt),
        in_specs=[
            pl.BlockSpec((tm, d), row),
            pl.BlockSpec((tm, ATT_WIDTH), row),
            pl.BlockSpec((tm, RET_WIDTH), row),
            pl.BlockSpec((ATT_WIDTH, d), const, pipeline_mode=once),
            pl.BlockSpec((RET_WIDTH, d), const, pipeline_mode=once),
            pl.BlockSpec((1, d), const),
            pl.BlockSpec((d, CROSS_WIDTH), const, pipeline_mode=once),
            pl.BlockSpec((1, m, CROSS_WIDTH), lambda bi, i: (bi, 0, 0)),
            pl.BlockSpec((1, m, CROSS_WIDTH), lambda bi, i: (bi, 0, 0)),
            pl.BlockSpec((CROSS_WIDTH, d), const, pipeline_mode=once),
            pl.BlockSpec((1, d), const),
            pl.BlockSpec((N_EXPERTS, d), const),
            pl.BlockSpec((N_EXPERTS, 1), const),
        ],
        out_specs=(
            pl.BlockSpec((tm, d), row),
            pl.BlockSpec((tm * (d // LANES), LANES), row),
            pl.BlockSpec((TOP_K, tm), lambda bi, i: (0, bi * nt + i)),
            pl.BlockSpec((TOP_K, tm), lambda bi, i: (0, bi * nt + i)),
        ),
        compiler_params=_cparams(("arbitrary", "arbitrary")),
        name="mid",
    )(x2, ya, yr, w_out_a, w_out_b, g_cross, w_cq, km, vm, w_co, g_ffn, w_rt, b_r)


def _experts_body(be_ref, nu_ref, idx_hbm, gate_ref, xn_hbm, wg_ref, wu_ref, bg_ref, bu_ref, wd_ref, bd_ref,
                  y_hbm, idx_smem, xbuf, xb_ref, acc_ref, ybuf, sem_idx, sem_g, sem_s):
    i = pl.program_id(0)
    j = pl.program_id(1)
    n_used = nu_ref[0]
    rows, d = xb_ref.shape
    w = d // LANES

    def idx_copy(blk, slot):
        return pltpu.make_async_copy(idx_hbm.at[blk],
                                     idx_smem.at[pl.ds(pl.multiple_of(slot * 2 * rows, 2 * rows), 2 * rows)],
                                     sem_idx.at[slot])

    def gather_start(islot, xslot):
        def body(r, carry):
            src = pl.multiple_of(idx_smem[islot * 2 * rows + r], w)
            pltpu.make_async_copy(xn_hbm.at[pl.ds(src, w)], xbuf.at[xslot, pl.ds(r * ROW_PITCH, w)],
                                  sem_g.at[xslot]).start()
            return carry
        lax.fori_loop(0, rows, body, 0, unroll=8)

    def gather_wait(xslot):
        pltpu.make_async_copy(xn_hbm.at[pl.ds(0, rows * w)], xbuf.at[xslot, pl.ds(0, rows * w)],
                              sem_g.at[xslot]).wait()

    def scatter_wait():
        pltpu.make_async_copy(ybuf.at[pl.ds(0, rows * w)], y_hbm.at[pl.ds(0, rows * w)], sem_s).wait()

    @pl.when(j == 0)
    def _():
        @pl.when(i == 0)
        def _():
            ybuf[...] = jnp.zeros_like(ybuf)
            dump = pltpu.make_async_copy(ybuf.at[pl.ds(0, rows * w)],
                                         y_hbm.at[pl.ds(y_hbm.shape[0] - rows * w, rows * w)], sem_s)
            dump.start()
            dump.wait()
            idx_copy(0, 0).start()

            @pl.when(n_used > 1)
            def _():
                idx_copy(1, 1).start()

            idx_copy(0, 0).wait()
            gather_start(0, 0)

        @pl.when(i + 1 < n_used)
        def _():
            idx_copy(i + 1, (i + 1) % IDX_RING).wait()

        @pl.when(i + 2 < n_used)
        def _():
            idx_copy(i + 2, (i + 2) % IDX_RING).start()

        @pl.when(i < n_used)
        def _():
            gather_wait(i % 2)
            xb_ref[...] = _load_rows(xbuf.at[i % 2], rows, d, ROW_PITCH).astype(BF16)

    @pl.when(i < n_used)
    def _():
        nxt = jnp.minimum(i + 1, n_used - 1)
        per_step = rows // (D_FF // wd_ref.shape[1])
        gbase = (nxt % IDX_RING) * 2 * rows + j * per_step
        xnext = xbuf.at[(i + 1) % 2]
        for k in range(per_step):
            src = pl.multiple_of(idx_smem[gbase + k], w)
            pltpu.make_async_copy(xn_hbm.at[pl.ds(src, w)], xnext.at[pl.ds((j * per_step + k) * ROW_PITCH, w)],
                                  sem_g.at[(i + 1) % 2]).start()

        xb = xb_ref[...]
        g = jnp.dot(xb, wg_ref[0], preferred_element_type=F32) + bg_ref[0]
        u = jnp.dot(xb, wu_ref[0], preferred_element_type=F32) + bu_ref[0]
        gt = jnp.minimum(g, SWIGLU_LIMIT)
        up = jnp.clip(u, -SWIGLU_LIMIT, SWIGLU_LIMIT)
        hm = ((up + 1.0) * (gt * jax.nn.sigmoid(SWIGLU_ALPHA * gt))).astype(BF16)
        part = jnp.dot(hm, wd_ref[0], preferred_element_type=F32)

        @pl.when(j == 0)
        def _():
            acc_ref[...] = part

        @pl.when(j > 0)
        def _():
            acc_ref[...] += part

        @pl.when(j == pl.num_programs(1) - 1)
        def _():
            y = (acc_ref[...] + bd_ref[0]) * gate_ref[0]

            @pl.when(i > 0)
            def _():
                scatter_wait()

            _store_rows(ybuf, y, ROW_PITCH)
            islot = i % IDX_RING

            def body(r, carry):
                dst = pl.multiple_of(idx_smem[islot * 2 * rows + rows + r], w)
                pltpu.make_async_copy(ybuf.at[pl.ds(r * ROW_PITCH, w)], y_hbm.at[pl.ds(dst, w)], sem_s).start()
                return carry
            lax.fori_loop(0, rows, body, 0, unroll=8)

            @pl.when(i == n_used - 1)
            def _():
                scatter_wait()
                gather_wait((i + 1) % 2)


def _experts(block_expert, n_used, slot_idx, slot_gate, xn, w_gu, b_gu, w_down, b_down):
    nb = slot_idx.shape[0]
    rows = slot_idx.shape[1] // 2
    e, d, ff2 = w_gu.shape
    w = d // LANES
    t = xn.shape[0] // w
    ff = ff2 // 2
    tf = MOE_TF
    nj = ff // tf
    a_rows = t * TOP_K + rows
    grid_spec = pltpu.PrefetchScalarGridSpec(
        num_scalar_prefetch=2,
        grid=(nb, nj),
        in_specs=[
            pl.BlockSpec(memory_space=pl.ANY),
            pl.BlockSpec((1, rows, 1), lambda i, j, be, nu: (i, 0, 0)),
            pl.BlockSpec(memory_space=pl.ANY),
            pl.BlockSpec((1, d, tf), lambda i, j, be, nu: (be[i], 0, j)),
            pl.BlockSpec((1, d, tf), lambda i, j, be, nu: (be[i], 0, j + nj)),
            pl.BlockSpec((1, 1, tf), lambda i, j, be, nu: (be[i], 0, j)),
            pl.BlockSpec((1, 1, tf), lambda i, j, be, nu: (be[i], 0, j + nj)),
            pl.BlockSpec((1, tf, d), lambda i, j, be, nu: (be[i], j, 0)),
            pl.BlockSpec((1, 1, d), lambda i, j, be, nu: (be[i], 0, 0)),
        ],
        out_specs=pl.BlockSpec(memory_space=pl.ANY),
        scratch_shapes=[
            pltpu.SMEM((IDX_RING * 2 * rows,), jnp.int32),
            pltpu.VMEM((2, rows * ROW_PITCH, LANES), F32),
            pltpu.VMEM((rows, d), BF16),
            pltpu.VMEM((rows, d), F32),
            pltpu.VMEM((rows * ROW_PITCH, LANES), F32),
            pltpu.SemaphoreType.DMA((IDX_RING,)),
            pltpu.SemaphoreType.DMA((2,)),
            pltpu.SemaphoreType.DMA,
        ],
    )
    return pl.pallas_call(
        _experts_body,
        out_shape=jax.ShapeDtypeStruct((a_rows * w, LANES), F32),
        grid_spec=grid_spec,
        compiler_params=_cparams(("arbitrary", "arbitrary")),
        name="experts",
    )(block_expert, n_used, slot_idx, slot_gate, xn, w_gu, w_gu, b_gu, b_gu, w_down, b_down)


def _combine_body(final_norm, h_ref, y0_ref, y1_ref, y2_ref, y3_ref, g_ref, o_ref):
    n, d = h_ref.shape
    acc = h_ref[...] + _load_rows(y0_ref, n, d) + _load_rows(y1_ref, n, d) + _load_rows(y2_ref, n, d) \
        + _load_rows(y3_ref, n, d)
    o_ref[...] = _rms(acc, g_ref[...]) if final_norm else acc


def _combine(h2, y_tok, gain, tm):
    t, d = h2.shape
    w = d // LANES
    nt = t // tm
    final_norm = gain is not None
    if gain is None:
        gain = jnp.ones((d,), F32)
    gain = gain.reshape(1, d)
    plane = lambda k: pl.BlockSpec((tm * w, LANES), lambda i: (k * nt + i, 0))
    return pl.pallas_call(
        functools.partial(_combine_body, final_norm),
        out_shape=jax.ShapeDtypeStruct((t, d), F32),
        grid=(nt,),
        in_specs=[pl.BlockSpec((tm, d), lambda i: (i, 0))] + [plane(k) for k in range(TOP_K)]
                 + [pl.BlockSpec((1, d), lambda i: (0, 0))],
        out_specs=pl.BlockSpec((tm, d), lambda i: (i, 0)),
        compiler_params=_cparams(("arbitrary",)),
        name="combine",
    )(h2, y_tok, y_tok, y_tok, y_tok, gain)


def _routing(top_idx, gates, rows, row_width):
    k, t = top_idx.shape
    a = t * k
    flat_e = top_idx.reshape(-1)
    _, s_asg, s_gate = lax.sort((flat_e, jnp.arange(a, dtype=jnp.int32), gates.reshape(-1)), num_keys=1)
    experts = jnp.arange(N_EXPERTS, dtype=jnp.int32)
    counts = jnp.sum((flat_e[None, :] == experts[:, None]).astype(jnp.int32), axis=1)
    start = jnp.cumsum(counts) - counts
    padded = (counts + rows - 1) // rows * rows
    padded_end = jnp.cumsum(padded)
    padded_start = padded_end - padded
    nb = -(-a // rows) + N_EXPERTS
    n_used = (padded_end[-1] // rows).astype(jnp.int32)
    blk = jnp.arange(nb, dtype=jnp.int32)
    bexp = jnp.minimum(jnp.sum((padded_end[None, :] <= (blk * rows)[:, None]).astype(jnp.int32), axis=1),
                       N_EXPERTS - 1)
    bexp = jnp.where(blk < n_used, bexp, bexp[jnp.maximum(n_used - 1, 0)])
    off = blk * rows - padded_start[bexp]
    first = jnp.clip(start[bexp] + off, 0, a)
    nvalid = jnp.where(blk < n_used, jnp.clip(counts[bexp] - off, 0, rows), 0)
    r = jnp.arange(rows, dtype=jnp.int32)[None, :]
    pos = jnp.minimum(first[:, None] + r, a - 1)
    asg = s_asg[pos]
    gate = s_gate[pos]
    valid = r < nvalid[:, None]
    slot_src = jnp.where(valid, asg % t, 0)
    slot_dst = jnp.where(valid, asg, a + r)
    slot_gate = jnp.where(valid, gate, 0.0)
    slot_idx = jnp.concatenate([slot_src, slot_dst], axis=1) * row_width
    return bexp, n_used.reshape(1), slot_idx, slot_gate.reshape(nb, rows, 1)


def _moe_stage(h2, xn, top_idx, gates, w_gate_up, b_gate_up, w_down, b_down, norm_final):
    t, d = h2.shape
    bexp, n_used, slot_idx, slot_gate = _routing(top_idx, gates, MOE_ROWS, d // LANES)
    e = w_gate_up.shape[0]
    y_tok = _experts(bexp, n_used, slot_idx, slot_gate, xn, w_gate_up.astype(BF16),
                     b_gate_up.reshape(e, 1, -1), w_down.astype(BF16), b_down.reshape(e, 1, -1))
    return _combine(h2, y_tok, norm_final, min(256, t))


def _permute_w_in(w_in):
    sp = np.cumsum([0, ATT_WIDTH, KV_LATENT, IDX_WIDTH, IDX_DIM, IDX_HEADS, RET_QK_WIDTH, RET_QK_WIDTH,
                    RET_WIDTH, RET_WIDTH])
    piece = lambda i: w_in[:, int(sp[i]):int(sp[i + 1])]
    d = w_in.shape[0]
    cols = [piece(0), piece(2), piece(7), piece(8), piece(5), piece(6), piece(1), piece(3), piece(4)]
    used = sum(c.shape[1] for c in cols)
    cols.append(jnp.zeros((d, PROJ_WIDTH - used), w_in.dtype))
    return jnp.concatenate(cols, axis=1).astype(BF16)


def _rotary_tables(positions):
    half = RET_QK_DIM // 2
    inv = ROPE_BASE ** (-jnp.arange(0, RET_QK_DIM, 2, dtype=F32) / RET_QK_DIM)
    ang = positions.astype(F32).reshape(-1, 1) * inv
    cos, sin = jnp.cos(ang), jnp.sin(ang)
    return jnp.concatenate([cos, cos], axis=1), jnp.concatenate([-sin, sin], axis=1)


def _front(x, positions, norm_mix, w_in, kv_norm, idx_k_norm):
    b, s, d = x.shape
    t = b * s
    tm = min(1024, t)
    proj = _proj(x.reshape(t, d), norm_mix.reshape(1, d), _permute_w_in(w_in), tm)
    cos2, sin2 = _rotary_tables(positions)
    g_k = jnp.concatenate([idx_k_norm, jnp.zeros((LANES - IDX_DIM,), F32)]).reshape(1, LANES)
    c, kidx, widx, q_rot, k_rot = _prep(proj, cos2, sin2, kv_norm.reshape(1, KV_LATENT), g_k, tm)
    return dict(proj=proj, c=c, kidx=kidx, widx=widx, q_rot=q_rot, k_rot=k_rot)


def _mid_stage(x, mem, y_att, y_ret, w_out, norm_cross, norm_mem, w_cq, w_ck, w_cv, w_co, norm_ffn,
               w_router, b_router):
    b, s, d = x.shape
    km, vm = _memkv(mem, norm_mem.reshape(1, d), w_ck.astype(BF16), w_cv.astype(BF16))
    w_out_b16 = w_out.astype(BF16)
    return _mid(x.reshape(b * s, d), y_att, y_ret, w_out_b16[:ATT_WIDTH], w_out_b16[ATT_WIDTH:],
                norm_cross.reshape(1, d), w_cq.astype(BF16), km, vm, w_co.astype(BF16),
                norm_ffn.reshape(1, d), w_router.T.astype(BF16), b_router.reshape(N_EXPERTS, 1),
                b, s, min(512, s))


def _ret_stage(st, b, s):
    q_rot = st["q_rot"].reshape(b, s, RET_QK_WIDTH)
    kt_rot = jnp.swapaxes(st["k_rot"].reshape(b, s, RET_QK_WIDTH), 1, 2)
    return _retention(q_rot, kt_rot, st["proj"], b, s)


def kernel(x, mem, positions, norm_mix, w_in, kv_norm, idx_k_norm, w_uk, w_uv, rel_bias, w_out, norm_cross,
           norm_mem, w_cq, w_ck, w_cv, w_co, norm_ffn, w_router, b_router, w_gate_up, b_gate_up, w_down, b_down,
           norm_final):
    b, s, d = x.shape
    depth = norm_mix.shape[0]
    h = x
    for l in range(depth):
        st = _front(h, positions, norm_mix[l], w_in[l], kv_norm[l], idx_k_norm[l])
        y_att = _dsa_stage(st, w_uk[l], w_uv[l], rel_bias, b, s)
        y_ret = _ret_stage(st, b, s)
        h2, xn, top_idx, gates = _mid_stage(h, mem, y_att, y_ret, w_out[l], norm_cross[l], norm_mem[l], w_cq[l],
                                            w_ck[l], w_cv[l], w_co[l], norm_ffn[l], w_router[l], b_router[l])
        gain = norm_final if l == depth - 1 else None
        h = _moe_stage(h2, xn, top_idx, gates, w_gate_up[l], b_gate_up[l], w_down[l], b_down[l], gain).reshape(b, s, d)
    return h
```

```python
import functools
import math

import jax
import jax.numpy as jnp
import numpy as np
from jax import lax
from jax.experimental import pallas as pl
from jax.experimental.pallas import tpu as pltpu

F32 = jnp.float32
BF16 = jnp.bfloat16

D_MODEL = 2048
N_ATT_HEADS = 8
ATT_HEAD_DIM = 128
KV_LATENT = 256
IDX_HEADS = 16
IDX_DIM = 64
IDX_TOPK_MAX = 256
N_RET_HEADS = 4
RET_QK_DIM = 128
RET_V_DIM = 256
ROPE_BASE = 10000.0
N_BUCKETS = 32
MAX_DISTANCE = 128
N_CROSS_HEADS = 4
CROSS_HEAD_DIM = 128
N_EXPERTS = 32
TOP_K = 4
D_FF = D_MODEL
SWIGLU_LIMIT = 7.0
SWIGLU_ALPHA = 1.702
EPS = 1e-6

ATT_WIDTH = N_ATT_HEADS * ATT_HEAD_DIM
RET_WIDTH = N_RET_HEADS * RET_V_DIM
RET_QK_WIDTH = N_RET_HEADS * RET_QK_DIM
IDX_WIDTH = IDX_HEADS * IDX_DIM
CROSS_WIDTH = N_CROSS_HEADS * CROSS_HEAD_DIM

LANES = 128
VMEM_LIMIT = 56 * 1024 * 1024

COL_Q_ATT = 0
COL_Q_IDX = COL_Q_ATT + ATT_WIDTH
COL_V_RET = COL_Q_IDX + IDX_WIDTH
COL_G_RET = COL_V_RET + RET_WIDTH
COL_Q_RET = COL_G_RET + RET_WIDTH
COL_K_RET = COL_Q_RET + RET_QK_WIDTH
COL_C_KV = COL_K_RET + RET_QK_WIDTH
COL_KW = COL_C_KV + KV_LATENT
PROJ_TN = 512
PROJ_WIDTH = 5632

Q_BLOCK = 128
KEY_TILE = 512
KEY_PAD = KEY_TILE - Q_BLOCK
NEG_BIG = -1e30
LOG2E = math.log2(math.e)

RET_CHUNK = 256

MOE_ROWS = 512
IDX_RING = 3
MOE_TF = 1024


def _cparams(sem, vmem=VMEM_LIMIT):
    return pltpu.CompilerParams(dimension_semantics=sem, vmem_limit_bytes=vmem)


def _rms(x, g):
    return x * lax.rsqrt(jnp.mean(x * x, axis=-1, keepdims=True) + EPS) * g


ROW_PITCH = 20


def _store_rows(ref, x, pitch=None):
    n, d = x.shape
    w = d // LANES
    for a in range(w):
        ref[pl.ds(a, n, stride=pitch or w), :] = x[:, a * LANES:(a + 1) * LANES].astype(ref.dtype)


def _load_rows(ref, n, d, pitch=None):
    w = d // LANES
    return jnp.concatenate([ref[pl.ds(a, n, stride=pitch or w), :] for a in range(w)], axis=1)


def _proj_body(x_ref, g_ref, w_ref, o_ref, xn_ref):
    @pl.when(pl.program_id(1) == 0)
    def _():
        xn_ref[...] = _rms(x_ref[...], g_ref[...]).astype(BF16)

    o_ref[...] = jnp.dot(xn_ref[...], w_ref[...], preferred_element_type=F32).astype(o_ref.dtype)


def _proj(x2, gain, w_p, tm):
    t, d = x2.shape
    n = w_p.shape[1]
    return pl.pallas_call(
        _proj_body,
        out_shape=jax.ShapeDtypeStruct((t, n), BF16),
        grid=(t // tm, n // PROJ_TN),
        in_specs=[
            pl.BlockSpec((tm, d), lambda i, j: (i, 0)),
            pl.BlockSpec((1, d), lambda i, j: (0, 0)),
            pl.BlockSpec((d, PROJ_TN), lambda i, j: (0, j)),
        ],
        out_specs=pl.BlockSpec((tm, PROJ_TN), lambda i, j: (i, j)),
        scratch_shapes=[pltpu.VMEM((tm, d), BF16)],
        compiler_params=_cparams(("arbitrary", "arbitrary")),
        name="proj",
    )(x2, gain, w_p)


def _prep_body(c_ref, kw_ref, q_ref, k_ref, cos_ref, sin_ref, gkv_ref, gk_ref,
               c_out, kidx_out, widx_out, q_out, k_out):
    c_out[...] = _rms(c_ref[...].astype(F32), gkv_ref[...]).astype(BF16)

    kw = kw_ref[...].astype(F32)
    lane = lax.broadcasted_iota(jnp.int32, kw.shape, 1)
    is_k = lane < IDX_DIM
    kk = jnp.where(is_k, kw, 0.0)
    ms = jnp.sum(kk * kk, axis=-1, keepdims=True) * (1.0 / IDX_DIM)
    kidx_out[...] = (kk * lax.rsqrt(ms + EPS) * gk_ref[...]).astype(BF16)
    widx_out[...] = kw * (IDX_HEADS ** -0.5 * IDX_DIM ** -0.5)

    cos2 = cos_ref[...]
    sin2 = sin_ref[...]
    for h in range(N_RET_HEADS):
        sl = slice(h * RET_QK_DIM, (h + 1) * RET_QK_DIM)
        qh = q_ref[:, sl].astype(F32)
        kh = k_ref[:, sl].astype(F32)
        q_out[:, sl] = (qh * cos2 + pltpu.roll(qh, RET_QK_DIM // 2, 1) * sin2).astype(BF16)
        k_out[:, sl] = ((kh * cos2 + pltpu.roll(kh, RET_QK_DIM // 2, 1) * sin2)
                        * (RET_QK_DIM ** -0.5)).astype(BF16)


def _prep(proj, cos2, sin2, g_kv, g_k, tm):
    t = proj.shape[0]
    row = lambda i: (i, 0)
    return pl.pallas_call(
        _prep_body,
        out_shape=(
            jax.ShapeDtypeStruct((t, KV_LATENT), BF16),
            jax.ShapeDtypeStruct((t, LANES), BF16),
            jax.ShapeDtypeStruct((t, LANES), F32),
            jax.ShapeDtypeStruct((t, RET_QK_WIDTH), BF16),
            jax.ShapeDtypeStruct((t, RET_QK_WIDTH), BF16),
        ),
        grid=(t // tm,),
        in_specs=[
            pl.BlockSpec((tm, KV_LATENT), lambda i: (i, COL_C_KV // KV_LATENT)),
            pl.BlockSpec((tm, LANES), lambda i: (i, COL_KW // LANES)),
            pl.BlockSpec((tm, RET_QK_WIDTH), lambda i: (i, COL_Q_RET // RET_QK_WIDTH)),
            pl.BlockSpec((tm, RET_QK_WIDTH), lambda i: (i, COL_K_RET // RET_QK_WIDTH)),
            pl.BlockSpec((tm, LANES), row),
            pl.BlockSpec((tm, LANES), row),
            pl.BlockSpec((1, KV_LATENT), lambda i: (0, 0)),
            pl.BlockSpec((1, LANES), lambda i: (0, 0)),
        ],
        out_specs=(
            pl.BlockSpec((tm, KV_LATENT), row),
            pl.BlockSpec((tm, LANES), row),
            pl.BlockSpec((tm, LANES), row),
            pl.BlockSpec((tm, RET_QK_WIDTH), row),
            pl.BlockSpec((tm, RET_QK_WIDTH), row),
        ),
        compiler_params=_cparams(("arbitrary",)),
        name="prep",
    )(proj, proj, proj, proj, cos2, sin2, g_kv, g_k)


def _ret_body(q_ref, kt_ref, v_ref, g_ref, dec_ref, cross_ref, state_ref, cd_ref, o_ref, r_ref):
    @pl.when(pl.program_id(1) == 0)
    def _():
        r_ref[...] = jnp.zeros_like(r_ref)

    for h in range(N_RET_HEADS):
        qs = slice(h * RET_QK_DIM, (h + 1) * RET_QK_DIM)
        vs = slice(h * RET_V_DIM, (h + 1) * RET_V_DIM)
        q = q_ref[0, :, qs]
        kt = kt_ref[0, qs, :]
        v = v_ref[:, vs]
        r_old = r_ref[h]
        inner = jnp.dot(q, kt, preferred_element_type=F32) * dec_ref[h]
        qc = (q.astype(F32) * cross_ref[h]).astype(BF16)
        o = (jnp.dot(inner.astype(BF16), v, preferred_element_type=F32)
             + jnp.dot(qc, r_old.astype(BF16), preferred_element_type=F32))
        ks = (kt.astype(F32) * state_ref[h]).astype(BF16)
        r_ref[h] = r_old * cd_ref[h] + jnp.dot(ks, v, preferred_element_type=F32)
        o = o * lax.rsqrt(jnp.mean(o * o, axis=-1, keepdims=True) + EPS)
        g = g_ref[:, vs].astype(F32)
        o_ref[:, vs] = (g * jax.nn.sigmoid(g) * o).astype(o_ref.dtype)


def _retention(q_rot, kt_rot, proj, b, s):
    c = RET_CHUNK
    n = s // c
    hh = N_RET_HEADS
    log_gamma = jnp.log(1.0 - 2.0 ** (-5.0 - jnp.arange(hh, dtype=F32)))
    j = jnp.arange(c, dtype=F32)
    diff = j[:, None] - j[None, :]
    dec = jnp.where(diff >= 0, jnp.exp(log_gamma[:, None, None] * jnp.maximum(diff, 0.0)), 0.0)
    cross = jnp.broadcast_to(jnp.exp(log_gamma[:, None] * (j + 1.0))[:, :, None], (hh, c, RET_QK_DIM))
    state = jnp.exp(log_gamma[:, None] * (c - 1.0 - j))[:, None, :]
    cdec = jnp.broadcast_to(jnp.exp(log_gamma * c)[:, None, None], (hh, 1, RET_V_DIM))
    const = lambda bi, ci: (0, 0, 0)
    return pl.pallas_call(
        _ret_body,
        out_shape=jax.ShapeDtypeStruct((b * s, RET_WIDTH), BF16),
        grid=(b, n),
        in_specs=[
            pl.BlockSpec((1, c, RET_QK_WIDTH), lambda bi, ci: (bi, ci, 0)),
            pl.BlockSpec((1, RET_QK_WIDTH, c), lambda bi, ci: (bi, 0, ci)),
            pl.BlockSpec((c, RET_WIDTH), lambda bi, ci: (bi * n + ci, COL_V_RET // RET_WIDTH)),
            pl.BlockSpec((c, RET_WIDTH), lambda bi, ci: (bi * n + ci, COL_G_RET // RET_WIDTH)),
            pl.BlockSpec((hh, c, c), const),
            pl.BlockSpec((hh, c, RET_QK_DIM), const),
            pl.BlockSpec((hh, 1, c), const),
            pl.BlockSpec((hh, 1, RET_V_DIM), const),
        ],
        out_specs=pl.BlockSpec((c, RET_WIDTH), lambda bi, ci: (bi * n + ci, 0)),
        scratch_shapes=[pltpu.VMEM((hh, RET_QK_DIM, RET_V_DIM), F32)],
        compiler_params=_cparams(("arbitrary", "arbitrary")),
        name="ret",
    )(q_rot, kt_rot, proj, proj, dec, cross, state, cdec)


def _dsa_body(topk, kidx_ref, qit_ref, w_ref, c_ref, ct_ref, qt_ref, wuk_ref, wuvt_ref, bias_ref, y_ref,
              sc_ref, qlt_ref, acc_ref, m_ref, l_ref):
    qb = pl.program_id(1)
    tk = KEY_TILE
    half = tk // 2
    e_pad = (qb + 1) * Q_BLOCK + KEY_PAD
    n_tiles = e_pad // tk
    nh = N_ATT_HEADS
    lane_q = qb * Q_BLOCK + lax.broadcasted_iota(jnp.int32, (1, Q_BLOCK), 1)

    def tile_start(j):
        return pl.multiple_of(e_pad - (j + 1) * tk, Q_BLOCK)

    for h in range(nh):
        sl = slice(h * Q_BLOCK, (h + 1) * Q_BLOCK)
        hd = slice(h * ATT_HEAD_DIM, (h + 1) * ATT_HEAD_DIM)
        qlt_ref[:, sl] = (jnp.dot(wuk_ref[h], qt_ref[0, 0, hd, :], preferred_element_type=F32)
                          * (ATT_HEAD_DIM ** -0.5 * LOG2E)).astype(BF16)

    def score_tile(j, carry):
        r0 = tile_start(j)
        for part in range(2):
            rs = pl.multiple_of(r0 + part * half, Q_BLOCK)
            z = jnp.dot(kidx_ref[0, pl.ds(rs, half), :], qit_ref[0, 0], preferred_element_type=F32)
            tot = jnp.zeros((half, Q_BLOCK), F32)
            for h in range(IDX_HEADS):
                sl = slice(h * Q_BLOCK, (h + 1) * Q_BLOCK)
                tot = tot + jnp.maximum(z[:, sl], 0.0) * w_ref[0, 0, :, sl]
            key = rs - KEY_PAD + lax.broadcasted_iota(jnp.int32, (half, Q_BLOCK), 0)
            ok = (key >= 0) & (key <= lane_q)
            sc_ref[pl.ds(rs, half), :] = jnp.where(ok, tot, NEG_BIG)
        return carry
    lax.fori_loop(0, n_tiles, score_tile, 0)

    def minmax_tile(j, carry):
        lo, hi = carry
        s = sc_ref[pl.ds(tile_start(j), tk), :]
        hi = jnp.maximum(hi, jnp.max(s, axis=0, keepdims=True))
        lo = jnp.minimum(lo, jnp.min(jnp.where(s > 0.5 * NEG_BIG, s, -NEG_BIG), axis=0, keepdims=True))
        return lo, hi
    lo0, hi0 = lax.fori_loop(0, n_tiles, minmax_tile,
                             (jnp.full((1, Q_BLOCK), -NEG_BIG, F32), jnp.full((1, Q_BLOCK), NEG_BIG, F32)))
    search = lane_q + 1 > topk

    n_chain = 4
    rows_chain = tk // n_chain

    def count_ge(x):
        def body(j, cnts):
            r0 = tile_start(j)
            out = []
            for k in range(n_chain):
                s = sc_ref[pl.ds(pl.multiple_of(r0 + k * rows_chain, Q_BLOCK), rows_chain), :]
                ind = jnp.where(s >= x, 1.0, 0.0).reshape(rows_chain // 8, 8, Q_BLOCK)
                out.append(cnts[k] + jnp.sum(ind, axis=0))
            return tuple(out)
        cnts = lax.fori_loop(0, n_tiles, body, tuple(jnp.zeros((8, Q_BLOCK), F32) for _ in range(n_chain)))
        return jnp.sum((cnts[0] + cnts[1]) + (cnts[2] + cnts[3]), axis=0, keepdims=True)

    steps_per_check = 3

    def bis_cond(st):
        it, _, _, _, active = st
        return (it < 42) & (active > 0.0)

    def bis_step(_, st):
        lo, hi, done = st
        mid = lo + 0.5 * (hi - lo)
        cnt = count_ge(mid)
        ge = cnt >= topk
        open_ = done < 0.5
        lo = jnp.where(open_ & ge, mid, lo)
        hi = jnp.where(open_ & (~ge), mid, hi)
        done = jnp.where(cnt == topk, 1.0, done)
        return lo, hi, done

    def bis_body(st):
        it, lo, hi, done, _ = st
        lo, hi, done = lax.fori_loop(0, steps_per_check, bis_step, (lo, hi, done))
        return it + steps_per_check, lo, hi, done, jnp.max(1.0 - done)

    done0 = jnp.where(search, 0.0, 1.0)
    _, lo_f, _, _, _ = lax.while_loop(bis_cond, bis_body, (jnp.int32(0), lo0, hi0, done0, jnp.max(1.0 - done0)))
    thr = jnp.where(search, lo_f, 0.5 * NEG_BIG)

    m_ref[...] = jnp.full(m_ref.shape, NEG_BIG, F32)
    l_ref[...] = jnp.zeros(l_ref.shape, F32)
    acc_ref[...] = jnp.zeros(acc_ref.shape, F32)

    n_grp = 4
    hg = nh // n_grp

    def attend(segments):
        scores, ct_tiles = [], []
        for rs, n, with_bias in segments:
            c_tile = c_ref[0, pl.ds(rs, n), :]
            ct_tiles.append(ct_ref[0, :, pl.ds(rs, n)])
            pen = jnp.where(sc_ref[pl.ds(rs, n), :] >= thr, 0.0, NEG_BIG)
            pen = jnp.concatenate([pen] * hg, axis=1)
            for g in range(n_grp):
                ls = slice(g * hg * Q_BLOCK, (g + 1) * hg * Q_BLOCK)
                st = jnp.dot(c_tile, qlt_ref[:, ls], preferred_element_type=F32) + pen
                if with_bias:
                    st = st + bias_ref[:, ls]
                scores.append(st)
        for k, ct_tile in enumerate(ct_tiles):
            for g in range(n_grp):
                ls = slice(g * hg * Q_BLOCK, (g + 1) * hg * Q_BLOCK)
                st = scores[k * n_grp + g]
                m_old = m_ref[:, ls]
                m_new = jnp.maximum(m_old, jnp.max(st, axis=0, keepdims=True))
                alpha = jnp.exp2(m_old - m_new)
                p = jnp.exp2(st - m_new)
                l_ref[:, ls] = alpha * l_ref[:, ls] + jnp.sum(p, axis=0, keepdims=True)
                acc_ref[:, ls] = acc_ref[:, ls] * alpha + jnp.dot(ct_tile, p.astype(BF16),
                                                                  preferred_element_type=F32)
                m_ref[:, ls] = m_new

    r_first = tile_start(0)
    attend([(pl.multiple_of(r_first + half, Q_BLOCK), half, True), (r_first, half, False)])

    def attend_pair(i, carry):
        attend([(tile_start(2 * i + 1), tk, False), (tile_start(2 * i + 2), tk, False)])
        return carry
    n_rest = n_tiles - 1
    lax.fori_loop(0, n_rest // 2, attend_pair, 0)

    @pl.when(n_rest % 2 == 1)
    def _():
        attend([(tile_start(n_tiles - 1), tk, False)])

    ot = acc_ref[...] / l_ref[...]
    for h in range(nh):
        sl = slice(h * Q_BLOCK, (h + 1) * Q_BLOCK)
        y_ref[0, h * ATT_HEAD_DIM:(h + 1) * ATT_HEAD_DIM, :] = jnp.dot(
            wuvt_ref[h], ot[:, sl].astype(BF16), preferred_element_type=F32).astype(y_ref.dtype)


def _dsa(kidx_p, qit, widx_t, c_p, ct_p, qt, wuk_h, wuvt_h, bias_t, b, s, topk):
    nqb = s // Q_BLOCK
    sp = s + KEY_PAD
    nh = N_ATT_HEADS
    per_b = lambda bi, qi: (bi, 0, 0)
    per_q = lambda bi, qi: (bi, qi, 0, 0)
    c3 = lambda bi, qi: (0, 0, 0)
    return pl.pallas_call(
        functools.partial(_dsa_body, topk),
        out_shape=jax.ShapeDtypeStruct((b, ATT_WIDTH, s), BF16),
        grid=(b, nqb),
        in_specs=[
            pl.BlockSpec((1, sp, LANES), per_b),
            pl.BlockSpec((1, 1, LANES, IDX_HEADS * Q_BLOCK), per_q),
            pl.BlockSpec((1, 1, 1, IDX_HEADS * Q_BLOCK), per_q),
            pl.BlockSpec((1, sp, KV_LATENT), per_b),
            pl.BlockSpec((1, KV_LATENT, sp), per_b),
            pl.BlockSpec((1, 1, ATT_WIDTH, Q_BLOCK), per_q),
            pl.BlockSpec((nh, KV_LATENT, ATT_HEAD_DIM), c3),
            pl.BlockSpec((nh, ATT_HEAD_DIM, KV_LATENT), c3),
            pl.BlockSpec((2 * Q_BLOCK, nh * Q_BLOCK), lambda bi, qi: (0, 0)),
        ],
        out_specs=pl.BlockSpec((1, ATT_WIDTH, Q_BLOCK), lambda bi, qi: (bi, 0, qi)),
        scratch_shapes=[
            pltpu.VMEM((sp, Q_BLOCK), F32),
            pltpu.VMEM((KV_LATENT, nh * Q_BLOCK), BF16),
            pltpu.VMEM((KV_LATENT, nh * Q_BLOCK), F32),
            pltpu.VMEM((1, nh * Q_BLOCK), F32),
            pltpu.VMEM((1, nh * Q_BLOCK), F32),
        ],
        compiler_params=_cparams(("arbitrary", "arbitrary")),
        name="dsa",
    )(kidx_p, qit, widx_t, c_p, ct_p, qt, wuk_h, wuvt_h, bias_t)


def _t5_bucket(dist):
    n = jnp.maximum(dist, 0)
    max_exact = N_BUCKETS // 2
    nf = jnp.maximum(n, 1).astype(F32)
    large = max_exact + (jnp.log(nf / max_exact) / math.log(MAX_DISTANCE / max_exact)
                         * (N_BUCKETS - max_exact)).astype(jnp.int32)
    return jnp.where(n < max_exact, n, jnp.minimum(large, N_BUCKETS - 1))


def _dsa_stage(st, w_uk, w_uv, rel_bias, b, s):
    nqb = s // Q_BLOCK
    nh = N_ATT_HEADS
    proj = st["proj"].reshape(b, nqb, Q_BLOCK, PROJ_WIDTH)
    qt = jnp.swapaxes(proj[..., COL_Q_ATT:COL_Q_ATT + ATT_WIDTH], 2, 3)
    qi = proj[..., COL_Q_IDX:COL_Q_IDX + IDX_WIDTH].reshape(b, nqb, Q_BLOCK, IDX_HEADS, IDX_DIM)
    qit = jnp.transpose(qi, (0, 1, 4, 3, 2)).reshape(b, nqb, IDX_DIM, IDX_HEADS * Q_BLOCK)
    qit = jnp.pad(qit, ((0, 0), (0, 0), (0, LANES - IDX_DIM), (0, 0)))
    wi = st["widx"].reshape(b, nqb, Q_BLOCK, LANES)[..., IDX_DIM:IDX_DIM + IDX_HEADS]
    widx_t = jnp.swapaxes(wi, 2, 3).reshape(b, nqb, 1, IDX_HEADS * Q_BLOCK)
    front = ((0, 0), (KEY_PAD, 0), (0, 0))
    kidx_p = jnp.pad(st["kidx"].reshape(b, s, LANES), front)
    c_p = jnp.pad(st["c"].reshape(b, s, KV_LATENT), front)
    ct_p = jnp.swapaxes(c_p, 1, 2)
    wuk_h = jnp.transpose(w_uk, (1, 0, 2)).astype(BF16)
    wuvt_h = jnp.transpose(w_uv, (1, 2, 0)).astype(BF16)
    i = jnp.arange(2 * Q_BLOCK)[:, None]
    j = jnp.arange(Q_BLOCK)[None, :]
    rb = (rel_bias - rel_bias[N_BUCKETS - 1]) * LOG2E
    bias = rb[_t5_bucket(j - i + Q_BLOCK)]
    bias_t = jnp.transpose(bias, (0, 2, 1)).reshape(2 * Q_BLOCK, nh * Q_BLOCK)
    topk = min(IDX_TOPK_MAX, s // 4)
    yt = _dsa(kidx_p, qit, widx_t, c_p, ct_p, qt, wuk_h, wuvt_h, bias_t, b, s, topk)
    return jnp.swapaxes(yt, 1, 2).reshape(b * s, ATT_WIDTH)


def _memkv_body(m_ref, g_ref, wk_ref, wv_ref, k_ref, v_ref):
    mn = _rms(m_ref[0], g_ref[...]).astype(BF16)
    k_ref[0] = jnp.dot(mn, wk_ref[...], preferred_element_type=F32).astype(BF16)
    v_ref[0] = jnp.dot(mn, wv_ref[...], preferred_element_type=F32).astype(BF16)


def _memkv(mem, gain, w_ck, w_cv):
    b, m, d = mem.shape
    w = w_ck.shape[1]
    const = lambda i: (0, 0)
    out = jax.ShapeDtypeStruct((b, m, w), BF16)
    return pl.pallas_call(
        _memkv_body,
        out_shape=(out, out),
        grid=(b,),
        in_specs=[
            pl.BlockSpec((1, m, d), lambda i: (i, 0, 0)),
            pl.BlockSpec((1, d), const),
            pl.BlockSpec((d, w), const),
            pl.BlockSpec((d, w), const),
        ],
        out_specs=(pl.BlockSpec((1, m, w), lambda i: (i, 0, 0)), pl.BlockSpec((1, m, w), lambda i: (i, 0, 0))),
        compiler_params=_cparams(("arbitrary",)),
        name="memkv",
    )(mem, gain, w_ck, w_cv)


def _mid_body(x_ref, ya_ref, yr_ref, woa_ref, wob_ref, gc_ref, wq_ref, km_ref, vm_ref, wo_ref,
              gf_ref, wr_ref, br_ref, h2_ref, xn_ref, idx_ref, gate_ref):
    h1 = (x_ref[...]
          + jnp.dot(ya_ref[...], woa_ref[...], preferred_element_type=F32)
          + jnp.dot(yr_ref[...], wob_ref[...], preferred_element_type=F32))
    hn = _rms(h1, gc_ref[...]).astype(BF16)
    q = jnp.dot(hn, wq_ref[...], preferred_element_type=F32).astype(BF16)
    heads = []
    for h in range(N_CROSS_HEADS):
        sl = slice(h * CROSS_HEAD_DIM, (h + 1) * CROSS_HEAD_DIM)
        s = lax.dot_general(q[:, sl], km_ref[0, :, sl], (((1,), (1,)), ((), ())),
                            preferred_element_type=F32) * (CROSS_HEAD_DIM ** -0.5)
        p = jnp.exp(s - jnp.max(s, axis=-1, keepdims=True))
        p = p / jnp.sum(p, axis=-1, keepdims=True)
        heads.append(jnp.dot(p.astype(BF16), vm_ref[0, :, sl], preferred_element_type=F32).astype(BF16))
    o = jnp.concatenate(heads, axis=-1)
    h2 = h1 + jnp.dot(o, wo_ref[...], preferred_element_type=F32)
    h2_ref[...] = h2

    xn = _rms(h2, gf_ref[...])
    _store_rows(xn_ref, xn)
    logit = lax.dot_general(wr_ref[...], xn.astype(BF16), (((1,), (1,)), ((), ())),
                            preferred_element_type=F32) + br_ref[...]
    eid = lax.broadcasted_iota(jnp.int32, logit.shape, 0)
    vals, ids = [], []
    for _ in range(TOP_K):
        mx = jnp.max(logit, axis=0, keepdims=True)
        sel = jnp.min(jnp.where(logit == mx, eid, N_EXPERTS), axis=0, keepdims=True)
        vals.append(mx)
        ids.append(sel)
        logit = jnp.where(eid == sel, -jnp.inf, logit)
    top = jnp.concatenate(vals, axis=0)
    e = jnp.exp(top - vals[0])
    gate_ref[...] = e / jnp.sum(e, axis=0, keepdims=True)
    idx_ref[...] = jnp.concatenate(ids, axis=0)


def _mid(x2, ya, yr, w_out_a, w_out_b, g_cross, w_cq, km, vm, w_co, g_ffn, w_rt, b_r, b, s, tm):
    t, d = x2.shape
    nt = s // tm
    row = lambda bi, i: (bi * nt + i, 0)
    const = lambda bi, i: (0, 0)
    once = pl.Buffered(1)
    m = km.shape[1]
    return pl.pallas_call(
        _mid_body,
        out_shape=(
            jax.ShapeDtypeStruct((t, d), F32),
            jax.ShapeDtypeStruct((t * (d // LANES), LANES), F32),
            jax.ShapeDtypeStruct((TOP_K, t), jnp.int32),
            jax.ShapeDtypeStruct((TOP_K, t), F32),
        ),
        grid=(b, nt),
        in_specs=[
            pl.BlockSpec((tm, d), row),
            pl.BlockSpec((tm, ATT_WIDTH), row),
            pl.BlockSpec((tm, RET_WIDTH), row),
            pl.BlockSpec((ATT_WIDTH, d), const, pipeline_mode=once),
            pl.BlockSpec((RET_WIDTH, d), const, pipeline_mode=once),
            pl.BlockSpec((1, d), const),
            pl.BlockSpec((d, CROSS_WIDTH), const, pipeline_mode=once),
            pl.BlockSpec((1, m, CROSS_WIDTH), lambda bi, i: (bi, 0, 0)),
            pl.BlockSpec((1, m, CROSS_WIDTH), lambda bi, i: (bi, 0, 0)),
            pl.BlockSpec((CROSS_WIDTH, d), const, pipeline_mode=once),
            pl.BlockSpec((1, d), const),
            pl.BlockSpec((N_EXPERTS, d), const),
            pl.BlockSpec((N_EXPERTS, 1), const),
        ],
        out_specs=(
            pl.BlockSpec((tm, d), row),
            pl.BlockSpec((tm * (d // LANES), LANES), row),
            pl.BlockSpec((TOP_K, tm), lambda bi, i: (0, bi * nt + i)),
            pl.BlockSpec((TOP_K, tm), lambda bi, i: (0, bi * nt + i)),
        ),
        compiler_params=_cparams(("arbitrary", "arbitrary")),
        name="mid",
    )(x2, ya, yr, w_out_a, w_out_b, g_cross, w_cq, km, vm, w_co, g_ffn, w_rt, b_r)


FP8 = jnp.float8_e4m3fn
FP8_MAX = 448.0
TINY = 1e-30


def _quant_body(w_ref, q_ref, s_ref):
    w = w_ref[0]
    amax = jnp.maximum(jnp.max(jnp.abs(w), axis=0, keepdims=True), TINY)
    q_ref[0] = (w * (FP8_MAX / amax)).astype(FP8)
    s_ref[0] = amax * (1.0 / FP8_MAX)


def _quant_weights(w, tn):
    e, k, n = w.shape
    return pl.pallas_call(
        _quant_body,
        out_shape=(jax.ShapeDtypeStruct((e, k, n), FP8), jax.ShapeDtypeStruct((e, 1, n), F32)),
        grid=(e, n // tn),
        in_specs=[pl.BlockSpec((1, k, tn), lambda i, j: (i, 0, j))],
        out_specs=(pl.BlockSpec((1, k, tn), lambda i, j: (i, 0, j)), pl.BlockSpec((1, 1, tn), lambda i, j: (i, 0, j))),
        compiler_params=_cparams(("arbitrary", "arbitrary")),
        name="quant",
    )(w)


def _quant_rows(x):
    amax = jnp.maximum(jnp.max(jnp.abs(x), axis=1, keepdims=True), TINY)
    return (x * (FP8_MAX / amax)).astype(FP8), amax * (1.0 / FP8_MAX)


def _experts_body(be_ref, nu_ref, idx_hbm, gate_ref, xn_hbm, wg_ref, wu_ref, sg_ref, su_ref, bg_ref, bu_ref,
                  wd_ref, sd_ref, bd_ref, y_hbm, idx_smem, xbuf, xb_ref, sx_ref, acc_ref, ybuf,
                  sem_idx, sem_g, sem_s):
    i = pl.program_id(0)
    j = pl.program_id(1)
    n_used = nu_ref[0]
    rows, d = xb_ref.shape
    w = d // LANES

    def idx_copy(blk, slot):
        return pltpu.make_async_copy(idx_hbm.at[blk],
                                     idx_smem.at[pl.ds(pl.multiple_of(slot * 2 * rows, 2 * rows), 2 * rows)],
                                     sem_idx.at[slot])

    def gather_start(islot, xslot):
        def body(r, carry):
            src = pl.multiple_of(idx_smem[islot * 2 * rows + r], w)
            pltpu.make_async_copy(xn_hbm.at[pl.ds(src, w)], xbuf.at[xslot, pl.ds(r * ROW_PITCH, w)],
                                  sem_g.at[xslot]).start()
            return carry
        lax.fori_loop(0, rows, body, 0, unroll=8)

    def gather_wait(xslot):
        pltpu.make_async_copy(xn_hbm.at[pl.ds(0, rows * w)], xbuf.at[xslot, pl.ds(0, rows * w)],
                              sem_g.at[xslot]).wait()

    def scatter_wait():
        pltpu.make_async_copy(ybuf.at[pl.ds(0, rows * w)], y_hbm.at[pl.ds(0, rows * w)], sem_s).wait()

    @pl.when(j == 0)
    def _():
        @pl.when(i == 0)
        def _():
            ybuf[...] = jnp.zeros_like(ybuf)
            dump = pltpu.make_async_copy(ybuf.at[pl.ds(0, rows * w)],
                                         y_hbm.at[pl.ds(y_hbm.shape[0] - rows * w, rows * w)], sem_s)
            dump.start()
            dump.wait()
            idx_copy(0, 0).start()

            @pl.when(n_used > 1)
            def _():
                idx_copy(1, 1).start()

            idx_copy(0, 0).wait()
            gather_start(0, 0)

        @pl.when(i + 1 < n_used)
        def _():
            idx_copy(i + 1, (i + 1) % IDX_RING).wait()

        @pl.when(i + 2 < n_used)
        def _():
            idx_copy(i + 2, (i + 2) % IDX_RING).start()

        @pl.when(i < n_used)
        def _():
            gather_wait(i % 2)
            xb_ref[...], sx_ref[...] = _quant_rows(_load_rows(xbuf.at[i % 2], rows, d, ROW_PITCH))

    @pl.when(i < n_used)
    def _():
        nxt = jnp.minimum(i + 1, n_used - 1)
        per_step = rows // (D_FF // wd_ref.shape[1])
        gbase = (nxt % IDX_RING) * 2 * rows + j * per_step
        xnext = xbuf.at[(i + 1) % 2]
        for k in range(per_step):
            src = pl.multiple_of(idx_smem[gbase + k], w)
            pltpu.make_async_copy(xn_hbm.at[pl.ds(src, w)], xnext.at[pl.ds((j * per_step + k) * ROW_PITCH, w)],
                                  sem_g.at[(i + 1) % 2]).start()

        xb = xb_ref[...]
        sx = sx_ref[...]
        g = jnp.dot(xb, wg_ref[0], preferred_element_type=F32) * sx * sg_ref[0] + bg_ref[0]
        u = jnp.dot(xb, wu_ref[0], preferred_element_type=F32) * sx * su_ref[0] + bu_ref[0]
        gt = jnp.minimum(g, SWIGLU_LIMIT)
        up = jnp.clip(u, -SWIGLU_LIMIT, SWIGLU_LIMIT)
        hq, sh = _quant_rows((up + 1.0) * (gt * jax.nn.sigmoid(SWIGLU_ALPHA * gt)))
        part = jnp.dot(hq, wd_ref[0], preferred_element_type=F32) * sh * sd_ref[0]

        @pl.when(j == 0)
        def _():
            acc_ref[...] = part

        @pl.when(j > 0)
        def _():
            acc_ref[...] += part

        @pl.when(j == pl.num_programs(1) - 1)
        def _():
            y = (acc_ref[...] + bd_ref[0]) * gate_ref[0]

            @pl.when(i > 0)
            def _():
                scatter_wait()

            _store_rows(ybuf, y, ROW_PITCH)
            islot = i % IDX_RING

            def body(r, carry):
                dst = pl.multiple_of(idx_smem[islot * 2 * rows + rows + r], w)
                pltpu.make_async_copy(ybuf.at[pl.ds(r * ROW_PITCH, w)], y_hbm.at[pl.ds(dst, w)], sem_s).start()
                return carry
            lax.fori_loop(0, rows, body, 0, unroll=8)

            @pl.when(i == n_used - 1)
            def _():
                scatter_wait()
                gather_wait((i + 1) % 2)


def _experts(block_expert, n_used, slot_idx, slot_gate, xn, w_gu, s_gu, b_gu, w_down, s_down, b_down):
    nb = slot_idx.shape[0]
    rows = slot_idx.shape[1] // 2
    e, d, ff2 = w_gu.shape
    w = d // LANES
    t = xn.shape[0] // w
    ff = ff2 // 2
    tf = MOE_TF
    nj = ff // tf
    a_rows = t * TOP_K + rows
    grid_spec = pltpu.PrefetchScalarGridSpec(
        num_scalar_prefetch=2,
        grid=(nb, nj),
        in_specs=[
            pl.BlockSpec(memory_space=pl.ANY),
            pl.BlockSpec((1, rows, 1), lambda i, j, be, nu: (i, 0, 0)),
            pl.BlockSpec(memory_space=pl.ANY),
            pl.BlockSpec((1, d, tf), lambda i, j, be, nu: (be[i], 0, j)),
            pl.BlockSpec((1, d, tf), lambda i, j, be, nu: (be[i], 0, j + nj)),
            pl.BlockSpec((1, 1, tf), lambda i, j, be, nu: (be[i], 0, j)),
            pl.BlockSpec((1, 1, tf), lambda i, j, be, nu: (be[i], 0, j + nj)),
            pl.BlockSpec((1, 1, tf), lambda i, j, be, nu: (be[i], 0, j)),
            pl.BlockSpec((1, 1, tf), lambda i, j, be, nu: (be[i], 0, j + nj)),
            pl.BlockSpec((1, tf, d), lambda i, j, be, nu: (be[i], j, 0)),
            pl.BlockSpec((1, 1, d), lambda i, j, be, nu: (be[i], 0, 0)),
            pl.BlockSpec((1, 1, d), lambda i, j, be, nu: (be[i], 0, 0)),
        ],
        out_specs=pl.BlockSpec(memory_space=pl.ANY),
        scratch_shapes=[
            pltpu.SMEM((IDX_RING * 2 * rows,), jnp.int32),
            pltpu.VMEM((2, rows * ROW_PITCH, LANES), F32),
            pltpu.VMEM((rows, d), FP8),
            pltpu.VMEM((rows, 1), F32),
            pltpu.VMEM((rows, d), F32),
            pltpu.VMEM((rows * ROW_PITCH, LANES), F32),
            pltpu.SemaphoreType.DMA((IDX_RING,)),
            pltpu.SemaphoreType.DMA((2,)),
            pltpu.SemaphoreType.DMA,
        ],
    )
    return pl.pallas_call(
        _experts_body,
        out_shape=jax.ShapeDtypeStruct((a_rows * w, LANES), F32),
        grid_spec=grid_spec,
        compiler_params=_cparams(("arbitrary", "arbitrary")),
        name="experts",
    )(block_expert, n_used, slot_idx, slot_gate, xn, w_gu, w_gu, s_gu, s_gu, b_gu, b_gu, w_down, s_down, b_down)


def _combine_body(final_norm, h_ref, y0_ref, y1_ref, y2_ref, y3_ref, g_ref, o_ref):
    n, d = h_ref.shape
    acc = h_ref[...] + _load_rows(y0_ref, n, d) + _load_rows(y1_ref, n, d) + _load_rows(y2_ref, n, d) \
        + _load_rows(y3_ref, n, d)
    o_ref[...] = _rms(acc, g_ref[...]) if final_norm else acc


def _combine(h2, y_tok, gain, tm):
    t, d = h2.shape
    w = d // LANES
    nt = t // tm
    final_norm = gain is not None
    if gain is None:
        gain = jnp.ones((d,), F32)
    gain = gain.reshape(1, d)
    plane = lambda k: pl.BlockSpec((tm * w, LANES), lambda i: (k * nt + i, 0))
    return pl.pallas_call(
        functools.partial(_combine_body, final_norm),
        out_shape=jax.ShapeDtypeStruct((t, d), F32),
        grid=(nt,),
        in_specs=[pl.BlockSpec((tm, d), lambda i: (i, 0))] + [plane(k) for k in range(TOP_K)]
                 + [pl.BlockSpec((1, d), lambda i: (0, 0))],
        out_specs=pl.BlockSpec((tm, d), lambda i: (i, 0)),
        compiler_params=_cparams(("arbitrary",)),
        name="combine",
    )(h2, y_tok, y_tok, y_tok, y_tok, gain)


def _routing(top_idx, gates, rows, row_width):
    k, t = top_idx.shape
    a = t * k
    flat_e = top_idx.reshape(-1)
    _, s_asg, s_gate = lax.sort((flat_e, jnp.arange(a, dtype=jnp.int32), gates.reshape(-1)), num_keys=1)
    experts = jnp.arange(N_EXPERTS, dtype=jnp.int32)
    counts = jnp.sum((flat_e[None, :] == experts[:, None]).astype(jnp.int32), axis=1)
    start = jnp.cumsum(counts) - counts
    padded = (counts + rows - 1) // rows * rows
    padded_end = jnp.cumsum(padded)
    padded_start = padded_end - padded
    nb = -(-a // rows) + N_EXPERTS
    n_used = (padded_end[-1] // rows).astype(jnp.int32)
    blk = jnp.arange(nb, dtype=jnp.int32)
    bexp = jnp.minimum(jnp.sum((padded_end[None, :] <= (blk * rows)[:, None]).astype(jnp.int32), axis=1),
                       N_EXPERTS - 1)
    bexp = jnp.where(blk < n_used, bexp, bexp[jnp.maximum(n_used - 1, 0)])
    off = blk * rows - padded_start[bexp]
    first = jnp.clip(start[bexp] + off, 0, a)
    nvalid = jnp.where(blk < n_used, jnp.clip(counts[bexp] - off, 0, rows), 0)
    r = jnp.arange(rows, dtype=jnp.int32)[None, :]
    pos = jnp.minimum(first[:, None] + r, a - 1)
    asg = s_asg[pos]
    gate = s_gate[pos]
    valid = r < nvalid[:, None]
    slot_src = jnp.where(valid, asg % t, 0)
    slot_dst = jnp.where(valid, asg, a + r)
    slot_gate = jnp.where(valid, gate, 0.0)
    slot_idx = jnp.concatenate([slot_src, slot_dst], axis=1) * row_width
    return bexp, n_used.reshape(1), slot_idx, slot_gate.reshape(nb, rows, 1)


def _moe_stage(h2, xn, top_idx, gates, w_gate_up, b_gate_up, w_down, b_down, norm_final):
    t, d = h2.shape
    bexp, n_used, slot_idx, slot_gate = _routing(top_idx, gates, MOE_ROWS, d // LANES)
    e = w_gate_up.shape[0]
    wq_gu, s_gu = _quant_weights(w_gate_up, MOE_TF)
    wq_d, s_d = _quant_weights(w_down, MOE_TF)
    y_tok = _experts(bexp, n_used, slot_idx, slot_gate, xn, wq_gu, s_gu, b_gate_up.reshape(e, 1, -1),
                     wq_d, s_d, b_down.reshape(e, 1, -1))
    return _combine(h2, y_tok, norm_final, min(256, t))


def _permute_w_in(w_in):
    sp = np.cumsum([0, ATT_WIDTH, KV_LATENT, IDX_WIDTH, IDX_DIM, IDX_HEADS, RET_QK_WIDTH, RET_QK_WIDTH,
                    RET_WIDTH, RET_WIDTH])
    piece = lambda i: w_in[:, int(sp[i]):int(sp[i + 1])]
    d = w_in.shape[0]
    cols = [piece(0), piece(2), piece(7), piece(8), piece(5), piece(6), piece(1), piece(3), piece(4)]
    used = sum(c.shape[1] for c in cols)
    cols.append(jnp.zeros((d, PROJ_WIDTH - used), w_in.dtype))
    return jnp.concatenate(cols, axis=1).astype(BF16)


def _rotary_tables(positions):
    half = RET_QK_DIM // 2
    inv = ROPE_BASE ** (-jnp.arange(0, RET_QK_DIM, 2, dtype=F32) / RET_QK_DIM)
    ang = positions.astype(F32).reshape(-1, 1) * inv
    cos, sin = jnp.cos(ang), jnp.sin(ang)
    return jnp.concatenate([cos, cos], axis=1), jnp.concatenate([-sin, sin], axis=1)


def _front(x, positions, norm_mix, w_in, kv_norm, idx_k_norm):
    b, s, d = x.shape
    t = b * s
    tm = min(1024, t)
    proj = _proj(x.reshape(t, d), norm_mix.reshape(1, d), _permute_w_in(w_in), tm)
    cos2, sin2 = _rotary_tables(positions)
    g_k = jnp.concatenate([idx_k_norm, jnp.zeros((LANES - IDX_DIM,), F32)]).reshape(1, LANES)
    c, kidx, widx, q_rot, k_rot = _prep(proj, cos2, sin2, kv_norm.reshape(1, KV_LATENT), g_k, tm)
    return dict(proj=proj, c=c, kidx=kidx, widx=widx, q_rot=q_rot, k_rot=k_rot)


def _mid_stage(x, mem, y_att, y_ret, w_out, norm_cross, norm_mem, w_cq, w_ck, w_cv, w_co, norm_ffn,
               w_router, b_router):
    b, s, d = x.shape
    km, vm = _memkv(mem, norm_mem.reshape(1, d), w_ck.astype(BF16), w_cv.astype(BF16))
    w_out_b16 = w_out.astype(BF16)
    return _mid(x.reshape(b * s, d), y_att, y_ret, w_out_b16[:ATT_WIDTH], w_out_b16[ATT_WIDTH:],
                norm_cross.reshape(1, d), w_cq.astype(BF16), km, vm, w_co.astype(BF16),
                norm_ffn.reshape(1, d), w_router.T.astype(BF16), b_router.reshape(N_EXPERTS, 1),
                b, s, min(512, s))


def _ret_stage(st, b, s):
    q_rot = st["q_rot"].reshape(b, s, RET_QK_WIDTH)
    kt_rot = jnp.swapaxes(st["k_rot"].reshape(b, s, RET_QK_WIDTH), 1, 2)
    return _retention(q_rot, kt_rot, st["proj"], b, s)


def kernel(x, mem, positions, norm_mix, w_in, kv_norm, idx_k_norm, w_uk, w_uv, rel_bias, w_out, norm_cross,
           norm_mem, w_cq, w_ck, w_cv, w_co, norm_ffn, w_router, b_router, w_gate_up, b_gate_up, w_down, b_down,
           norm_final):
    b, s, d = x.shape
    depth = norm_mix.shape[0]
    h = x
    for l in range(depth):
        st = _front(h, positions, norm_mix[l], w_in[l], kv_norm[l], idx_k_norm[l])
        y_att = _dsa_stage(st, w_uk[l], w_uv[l], rel_bias, b, s)
        y_ret = _ret_stage(st, b, s)
        h2, xn, top_idx, gates = _mid_stage(h, mem, y_att, y_ret, w_out[l], norm_cross[l], norm_mem[l], w_cq[l],
                                            w_ck[l], w_cv[l], w_co[l], norm_ffn[l], w_router[l], b_router[l])
        gain = norm_final if l == depth - 1 else None
        h = _moe_stage(h2, xn, top_idx, gates, w_gate_up[l], b_gate_up[l], w_down[l], b_down[l], gain).reshape(b, s, d)
    return h
```

```python
import functools
import math

import jax
import jax.numpy as jnp
import numpy as np
from jax import lax
from jax.experimental import pallas as pl
from jax.experimental.pallas import tpu as pltpu

F32 = jnp.float32
BF16 = jnp.bfloat16

D_MODEL = 2048
N_ATT_HEADS = 8
ATT_HEAD_DIM = 128
KV_LATENT = 256
IDX_HEADS = 16
IDX_DIM = 64
IDX_TOPK_MAX = 256
N_RET_HEADS = 4
RET_QK_DIM = 128
RET_V_DIM = 256
ROPE_BASE = 10000.0
N_BUCKETS = 32
MAX_DISTANCE = 128
N_CROSS_HEADS = 4
CROSS_HEAD_DIM = 128
N_EXPERTS = 32
TOP_K = 4
D_FF = D_MODEL
SWIGLU_LIMIT = 7.0
SWIGLU_ALPHA = 1.702
EPS = 1e-6

ATT_WIDTH = N_ATT_HEADS * ATT_HEAD_DIM
RET_WIDTH = N_RET_HEADS * RET_V_DIM
RET_QK_WIDTH = N_RET_HEADS * RET_QK_DIM
IDX_WIDTH = IDX_HEADS * IDX_DIM
CROSS_WIDTH = N_CROSS_HEADS * CROSS_HEAD_DIM

LANES = 128
VMEM_LIMIT = 56 * 1024 * 1024

COL_Q_ATT = 0
COL_Q_IDX = COL_Q_ATT + ATT_WIDTH
COL_V_RET = COL_Q_IDX + IDX_WIDTH
COL_G_RET = COL_V_RET + RET_WIDTH
COL_Q_RET = COL_G_RET + RET_WIDTH
COL_K_RET = COL_Q_RET + RET_QK_WIDTH
COL_C_KV = COL_K_RET + RET_QK_WIDTH
COL_KW = COL_C_KV + KV_LATENT
PROJ_TN = 512
PROJ_WIDTH = 5632

Q_BLOCK = 256
KEY_TILE = 512
KEY_PAD = KEY_TILE - Q_BLOCK
NEG_BIG = -1e30
LOG2E = math.log2(math.e)

RET_CHUNK = 256

MOE_ROWS = 512
IDX_RING = 3
MOE_TF = 1024


def _cparams(sem, vmem=VMEM_LIMIT):
    return pltpu.CompilerParams(dimension_semantics=sem, vmem_limit_bytes=vmem)


def _rms(x, g):
    return x * lax.rsqrt(jnp.mean(x * x, axis=-1, keepdims=True) + EPS) * g


ROW_PITCH = 20


def _store_rows(ref, x, pitch=None):
    n, d = x.shape
    w = d // LANES
    for a in range(w):
        ref[pl.ds(a, n, stride=pitch or w), :] = x[:, a * LANES:(a + 1) * LANES].astype(ref.dtype)


def _load_rows(ref, n, d, pitch=None):
    w = d // LANES
    return jnp.concatenate([ref[pl.ds(a, n, stride=pitch or w), :] for a in range(w)], axis=1)


def _proj_body(x_ref, g_ref, w_ref, o_ref, xn_ref):
    @pl.when(pl.program_id(1) == 0)
    def _():
        xn_ref[...] = _rms(x_ref[...], g_ref[...]).astype(BF16)

    o_ref[...] = jnp.dot(xn_ref[...], w_ref[...], preferred_element_type=F32).astype(o_ref.dtype)


def _proj(x2, gain, w_p, tm):
    t, d = x2.shape
    n = w_p.shape[1]
    return pl.pallas_call(
        _proj_body,
        out_shape=jax.ShapeDtypeStruct((t, n), BF16),
        grid=(t // tm, n // PROJ_TN),
        in_specs=[
            pl.BlockSpec((tm, d), lambda i, j: (i, 0)),
            pl.BlockSpec((1, d), lambda i, j: (0, 0)),
            pl.BlockSpec((d, PROJ_TN), lambda i, j: (0, j)),
        ],
        out_specs=pl.BlockSpec((tm, PROJ_TN), lambda i, j: (i, j)),
        scratch_shapes=[pltpu.VMEM((tm, d), BF16)],
        compiler_params=_cparams(("arbitrary", "arbitrary")),
        name="proj",
    )(x2, gain, w_p)


def _prep_body(c_ref, kw_ref, q_ref, k_ref, cos_ref, sin_ref, gkv_ref, gk_ref,
               c_out, kidx_out, widx_out, q_out, k_out):
    c_out[...] = _rms(c_ref[...].astype(F32), gkv_ref[...]).astype(BF16)

    kw = kw_ref[...].astype(F32)
    lane = lax.broadcasted_iota(jnp.int32, kw.shape, 1)
    is_k = lane < IDX_DIM
    kk = jnp.where(is_k, kw, 0.0)
    ms = jnp.sum(kk * kk, axis=-1, keepdims=True) * (1.0 / IDX_DIM)
    kidx_out[...] = (kk * lax.rsqrt(ms + EPS) * gk_ref[...]).astype(BF16)
    widx_out[...] = kw * (IDX_HEADS ** -0.5 * IDX_DIM ** -0.5)

    cos2 = cos_ref[...]
    sin2 = sin_ref[...]
    for h in range(N_RET_HEADS):
        sl = slice(h * RET_QK_DIM, (h + 1) * RET_QK_DIM)
        qh = q_ref[:, sl].astype(F32)
        kh = k_ref[:, sl].astype(F32)
        q_out[:, sl] = (qh * cos2 + pltpu.roll(qh, RET_QK_DIM // 2, 1) * sin2).astype(BF16)
        k_out[:, sl] = ((kh * cos2 + pltpu.roll(kh, RET_QK_DIM // 2, 1) * sin2)
                        * (RET_QK_DIM ** -0.5)).astype(BF16)


def _prep(proj, cos2, sin2, g_kv, g_k, tm):
    t = proj.shape[0]
    row = lambda i: (i, 0)
    return pl.pallas_call(
        _prep_body,
        out_shape=(
            jax.ShapeDtypeStruct((t, KV_LATENT), BF16),
            jax.ShapeDtypeStruct((t, LANES), BF16),
            jax.ShapeDtypeStruct((t, LANES), F32),
            jax.ShapeDtypeStruct((t, RET_QK_WIDTH), BF16),
            jax.ShapeDtypeStruct((t, RET_QK_WIDTH), BF16),
        ),
        grid=(t // tm,),
        in_specs=[
            pl.BlockSpec((tm, KV_LATENT), lambda i: (i, COL_C_KV // KV_LATENT)),
            pl.BlockSpec((tm, LANES), lambda i: (i, COL_KW // LANES)),
            pl.BlockSpec((tm, RET_QK_WIDTH), lambda i: (i, COL_Q_RET // RET_QK_WIDTH)),
            pl.BlockSpec((tm, RET_QK_WIDTH), lambda i: (i, COL_K_RET // RET_QK_WIDTH)),
            pl.BlockSpec((tm, LANES), row),
            pl.BlockSpec((tm, LANES), row),
            pl.BlockSpec((1, KV_LATENT), lambda i: (0, 0)),
            pl.BlockSpec((1, LANES), lambda i: (0, 0)),
        ],
        out_specs=(
            pl.BlockSpec((tm, KV_LATENT), row),
            pl.BlockSpec((tm, LANES), row),
            pl.BlockSpec((tm, LANES), row),
            pl.BlockSpec((tm, RET_QK_WIDTH), row),
            pl.BlockSpec((tm, RET_QK_WIDTH), row),
        ),
        compiler_params=_cparams(("arbitrary",)),
        name="prep",
    )(proj, proj, proj, proj, cos2, sin2, g_kv, g_k)


def _ret_body(q_ref, kt_ref, v_ref, g_ref, dec_ref, cross_ref, state_ref, cd_ref, o_ref, r_ref):
    @pl.when(pl.program_id(1) == 0)
    def _():
        r_ref[...] = jnp.zeros_like(r_ref)

    for h in range(N_RET_HEADS):
        qs = slice(h * RET_QK_DIM, (h + 1) * RET_QK_DIM)
        vs = slice(h * RET_V_DIM, (h + 1) * RET_V_DIM)
        q = q_ref[0, :, qs]
        kt = kt_ref[0, qs, :]
        v = v_ref[:, vs]
        r_old = r_ref[h]
        inner = jnp.dot(q, kt, preferred_element_type=F32) * dec_ref[h]
        qc = (q.astype(F32) * cross_ref[h]).astype(BF16)
        o = (jnp.dot(inner.astype(BF16), v, preferred_element_type=F32)
             + jnp.dot(qc, r_old.astype(BF16), preferred_element_type=F32))
        ks = (kt.astype(F32) * state_ref[h]).astype(BF16)
        r_ref[h] = r_old * cd_ref[h] + jnp.dot(ks, v, preferred_element_type=F32)
        o = o * lax.rsqrt(jnp.mean(o * o, axis=-1, keepdims=True) + EPS)
        g = g_ref[:, vs].astype(F32)
        o_ref[:, vs] = (g * jax.nn.sigmoid(g) * o).astype(o_ref.dtype)


def _retention(q_rot, kt_rot, proj, b, s):
    c = RET_CHUNK
    n = s // c
    hh = N_RET_HEADS
    log_gamma = jnp.log(1.0 - 2.0 ** (-5.0 - jnp.arange(hh, dtype=F32)))
    j = jnp.arange(c, dtype=F32)
    diff = j[:, None] - j[None, :]
    dec = jnp.where(diff >= 0, jnp.exp(log_gamma[:, None, None] * jnp.maximum(diff, 0.0)), 0.0)
    cross = jnp.broadcast_to(jnp.exp(log_gamma[:, None] * (j + 1.0))[:, :, None], (hh, c, RET_QK_DIM))
    state = jnp.exp(log_gamma[:, None] * (c - 1.0 - j))[:, None, :]
    cdec = jnp.broadcast_to(jnp.exp(log_gamma * c)[:, None, None], (hh, 1, RET_V_DIM))
    const = lambda bi, ci: (0, 0, 0)
    return pl.pallas_call(
        _ret_body,
        out_shape=jax.ShapeDtypeStruct((b * s, RET_WIDTH), BF16),
        grid=(b, n),
        in_specs=[
            pl.BlockSpec((1, c, RET_QK_WIDTH), lambda bi, ci: (bi, ci, 0)),
            pl.BlockSpec((1, RET_QK_WIDTH, c), lambda bi, ci: (bi, 0, ci)),
            pl.BlockSpec((c, RET_WIDTH), lambda bi, ci: (bi * n + ci, COL_V_RET // RET_WIDTH)),
            pl.BlockSpec((c, RET_WIDTH), lambda bi, ci: (bi * n + ci, COL_G_RET // RET_WIDTH)),
            pl.BlockSpec((hh, c, c), const),
            pl.BlockSpec((hh, c, RET_QK_DIM), const),
            pl.BlockSpec((hh, 1, c), const),
            pl.BlockSpec((hh, 1, RET_V_DIM), const),
        ],
        out_specs=pl.BlockSpec((c, RET_WIDTH), lambda bi, ci: (bi * n + ci, 0)),
        scratch_shapes=[pltpu.VMEM((hh, RET_QK_DIM, RET_V_DIM), F32)],
        compiler_params=_cparams(("arbitrary", "arbitrary")),
        name="ret",
    )(q_rot, kt_rot, proj, proj, dec, cross, state, cdec)


def _dsa_body(topk, kidx_ref, qit_ref, w_ref, c_ref, ct_ref, qt_ref, wuk_ref, wuvt_ref, bias_ref, y_ref,
              sc_ref, qlt_ref, acc_ref, m_ref, l_ref):
    qb = pl.program_id(1)
    tk = KEY_TILE
    half = tk // 2
    e_pad = (qb + 1) * Q_BLOCK + KEY_PAD
    n_tiles = e_pad // tk
    nh = N_ATT_HEADS
    lane_q = qb * Q_BLOCK + lax.broadcasted_iota(jnp.int32, (1, Q_BLOCK), 1)

    def tile_start(j):
        return pl.multiple_of(e_pad - (j + 1) * tk, Q_BLOCK)

    for h in range(nh):
        sl = slice(h * Q_BLOCK, (h + 1) * Q_BLOCK)
        hd = slice(h * ATT_HEAD_DIM, (h + 1) * ATT_HEAD_DIM)
        qlt_ref[:, sl] = (jnp.dot(wuk_ref[h], qt_ref[0, 0, hd, :], preferred_element_type=F32)
                          * (ATT_HEAD_DIM ** -0.5 * LOG2E)).astype(BF16)

    def score_tile(j, carry):
        lo8, hi8 = carry
        r0 = tile_start(j)
        for part in range(2):
            rs = pl.multiple_of(r0 + part * half, Q_BLOCK)
            z = jnp.dot(kidx_ref[0, pl.ds(rs, half), :], qit_ref[0, 0], preferred_element_type=F32)
            tot = jnp.zeros((half, Q_BLOCK), F32)
            for h in range(IDX_HEADS):
                sl = slice(h * Q_BLOCK, (h + 1) * Q_BLOCK)
                tot = tot + jnp.maximum(z[:, sl], 0.0) * w_ref[0, 0, :, sl]
            key = rs - KEY_PAD + lax.broadcasted_iota(jnp.int32, (half, Q_BLOCK), 0)
            ok = (key >= 0) & (key <= lane_q)
            val = jnp.where(ok, tot, NEG_BIG)
            sc_ref[pl.ds(rs, half), :] = val
            hi8 = jnp.maximum(hi8, jnp.max(val.reshape(half // 8, 8, Q_BLOCK), axis=0))
            lo8 = jnp.minimum(lo8, jnp.min(jnp.where(ok, tot, -NEG_BIG).reshape(half // 8, 8, Q_BLOCK), axis=0))
        return lo8, hi8
    lo8, hi8 = lax.fori_loop(0, n_tiles, score_tile,
                             (jnp.full((8, Q_BLOCK), -NEG_BIG, F32), jnp.full((8, Q_BLOCK), NEG_BIG, F32)))

    lo0 = jnp.min(lo8, axis=0, keepdims=True)
    hi0 = jnp.max(hi8, axis=0, keepdims=True)
    search = lane_q + 1 > topk

    n_chain = 4
    rows_chain = tk // n_chain

    def count_ge(x):
        def body(j, cnts):
            r0 = tile_start(j)
            out = []
            for k in range(n_chain):
                s = sc_ref[pl.ds(pl.multiple_of(r0 + k * rows_chain, rows_chain), rows_chain), :]
                ind = jnp.where(s >= x, 1.0, 0.0).reshape(rows_chain // 8, 8, Q_BLOCK)
                out.append(cnts[k] + jnp.sum(ind, axis=0))
            return tuple(out)
        cnts = lax.fori_loop(0, n_tiles, body, tuple(jnp.zeros((8, Q_BLOCK), F32) for _ in range(n_chain)))
        return jnp.sum((cnts[0] + cnts[1]) + (cnts[2] + cnts[3]), axis=0, keepdims=True)

    steps_per_check = 3

    def bis_cond(st):
        it, _, _, _, active = st
        return (it < 42) & (active > 0.0)

    def bis_step(_, st):
        lo, hi, done = st
        mid = lo + 0.5 * (hi - lo)
        cnt = count_ge(mid)
        ge = cnt >= topk
        open_ = done < 0.5
        lo = jnp.where(open_ & ge, mid, lo)
        hi = jnp.where(open_ & (~ge), mid, hi)
        done = jnp.where(cnt == topk, 1.0, done)
        return lo, hi, done

    def bis_body(st):
        it, lo, hi, done, _ = st
        lo, hi, done = lax.fori_loop(0, steps_per_check, bis_step, (lo, hi, done))
        return it + steps_per_check, lo, hi, done, jnp.max(1.0 - done)

    done0 = jnp.where(search, 0.0, 1.0)
    _, lo_f, _, _, _ = lax.while_loop(bis_cond, bis_body, (jnp.int32(0), lo0, hi0, done0, jnp.max(1.0 - done0)))
    thr = jnp.where(search, lo_f, 0.5 * NEG_BIG)

    m_ref[...] = jnp.full(m_ref.shape, NEG_BIG, F32)
    l_ref[...] = jnp.zeros(l_ref.shape, F32)
    acc_ref[...] = jnp.zeros(acc_ref.shape, F32)

    n_grp = 4
    hg = nh // n_grp

    def attend(segments):
        scores, ct_tiles = [], []
        for rs, n, with_bias in segments:
            c_tile = c_ref[0, pl.ds(rs, n), :]
            ct_tiles.append(ct_ref[0, :, pl.ds(rs, n)])
            pen = jnp.where(sc_ref[pl.ds(rs, n), :] >= thr, 0.0, NEG_BIG)
            pen = jnp.concatenate([pen] * hg, axis=1)
            for g in range(n_grp):
                ls = slice(g * hg * Q_BLOCK, (g + 1) * hg * Q_BLOCK)
                st = jnp.dot(c_tile, qlt_ref[:, ls], preferred_element_type=F32) + pen
                if with_bias:
                    st = st + bias_ref[:, ls]
                scores.append(st)
        for k, ct_tile in enumerate(ct_tiles):
            for g in range(n_grp):
                ls = slice(g * hg * Q_BLOCK, (g + 1) * hg * Q_BLOCK)
                st = scores[k * n_grp + g]
                m_old = m_ref[:, ls]
                m_new = jnp.maximum(m_old, jnp.max(st, axis=0, keepdims=True))
                alpha = jnp.exp2(m_old - m_new)
                p = jnp.exp2(st - m_new)
                l_ref[:, ls] = alpha * l_ref[:, ls] + jnp.sum(p, axis=0, keepdims=True)
                acc_ref[:, ls] = acc_ref[:, ls] * alpha + jnp.dot(ct_tile, p.astype(BF16),
                                                                  preferred_element_type=F32)
                m_ref[:, ls] = m_new

    r_first = tile_start(0)
    near = 2 * Q_BLOCK
    if near == tk:
        attend([(r_first, tk, True)])
    else:
        attend([(pl.multiple_of(r_first + tk - near, Q_BLOCK), near, True), (r_first, tk - near, False)])

    def attend_pair(i, carry):
        attend([(tile_start(2 * i + 1), tk, False), (tile_start(2 * i + 2), tk, False)])
        return carry
    n_rest = n_tiles - 1
    lax.fori_loop(0, n_rest // 2, attend_pair, 0)

    @pl.when(n_rest % 2 == 1)
    def _():
        attend([(tile_start(n_tiles - 1), tk, False)])

    ot = acc_ref[...] / l_ref[...]
    for h in range(nh):
        sl = slice(h * Q_BLOCK, (h + 1) * Q_BLOCK)
        y_ref[0, h * ATT_HEAD_DIM:(h + 1) * ATT_HEAD_DIM, :] = jnp.dot(
            wuvt_ref[h], ot[:, sl].astype(BF16), preferred_element_type=F32).astype(y_ref.dtype)


def _dsa(kidx_p, qit, widx_t, c_p, ct_p, qt, wuk_h, wuvt_h, bias_t, b, s, topk):
    nqb = s // Q_BLOCK
    sp = s + KEY_PAD
    nh = N_ATT_HEADS
    per_b = lambda bi, qi: (bi, 0, 0)
    per_q = lambda bi, qi: (bi, qi, 0, 0)
    c3 = lambda bi, qi: (0, 0, 0)
    return pl.pallas_call(
        functools.partial(_dsa_body, topk),
        out_shape=jax.ShapeDtypeStruct((b, ATT_WIDTH, s), BF16),
        grid=(b, nqb),
        in_specs=[
            pl.BlockSpec((1, sp, LANES), per_b),
            pl.BlockSpec((1, 1, LANES, IDX_HEADS * Q_BLOCK), per_q),
            pl.BlockSpec((1, 1, 1, IDX_HEADS * Q_BLOCK), per_q),
            pl.BlockSpec((1, sp, KV_LATENT), per_b),
            pl.BlockSpec((1, KV_LATENT, sp), per_b),
            pl.BlockSpec((1, 1, ATT_WIDTH, Q_BLOCK), per_q),
            pl.BlockSpec((nh, KV_LATENT, ATT_HEAD_DIM), c3),
            pl.BlockSpec((nh, ATT_HEAD_DIM, KV_LATENT), c3),
            pl.BlockSpec((2 * Q_BLOCK, nh * Q_BLOCK), lambda bi, qi: (0, 0)),
        ],
        out_specs=pl.BlockSpec((1, ATT_WIDTH, Q_BLOCK), lambda bi, qi: (bi, 0, qi)),
        scratch_shapes=[
            pltpu.VMEM((sp, Q_BLOCK), F32),
            pltpu.VMEM((KV_LATENT, nh * Q_BLOCK), BF16),
            pltpu.VMEM((KV_LATENT, nh * Q_BLOCK), F32),
            pltpu.VMEM((1, nh * Q_BLOCK), F32),
            pltpu.VMEM((1, nh * Q_BLOCK), F32),
        ],
        compiler_params=_cparams(("arbitrary", "arbitrary")),
        name="dsa",
    )(kidx_p, qit, widx_t, c_p, ct_p, qt, wuk_h, wuvt_h, bias_t)


def _t5_bucket(dist):
    n = jnp.maximum(dist, 0)
    max_exact = N_BUCKETS // 2
    nf = jnp.maximum(n, 1).astype(F32)
    large = max_exact + (jnp.log(nf / max_exact) / math.log(MAX_DISTANCE / max_exact)
                         * (N_BUCKETS - max_exact)).astype(jnp.int32)
    return jnp.where(n < max_exact, n, jnp.minimum(large, N_BUCKETS - 1))


def _dsa_stage(st, w_uk, w_uv, rel_bias, b, s):
    assert Q_BLOCK >= MAX_DISTANCE and 2 * Q_BLOCK <= KEY_TILE and s % Q_BLOCK == 0
    nqb = s // Q_BLOCK
    nh = N_ATT_HEADS
    proj = st["proj"].reshape(b, nqb, Q_BLOCK, PROJ_WIDTH)
    qt = jnp.swapaxes(proj[..., COL_Q_ATT:COL_Q_ATT + ATT_WIDTH], 2, 3)
    qi = proj[..., COL_Q_IDX:COL_Q_IDX + IDX_WIDTH].reshape(b, nqb, Q_BLOCK, IDX_HEADS, IDX_DIM)
    qit = jnp.transpose(qi, (0, 1, 4, 3, 2)).reshape(b, nqb, IDX_DIM, IDX_HEADS * Q_BLOCK)
    qit = jnp.pad(qit, ((0, 0), (0, 0), (0, LANES - IDX_DIM), (0, 0)))
    wi = st["widx"].reshape(b, nqb, Q_BLOCK, LANES)[..., IDX_DIM:IDX_DIM + IDX_HEADS]
    widx_t = jnp.swapaxes(wi, 2, 3).reshape(b, nqb, 1, IDX_HEADS * Q_BLOCK)
    front = ((0, 0), (KEY_PAD, 0), (0, 0))
    kidx_p = jnp.pad(st["kidx"].reshape(b, s, LANES), front)
    c_p = jnp.pad(st["c"].reshape(b, s, KV_LATENT), front)
    ct_p = jnp.swapaxes(c_p, 1, 2)
    wuk_h = jnp.transpose(w_uk, (1, 0, 2)).astype(BF16)
    wuvt_h = jnp.transpose(w_uv, (1, 2, 0)).astype(BF16)
    i = jnp.arange(2 * Q_BLOCK)[:, None]
    j = jnp.arange(Q_BLOCK)[None, :]
    rb = (rel_bias - rel_bias[N_BUCKETS - 1]) * LOG2E
    bias = rb[_t5_bucket(j - i + Q_BLOCK)]
    bias_t = jnp.transpose(bias, (0, 2, 1)).reshape(2 * Q_BLOCK, nh * Q_BLOCK)
    topk = min(IDX_TOPK_MAX, s // 4)
    yt = _dsa(kidx_p, qit, widx_t, c_p, ct_p, qt, wuk_h, wuvt_h, bias_t, b, s, topk)
    return jnp.swapaxes(yt, 1, 2).reshape(b * s, ATT_WIDTH)


def _memkv_body(m_ref, g_ref, wk_ref, wv_ref, k_ref, v_ref):
    mn = _rms(m_ref[0], g_ref[...]).astype(BF16)
    k_ref[0] = jnp.dot(mn, wk_ref[...], preferred_element_type=F32).astype(BF16)
    v_ref[0] = jnp.dot(mn, wv_ref[...], preferred_element_type=F32).astype(BF16)


def _memkv(mem, gain, w_ck, w_cv):
    b, m, d = mem.shape
    w = w_ck.shape[1]
    const = lambda i: (0, 0)
    out = jax.ShapeDtypeStruct((b, m, w), BF16)
    return pl.pallas_call(
        _memkv_body,
        out_shape=(out, out),
        grid=(b,),
        in_specs=[
            pl.BlockSpec((1, m, d), lambda i: (i, 0, 0)),
            pl.BlockSpec((1, d), const),
            pl.BlockSpec((d, w), const),
            pl.BlockSpec((d, w), const),
        ],
        out_specs=(pl.BlockSpec((1, m, w), lambda i: (i, 0, 0)), pl.BlockSpec((1, m, w), lambda i: (i, 0, 0))),
        compiler_params=_cparams(("arbitrary",)),
        name="memkv",
    )(mem, gain, w_ck, w_cv)


def _mid_body(x_ref, ya_ref, yr_ref, woa_ref, wob_ref, gc_ref, wq_ref, km_ref, vm_ref, wo_ref,
              gf_ref, wr_ref, br_ref, h2_ref, xn_ref, idx_ref, gate_ref):
    h1 = (x_ref[...]
          + jnp.dot(ya_ref[...], woa_ref[...], preferred_element_type=F32)
          + jnp.dot(yr_ref[...], wob_ref[...], preferred_element_type=F32))
    hn = _rms(h1, gc_ref[...]).astype(BF16)
    q = jnp.dot(hn, wq_ref[...], preferred_element_type=F32).astype(BF16)
    heads = []
    for h in range(N_CROSS_HEADS):
        sl = slice(h * CROSS_HEAD_DIM, (h + 1) * CROSS_HEAD_DIM)
        s = lax.dot_general(q[:, sl], km_ref[0, :, sl], (((1,), (1,)), ((), ())),
                            preferred_element_type=F32) * (CROSS_HEAD_DIM ** -0.5)
        p = jnp.exp(s - jnp.max(s, axis=-1, keepdims=True))
        p = p / jnp.sum(p, axis=-1, keepdims=True)
        heads.append(jnp.dot(p.astype(BF16), vm_ref[0, :, sl], preferred_element_type=F32).astype(BF16))
    o = jnp.concatenate(heads, axis=-1)
    h2 = h1 + jnp.dot(o, wo_ref[...], preferred_element_type=F32)
    h2_ref[...] = h2

    xn = _rms(h2, gf_ref[...])
    _store_rows(xn_ref, xn)
    logit = lax.dot_general(wr_ref[...], xn.astype(BF16), (((1,), (1,)), ((), ())),
                            preferred_element_type=F32) + br_ref[...]
    eid = lax.broadcasted_iota(jnp.int32, logit.shape, 0)
    vals, ids = [], []
    for _ in range(TOP_K):
        mx = jnp.max(logit, axis=0, keepdims=True)
        sel = jnp.min(jnp.where(logit == mx, eid, N_EXPERTS), axis=0, keepdims=True)
        vals.append(mx)
        ids.append(sel)
        logit = jnp.where(eid == sel, -jnp.inf, logit)
    top = jnp.concatenate(vals, axis=0)
    e = jnp.exp(top - vals[0])
    gate_ref[...] = e / jnp.sum(e, axis=0, keepdims=True)
    idx_ref[...] = jnp.concatenate(ids, axis=0)


def _mid(x2, ya, yr, w_out_a, w_out_b, g_cross, w_cq, km, vm, w_co, g_ffn, w_rt, b_r, b, s, tm):
    t, d = x2.shape
    nt = s // tm
    row = lambda bi, i: (bi * nt + i, 0)
    const = lambda bi, i: (0, 0)
    once = pl.Buffered(1)
    m = km.shape[1]
    return pl.pallas_call(
        _mid_body,
        out_shape=(
            jax.ShapeDtypeStruct((t, d), F32),
            jax.ShapeDtypeStruct((t * (d // LANES), LANES), F32),
            jax.ShapeDtypeStruct((TOP_K, t), jnp.int32),
            jax.ShapeDtypeStruct((TOP_K, t), F32),
        ),
        grid=(b, nt),
        in_specs=[
            pl.BlockSpec((tm, d), row),
            pl.BlockSpec((tm, ATT_WIDTH), row),
            pl.BlockSpec((tm, RET_WIDTH), row),
            pl.BlockSpec((ATT_WIDTH, d), const, pipeline_mode=once),
            pl.BlockSpec((RET_WIDTH, d), const, pipeline_mode=once),
            pl.BlockSpec((1, d), const),
            pl.BlockSpec((d, CROSS_WIDTH), const, pipeline_mode=once),
            pl.BlockSpec((1, m, CROSS_WIDTH), lambda bi, i: (bi, 0, 0)),
            pl.BlockSpec((1, m, CROSS_WIDTH), lambda bi, i: (bi, 0, 0)),
            pl.BlockSpec((CROSS_WIDTH, d), const, pipeline_mode=once),
            pl.BlockSpec((1, d), const),
            pl.BlockSpec((N_EXPERTS, d), const),
            pl.BlockSpec((N_EXPERTS, 1), const),
        ],
        out_specs=(
            pl.BlockSpec((tm, d), row),
            pl.BlockSpec((tm * (d // LANES), LANES), row),
            pl.BlockSpec((TOP_K, tm), lambda bi, i: (0, bi * nt + i)),
            pl.BlockSpec((TOP_K, tm), lambda bi, i: (0, bi * nt + i)),
        ),
        compiler_params=_cparams(("arbitrary", "arbitrary")),
        name="mid",
    )(x2, ya, yr, w_out_a, w_out_b, g_cross, w_cq, km, vm, w_co, g_ffn, w_rt, b_r)


FP8 = jnp.float8_e4m3fn
FP8_MAX = 448.0
TINY = 1e-30


def _quant_body(w_ref, q_ref, s_ref):
    w = w_ref[0]
    amax = jnp.maximum(jnp.max(jnp.abs(w), axis=0, keepdims=True), TINY)
    q_ref[0] = (w * (FP8_MAX / amax)).astype(FP8)
    s_ref[0] = amax * (1.0 / FP8_MAX)


def _quant_weights(w, tn):
    e, k, n = w.shape
    return pl.pallas_call(
        _quant_body,
        out_shape=(jax.ShapeDtypeStruct((e, k, n), FP8), jax.ShapeDtypeStruct((e, 1, n), F32)),
        grid=(e, n // tn),
        in_specs=[pl.BlockSpec((1, k, tn), lambda i, j: (i, 0, j))],
        out_specs=(pl.BlockSpec((1, k, tn), lambda i, j: (i, 0, j)), pl.BlockSpec((1, 1, tn), lambda i, j: (i, 0, j))),
        compiler_params=_cparams(("arbitrary", "arbitrary")),
        name="quant",
    )(w)


def _quant_rows(x):
    amax = jnp.maximum(jnp.max(jnp.abs(x), axis=1, keepdims=True), TINY)
    return (x * (FP8_MAX / amax)).astype(FP8), amax * (1.0 / FP8_MAX)


def _experts_body(be_ref, nu_ref, idx_hbm, gate_ref, xn_hbm, wg_ref, wu_ref, sg_ref, su_ref, bg_ref, bu_ref,
                  wd_ref, sd_ref, bd_ref, y_hbm, idx_smem, xbuf, xb_ref, sx_ref, acc_ref, ybuf,
                  sem_idx, sem_g, sem_s):
    i = pl.program_id(0)
    j = pl.program_id(1)
    n_used = nu_ref[0]
    rows, d = xb_ref.shape
    w = d // LANES

    def idx_copy(blk, slot):
        return pltpu.make_async_copy(idx_hbm.at[blk],
                                     idx_smem.at[pl.ds(pl.multiple_of(slot * 2 * rows, 2 * rows), 2 * rows)],
                                     sem_idx.at[slot])

    def gather_start(islot, xslot):
        def body(r, carry):
            src = pl.multiple_of(idx_smem[islot * 2 * rows + r], w)
            pltpu.make_async_copy(xn_hbm.at[pl.ds(src, w)], xbuf.at[xslot, pl.ds(r * ROW_PITCH, w)],
                                  sem_g.at[xslot]).start()
            return carry
        lax.fori_loop(0, rows, body, 0, unroll=8)

    def gather_wait(xslot):
        pltpu.make_async_copy(xn_hbm.at[pl.ds(0, rows * w)], xbuf.at[xslot, pl.ds(0, rows * w)],
                              sem_g.at[xslot]).wait()

    def scatter_wait():
        pltpu.make_async_copy(ybuf.at[pl.ds(0, rows * w)], y_hbm.at[pl.ds(0, rows * w)], sem_s).wait()

    @pl.when(j == 0)
    def _():
        @pl.when(i == 0)
        def _():
            ybuf[...] = jnp.zeros_like(ybuf)
            dump = pltpu.make_async_copy(ybuf.at[pl.ds(0, rows * w)],
                                         y_hbm.at[pl.ds(y_hbm.shape[0] - rows * w, rows * w)], sem_s)
            dump.start()
            dump.wait()
            idx_copy(0, 0).start()

            @pl.when(n_used > 1)
            def _():
                idx_copy(1, 1).start()

            idx_copy(0, 0).wait()
            gather_start(0, 0)

        @pl.when(i + 1 < n_used)
        def _():
            idx_copy(i + 1, (i + 1) % IDX_RING).wait()

        @pl.when(i + 2 < n_used)
        def _():
            idx_copy(i + 2, (i + 2) % IDX_RING).start()

        @pl.when(i < n_used)
        def _():
            gather_wait(i % 2)
            xb_ref[...], sx_ref[...] = _quant_rows(_load_rows(xbuf.at[i % 2], rows, d, ROW_PITCH))
            acc_ref[...] = jnp.zeros_like(acc_ref)

    @pl.when(i < n_used)
    def _():
        nxt = jnp.minimum(i + 1, n_used - 1)
        per_step = rows // (D_FF // wd_ref.shape[1])
        gbase = (nxt % IDX_RING) * 2 * rows + j * per_step
        xnext = xbuf.at[(i + 1) % 2]
        for k in range(per_step):
            src = pl.multiple_of(idx_smem[gbase + k], w)
            pltpu.make_async_copy(xn_hbm.at[pl.ds(src, w)], xnext.at[pl.ds((j * per_step + k) * ROW_PITCH, w)],
                                  sem_g.at[(i + 1) % 2]).start()

        n_half = 2
        hr = rows // n_half
        for k in range(n_half):
            rs = slice(k * hr, (k + 1) * hr)
            xb = xb_ref[rs, :]
            sx = sx_ref[rs, :]
            g = jnp.dot(xb, wg_ref[0], preferred_element_type=F32) * sx * sg_ref[0] + bg_ref[0]
            u = jnp.dot(xb, wu_ref[0], preferred_element_type=F32) * sx * su_ref[0] + bu_ref[0]
            gt = jnp.minimum(g, SWIGLU_LIMIT)
            up = jnp.clip(u, -SWIGLU_LIMIT, SWIGLU_LIMIT)
            hq, sh = _quant_rows((up + 1.0) * (gt * jax.nn.sigmoid(SWIGLU_ALPHA * gt)))
            acc_ref[rs, :] += jnp.dot(hq, wd_ref[0], preferred_element_type=F32) * sh * sd_ref[0]

        @pl.when(j == pl.num_programs(1) - 1)
        def _():
            y = (acc_ref[...] + bd_ref[0]) * gate_ref[0]

            @pl.when(i > 0)
            def _():
                scatter_wait()

            _store_rows(ybuf, y, ROW_PITCH)
            islot = i % IDX_RING

            def body(r, carry):
                dst = pl.multiple_of(idx_smem[islot * 2 * rows + rows + r], w)
                pltpu.make_async_copy(ybuf.at[pl.ds(r * ROW_PITCH, w)], y_hbm.at[pl.ds(dst, w)], sem_s).start()
                return carry
            lax.fori_loop(0, rows, body, 0, unroll=8)

            @pl.when(i == n_used - 1)
            def _():
                scatter_wait()
                gather_wait((i + 1) % 2)


def _experts(block_expert, n_used, slot_idx, slot_gate, xn, w_gu, s_gu, b_gu, w_down, s_down, b_down):
    nb = slot_idx.shape[0]
    rows = slot_idx.shape[1] // 2
    e, d, ff2 = w_gu.shape
    w = d // LANES
    t = xn.shape[0] // w
    ff = ff2 // 2
    tf = MOE_TF
    nj = ff // tf
    a_rows = t * TOP_K + rows
    grid_spec = pltpu.PrefetchScalarGridSpec(
        num_scalar_prefetch=2,
        grid=(nb, nj),
        in_specs=[
            pl.BlockSpec(memory_space=pl.ANY),
            pl.BlockSpec((1, rows, 1), lambda i, j, be, nu: (i, 0, 0)),
            pl.BlockSpec(memory_space=pl.ANY),
            pl.BlockSpec((1, d, tf), lambda i, j, be, nu: (be[i], 0, j)),
            pl.BlockSpec((1, d, tf), lambda i, j, be, nu: (be[i], 0, j + nj)),
            pl.BlockSpec((1, 1, tf), lambda i, j, be, nu: (be[i], 0, j)),
            pl.BlockSpec((1, 1, tf), lambda i, j, be, nu: (be[i], 0, j + nj)),
            pl.BlockSpec((1, 1, tf), lambda i, j, be, nu: (be[i], 0, j)),
            pl.BlockSpec((1, 1, tf), lambda i, j, be, nu: (be[i], 0, j + nj)),
            pl.BlockSpec((1, tf, d), lambda i, j, be, nu: (be[i], j, 0)),
            pl.BlockSpec((1, 1, d), lambda i, j, be, nu: (be[i], 0, 0)),
            pl.BlockSpec((1, 1, d), lambda i, j, be, nu: (be[i], 0, 0)),
        ],
        out_specs=pl.BlockSpec(memory_space=pl.ANY),
        scratch_shapes=[
            pltpu.SMEM((IDX_RING * 2 * rows,), jnp.int32),
            pltpu.VMEM((2, rows * ROW_PITCH, LANES), F32),
            pltpu.VMEM((rows, d), FP8),
            pltpu.VMEM((rows, 1), F32),
            pltpu.VMEM((rows, d), F32),
            pltpu.VMEM((rows * ROW_PITCH, LANES), F32),
            pltpu.SemaphoreType.DMA((IDX_RING,)),
            pltpu.SemaphoreType.DMA((2,)),
            pltpu.SemaphoreType.DMA,
        ],
    )
    return pl.pallas_call(
        _experts_body,
        out_shape=jax.ShapeDtypeStruct((a_rows * w, LANES), F32),
        grid_spec=grid_spec,
        compiler_params=_cparams(("arbitrary", "arbitrary")),
        name="experts",
    )(block_expert, n_used, slot_idx, slot_gate, xn, w_gu, w_gu, s_gu, s_gu, b_gu, b_gu, w_down, s_down, b_down)


def _combine_body(final_norm, h_ref, y0_ref, y1_ref, y2_ref, y3_ref, g_ref, o_ref):
    n, d = h_ref.shape
    acc = h_ref[...] + _load_rows(y0_ref, n, d) + _load_rows(y1_ref, n, d) + _load_rows(y2_ref, n, d) \
        + _load_rows(y3_ref, n, d)
    o_ref[...] = _rms(acc, g_ref[...]) if final_norm else acc


def _combine(h2, y_tok, gain, tm):
    t, d = h2.shape
    w = d // LANES
    nt = t // tm
    final_norm = gain is not None
    if gain is None:
        gain = jnp.ones((d,), F32)
    gain = gain.reshape(1, d)
    plane = lambda k: pl.BlockSpec((tm * w, LANES), lambda i: (k * nt + i, 0))
    return pl.pallas_call(
        functools.partial(_combine_body, final_norm),
        out_shape=jax.ShapeDtypeStruct((t, d), F32),
        grid=(nt,),
        in_specs=[pl.BlockSpec((tm, d), lambda i: (i, 0))] + [plane(k) for k in range(TOP_K)]
                 + [pl.BlockSpec((1, d), lambda i: (0, 0))],
        out_specs=pl.BlockSpec((tm, d), lambda i: (i, 0)),
        compiler_params=_cparams(("arbitrary",)),
        name="combine",
    )(h2, y_tok, y_tok, y_tok, y_tok, gain)


def _routing(top_idx, gates, rows, row_width):
    k, t = top_idx.shape
    a = t * k
    flat_e = top_idx.reshape(-1)
    _, s_asg, s_gate = lax.sort((flat_e, jnp.arange(a, dtype=jnp.int32), gates.reshape(-1)), num_keys=1)
    experts = jnp.arange(N_EXPERTS, dtype=jnp.int32)
    counts = jnp.sum((flat_e[None, :] == experts[:, None]).astype(jnp.int32), axis=1)
    start = jnp.cumsum(counts) - counts
    padded = (counts + rows - 1) // rows * rows
    padded_end = jnp.cumsum(padded)
    padded_start = padded_end - padded
    nb = -(-a // rows) + N_EXPERTS
    n_used = (padded_end[-1] // rows).astype(jnp.int32)
    blk = jnp.arange(nb, dtype=jnp.int32)
    bexp = jnp.minimum(jnp.sum((padded_end[None, :] <= (blk * rows)[:, None]).astype(jnp.int32), axis=1),
                       N_EXPERTS - 1)
    bexp = jnp.where(blk < n_used, bexp, bexp[jnp.maximum(n_used - 1, 0)])
    off = blk * rows - padded_start[bexp]
    first = jnp.clip(start[bexp] + off, 0, a)
    nvalid = jnp.where(blk < n_used, jnp.clip(counts[bexp] - off, 0, rows), 0)
    r = jnp.arange(rows, dtype=jnp.int32)[None, :]
    pos = jnp.minimum(first[:, None] + r, a - 1)
    asg = s_asg[pos]
    gate = s_gate[pos]
    valid = r < nvalid[:, None]
    slot_src = jnp.where(valid, asg % t, 0)
    slot_dst = jnp.where(valid, asg, a + r)
    slot_gate = jnp.where(valid, gate, 0.0)
    slot_idx = jnp.concatenate([slot_src, slot_dst], axis=1) * row_width
    return bexp, n_used.reshape(1), slot_idx, slot_gate.reshape(nb, rows, 1)


def _moe_stage(h2, xn, top_idx, gates, w_gate_up, b_gate_up, w_down, b_down, norm_final):
    t, d = h2.shape
    bexp, n_used, slot_idx, slot_gate = _routing(top_idx, gates, MOE_ROWS, d // LANES)
    e = w_gate_up.shape[0]
    wq_gu, s_gu = _quant_weights(w_gate_up, MOE_TF)
    wq_d, s_d = _quant_weights(w_down, MOE_TF)
    y_tok = _experts(bexp, n_used, slot_idx, slot_gate, xn, wq_gu, s_gu, b_gate_up.reshape(e, 1, -1),
                     wq_d, s_d, b_down.reshape(e, 1, -1))
    return _combine(h2, y_tok, norm_final, min(256, t))


def _permute_w_in(w_in):
    sp = np.cumsum([0, ATT_WIDTH, KV_LATENT, IDX_WIDTH, IDX_DIM, IDX_HEADS, RET_QK_WIDTH, RET_QK_WIDTH,
                    RET_WIDTH, RET_WIDTH])
    piece = lambda i: w_in[:, int(sp[i]):int(sp[i + 1])]
    d = w_in.shape[0]
    cols = [piece(0), piece(2), piece(7), piece(8), piece(5), piece(6), piece(1), piece(3), piece(4)]
    used = sum(c.shape[1] for c in cols)
    cols.append(jnp.zeros((d, PROJ_WIDTH - used), w_in.dtype))
    return jnp.concatenate(cols, axis=1).astype(BF16)


def _rotary_tables(positions):
    half = RET_QK_DIM // 2
    inv = ROPE_BASE ** (-jnp.arange(0, RET_QK_DIM, 2, dtype=F32) / RET_QK_DIM)
    ang = positions.astype(F32).reshape(-1, 1) * inv
    cos, sin = jnp.cos(ang), jnp.sin(ang)
    return jnp.concatenate([cos, cos], axis=1), jnp.concatenate([-sin, sin], axis=1)


def _front(x, positions, norm_mix, w_in, kv_norm, idx_k_norm):
    b, s, d = x.shape
    t = b * s
    tm = min(1024, t)
    proj = _proj(x.reshape(t, d), norm_mix.reshape(1, d), _permute_w_in(w_in), tm)
    cos2, sin2 = _rotary_tables(positions)
    g_k = jnp.concatenate([idx_k_norm, jnp.zeros((LANES - IDX_DIM,), F32)]).reshape(1, LANES)
    c, kidx, widx, q_rot, k_rot = _prep(proj, cos2, sin2, kv_norm.reshape(1, KV_LATENT), g_k, tm)
    return dict(proj=proj, c=c, kidx=kidx, widx=widx, q_rot=q_rot, k_rot=k_rot)


def _mid_stage(x, mem, y_att, y_ret, w_out, norm_cross, norm_mem, w_cq, w_ck, w_cv, w_co, norm_ffn,
               w_router, b_router):
    b, s, d = x.shape
    km, vm = _memkv(mem, norm_mem.reshape(1, d), w_ck.astype(BF16), w_cv.astype(BF16))
    w_out_b16 = w_out.astype(BF16)
    return _mid(x.reshape(b * s, d), y_att, y_ret, w_out_b16[:ATT_WIDTH], w_out_b16[ATT_WIDTH:],
                norm_cross.reshape(1, d), w_cq.astype(BF16), km, vm, w_co.astype(BF16),
                norm_ffn.reshape(1, d), w_router.T.astype(BF16), b_router.reshape(N_EXPERTS, 1),
                b, s, min(512, s))


def _ret_stage(st, b, s):
    q_rot = st["q_rot"].reshape(b, s, RET_QK_WIDTH)
    kt_rot = jnp.swapaxes(st["k_rot"].reshape(b, s, RET_QK_WIDTH), 1, 2)
    return _retention(q_rot, kt_rot, st["proj"], b, s)


def kernel(x, mem, positions, norm_mix, w_in, kv_norm, idx_k_norm, w_uk, w_uv, rel_bias, w_out, norm_cross,
           norm_mem, w_cq, w_ck, w_cv, w_co, norm_ffn, w_router, b_router, w_gate_up, b_gate_up, w_down, b_down,
           norm_final):
    b, s, d = x.shape
    depth = norm_mix.shape[0]
    h = x
    for l in range(depth):
        st = _front(h, positions, norm_mix[l], w_in[l], kv_norm[l], idx_k_norm[l])
        y_att = _dsa_stage(st, w_uk[l], w_uv[l], rel_bias, b, s)
        y_ret = _ret_stage(st, b, s)
        h2, xn, top_idx, gates = _mid_stage(h, mem, y_att, y_ret, w_out[l], norm_cross[l], norm_mem[l], w_cq[l],
                                            w_ck[l], w_cv[l], w_co[l], norm_ffn[l], w_router[l], b_router[l])
        gain = norm_final if l == depth - 1 else None
        h = _moe_stage(h2, xn, top_idx, gates, w_gate_up[l], b_gate_up[l], w_down[l], b_down[l], gain).reshape(b, s, d)
    return h
```

```python
import functools
import math

import jax
import jax.numpy as jnp
import numpy as np
from jax import lax
from jax.experimental import pallas as pl
from jax.experimental.pallas import tpu as pltpu

F32 = jnp.float32
BF16 = jnp.bfloat16

D_MODEL = 2048
N_ATT_HEADS = 8
ATT_HEAD_DIM = 128
KV_LATENT = 256
IDX_HEADS = 16
IDX_DIM = 64
IDX_TOPK_MAX = 256
N_RET_HEADS = 4
RET_QK_DIM = 128
RET_V_DIM = 256
ROPE_BASE = 10000.0
N_BUCKETS = 32
MAX_DISTANCE = 128
N_CROSS_HEADS = 4
CROSS_HEAD_DIM = 128
N_EXPERTS = 32
TOP_K = 4
D_FF = D_MODEL
SWIGLU_LIMIT = 7.0
SWIGLU_ALPHA = 1.702
EPS = 1e-6

ATT_WIDTH = N_ATT_HEADS * ATT_HEAD_DIM
RET_WIDTH = N_RET_HEADS * RET_V_DIM
RET_QK_WIDTH = N_RET_HEADS * RET_QK_DIM
IDX_WIDTH = IDX_HEADS * IDX_DIM
CROSS_WIDTH = N_CROSS_HEADS * CROSS_HEAD_DIM

LANES = 128
VMEM_LIMIT = 56 * 1024 * 1024

COL_Q_ATT = 0
COL_Q_IDX = COL_Q_ATT + ATT_WIDTH
COL_V_RET = COL_Q_IDX + IDX_WIDTH
COL_G_RET = COL_V_RET + RET_WIDTH
COL_Q_RET = COL_G_RET + RET_WIDTH
COL_K_RET = COL_Q_RET + RET_QK_WIDTH
COL_C_KV = COL_K_RET + RET_QK_WIDTH
COL_KW = COL_C_KV + KV_LATENT
PROJ_TN = 512
PROJ_WIDTH = 5632

Q_BLOCK = 256
KEY_TILE = 512
KEY_PAD = KEY_TILE - Q_BLOCK
NEG_BIG = -1e30
LOG2E = math.log2(math.e)

RET_CHUNK = 256

MOE_ROWS = 512
IDX_RING = 3
MOE_TF = 1024


def _cparams(sem, vmem=VMEM_LIMIT):
    return pltpu.CompilerParams(dimension_semantics=sem, vmem_limit_bytes=vmem)


def _rms(x, g):
    return x * lax.rsqrt(jnp.mean(x * x, axis=-1, keepdims=True) + EPS) * g


ROW_PITCH = 20


def _store_rows(ref, x, pitch=None):
    n, d = x.shape
    w = d // LANES
    for a in range(w):
        ref[pl.ds(a, n, stride=pitch or w), :] = x[:, a * LANES:(a + 1) * LANES].astype(ref.dtype)


def _load_rows(ref, n, d, pitch=None):
    w = d // LANES
    return jnp.concatenate([ref[pl.ds(a, n, stride=pitch or w), :] for a in range(w)], axis=1)


def _proj_body(x_ref, g_ref, w_ref, o_ref, xn_ref):
    @pl.when(pl.program_id(1) == 0)
    def _():
        xn_ref[...] = _rms(x_ref[...], g_ref[...]).astype(BF16)

    o_ref[...] = jnp.dot(xn_ref[...], w_ref[...], preferred_element_type=F32).astype(o_ref.dtype)


def _proj(x2, gain, w_p, tm):
    t, d = x2.shape
    n = w_p.shape[1]
    return pl.pallas_call(
        _proj_body,
        out_shape=jax.ShapeDtypeStruct((t, n), BF16),
        grid=(t // tm, n // PROJ_TN),
        in_specs=[
            pl.BlockSpec((tm, d), lambda i, j: (i, 0)),
            pl.BlockSpec((1, d), lambda i, j: (0, 0)),
            pl.BlockSpec((d, PROJ_TN), lambda i, j: (0, j)),
        ],
        out_specs=pl.BlockSpec((tm, PROJ_TN), lambda i, j: (i, j)),
        scratch_shapes=[pltpu.VMEM((tm, d), BF16)],
        compiler_params=_cparams(("arbitrary", "arbitrary")),
        name="proj",
    )(x2, gain, w_p)


def _prep_body(c_ref, kw_ref, q_ref, k_ref, cos_ref, sin_ref, gkv_ref, gk_ref,
               c_out, kidx_out, widx_out, q_out, k_out):
    c_out[...] = _rms(c_ref[...].astype(F32), gkv_ref[...]).astype(BF16)

    kw = kw_ref[...].astype(F32)
    lane = lax.broadcasted_iota(jnp.int32, kw.shape, 1)
    is_k = lane < IDX_DIM
    kk = jnp.where(is_k, kw, 0.0)
    ms = jnp.sum(kk * kk, axis=-1, keepdims=True) * (1.0 / IDX_DIM)
    kidx_out[...] = (kk * lax.rsqrt(ms + EPS) * gk_ref[...]).astype(BF16)
    widx_out[...] = kw * (IDX_HEADS ** -0.5 * IDX_DIM ** -0.5)

    cos2 = cos_ref[...]
    sin2 = sin_ref[...]
    for h in range(N_RET_HEADS):
        sl = slice(h * RET_QK_DIM, (h + 1) * RET_QK_DIM)
        qh = q_ref[:, sl].astype(F32)
        kh = k_ref[:, sl].astype(F32)
        q_out[:, sl] = (qh * cos2 + pltpu.roll(qh, RET_QK_DIM // 2, 1) * sin2).astype(BF16)
        k_out[:, sl] = ((kh * cos2 + pltpu.roll(kh, RET_QK_DIM // 2, 1) * sin2)
                        * (RET_QK_DIM ** -0.5)).astype(BF16)


def _prep(proj, cos2, sin2, g_kv, g_k, tm):
    t = proj.shape[0]
    row = lambda i: (i, 0)
    return pl.pallas_call(
        _prep_body,
        out_shape=(
            jax.ShapeDtypeStruct((t, KV_LATENT), BF16),
            jax.ShapeDtypeStruct((t, LANES), BF16),
            jax.ShapeDtypeStruct((t, LANES), F32),
            jax.ShapeDtypeStruct((t, RET_QK_WIDTH), BF16),
            jax.ShapeDtypeStruct((t, RET_QK_WIDTH), BF16),
        ),
        grid=(t // tm,),
        in_specs=[
            pl.BlockSpec((tm, KV_LATENT), lambda i: (i, COL_C_KV // KV_LATENT)),
            pl.BlockSpec((tm, LANES), lambda i: (i, COL_KW // LANES)),
            pl.BlockSpec((tm, RET_QK_WIDTH), lambda i: (i, COL_Q_RET // RET_QK_WIDTH)),
            pl.BlockSpec((tm, RET_QK_WIDTH), lambda i: (i, COL_K_RET // RET_QK_WIDTH)),
            pl.BlockSpec((tm, LANES), row),
            pl.BlockSpec((tm, LANES), row),
            pl.BlockSpec((1, KV_LATENT), lambda i: (0, 0)),
            pl.BlockSpec((1, LANES), lambda i: (0, 0)),
        ],
        out_specs=(
            pl.BlockSpec((tm, KV_LATENT), row),
            pl.BlockSpec((tm, LANES), row),
            pl.BlockSpec((tm, LANES), row),
            pl.BlockSpec((tm, RET_QK_WIDTH), row),
            pl.BlockSpec((tm, RET_QK_WIDTH), row),
        ),
        compiler_params=_cparams(("arbitrary",)),
        name="prep",
    )(proj, proj, proj, proj, cos2, sin2, g_kv, g_k)


def _ret_body(q_ref, kt_ref, v_ref, g_ref, dec_ref, cross_ref, state_ref, cd_ref, o_ref, r_ref):
    @pl.when(pl.program_id(1) == 0)
    def _():
        r_ref[...] = jnp.zeros_like(r_ref)

    for h in range(N_RET_HEADS):
        qs = slice(h * RET_QK_DIM, (h + 1) * RET_QK_DIM)
        vs = slice(h * RET_V_DIM, (h + 1) * RET_V_DIM)
        q = q_ref[0, :, qs]
        kt = kt_ref[0, qs, :]
        v = v_ref[:, vs]
        r_old = r_ref[h]
        inner = jnp.dot(q, kt, preferred_element_type=F32) * dec_ref[h]
        qc = (q.astype(F32) * cross_ref[h]).astype(BF16)
        o = (jnp.dot(inner.astype(BF16), v, preferred_element_type=F32)
             + jnp.dot(qc, r_old.astype(BF16), preferred_element_type=F32))
        ks = (kt.astype(F32) * state_ref[h]).astype(BF16)
        r_ref[h] = r_old * cd_ref[h] + jnp.dot(ks, v, preferred_element_type=F32)
        o = o * lax.rsqrt(jnp.mean(o * o, axis=-1, keepdims=True) + EPS)
        g = g_ref[:, vs].astype(F32)
        o_ref[:, vs] = (g * jax.nn.sigmoid(g) * o).astype(o_ref.dtype)


def _retention(q_rot, kt_rot, proj, b, s):
    c = RET_CHUNK
    n = s // c
    hh = N_RET_HEADS
    log_gamma = jnp.log(1.0 - 2.0 ** (-5.0 - jnp.arange(hh, dtype=F32)))
    j = jnp.arange(c, dtype=F32)
    diff = j[:, None] - j[None, :]
    dec = jnp.where(diff >= 0, jnp.exp(log_gamma[:, None, None] * jnp.maximum(diff, 0.0)), 0.0)
    cross = jnp.broadcast_to(jnp.exp(log_gamma[:, None] * (j + 1.0))[:, :, None], (hh, c, RET_QK_DIM))
    state = jnp.exp(log_gamma[:, None] * (c - 1.0 - j))[:, None, :]
    cdec = jnp.broadcast_to(jnp.exp(log_gamma * c)[:, None, None], (hh, 1, RET_V_DIM))
    const = lambda bi, ci: (0, 0, 0)
    return pl.pallas_call(
        _ret_body,
        out_shape=jax.ShapeDtypeStruct((b * s, RET_WIDTH), BF16),
        grid=(b, n),
        in_specs=[
            pl.BlockSpec((1, c, RET_QK_WIDTH), lambda bi, ci: (bi, ci, 0)),
            pl.BlockSpec((1, RET_QK_WIDTH, c), lambda bi, ci: (bi, 0, ci)),
            pl.BlockSpec((c, RET_WIDTH), lambda bi, ci: (bi * n + ci, COL_V_RET // RET_WIDTH)),
            pl.BlockSpec((c, RET_WIDTH), lambda bi, ci: (bi * n + ci, COL_G_RET // RET_WIDTH)),
            pl.BlockSpec((hh, c, c), const),
            pl.BlockSpec((hh, c, RET_QK_DIM), const),
            pl.BlockSpec((hh, 1, c), const),
            pl.BlockSpec((hh, 1, RET_V_DIM), const),
        ],
        out_specs=pl.BlockSpec((c, RET_WIDTH), lambda bi, ci: (bi * n + ci, 0)),
        scratch_shapes=[pltpu.VMEM((hh, RET_QK_DIM, RET_V_DIM), F32)],
        compiler_params=_cparams(("arbitrary", "arbitrary")),
        name="ret",
    )(q_rot, kt_rot, proj, proj, dec, cross, state, cdec)


def _dsa_body(topk, kidx_ref, qit_ref, w_ref, c_ref, ct_ref, qt_ref, wuk_ref, wuvt_ref, bias_ref, y_ref,
              sc_ref, qlt_ref, acc_ref, m_ref, l_ref):
    qb = pl.program_id(1)
    tk = KEY_TILE
    half = tk // 2
    e_pad = (qb + 1) * Q_BLOCK + KEY_PAD
    n_tiles = e_pad // tk
    nh = N_ATT_HEADS
    lane_q = qb * Q_BLOCK + lax.broadcasted_iota(jnp.int32, (1, Q_BLOCK), 1)

    def tile_start(j):
        return pl.multiple_of(e_pad - (j + 1) * tk, Q_BLOCK)

    for h in range(nh):
        sl = slice(h * Q_BLOCK, (h + 1) * Q_BLOCK)
        hd = slice(h * ATT_HEAD_DIM, (h + 1) * ATT_HEAD_DIM)
        qlt_ref[:, sl] = (jnp.dot(wuk_ref[h], qt_ref[0, 0, hd, :], preferred_element_type=F32)
                          * (ATT_HEAD_DIM ** -0.5 * LOG2E)).astype(BF16)

    def score_tile(j, carry):
        lo8, hi8 = carry
        r0 = tile_start(j)
        for part in range(2):
            rs = pl.multiple_of(r0 + part * half, Q_BLOCK)
            z = jnp.dot(kidx_ref[0, pl.ds(rs, half), :], qit_ref[0, 0], preferred_element_type=F32)
            tot = jnp.zeros((half, Q_BLOCK), F32)
            for h in range(IDX_HEADS):
                sl = slice(h * Q_BLOCK, (h + 1) * Q_BLOCK)
                tot = tot + jnp.maximum(z[:, sl], 0.0) * w_ref[0, 0, :, sl]
            key = rs - KEY_PAD + lax.broadcasted_iota(jnp.int32, (half, Q_BLOCK), 0)
            ok = (key >= 0) & (key <= lane_q)
            val = jnp.where(ok, tot, NEG_BIG)
            sc_ref[pl.ds(rs, half), :] = val
            hi8 = jnp.maximum(hi8, jnp.max(val.reshape(half // 8, 8, Q_BLOCK), axis=0))
            lo8 = jnp.minimum(lo8, jnp.min(jnp.where(ok, tot, -NEG_BIG).reshape(half // 8, 8, Q_BLOCK), axis=0))
        return lo8, hi8
    lo8, hi8 = lax.fori_loop(0, n_tiles, score_tile,
                             (jnp.full((8, Q_BLOCK), -NEG_BIG, F32), jnp.full((8, Q_BLOCK), NEG_BIG, F32)))

    lo0 = jnp.min(lo8, axis=0, keepdims=True)
    hi0 = jnp.max(hi8, axis=0, keepdims=True)
    search = lane_q + 1 > topk

    n_chain = 4
    rows_chain = tk // n_chain

    def count_ge(x):
        def body(j, cnts):
            r0 = tile_start(j)
            out = []
            for k in range(n_chain):
                s = sc_ref[pl.ds(pl.multiple_of(r0 + k * rows_chain, rows_chain), rows_chain), :]
                ind = jnp.where(s >= x, 1.0, 0.0).reshape(rows_chain // 8, 8, Q_BLOCK)
                out.append(cnts[k] + jnp.sum(ind, axis=0))
            return tuple(out)
        cnts = lax.fori_loop(0, n_tiles, body, tuple(jnp.zeros((8, Q_BLOCK), F32) for _ in range(n_chain)))
        return jnp.sum((cnts[0] + cnts[1]) + (cnts[2] + cnts[3]), axis=0, keepdims=True)

    steps_per_check = 3

    def bis_cond(st):
        it, _, _, _, active = st
        return (it < 42) & (active > 0.0)

    def bis_step(_, st):
        lo, hi, done = st
        mid = lo + 0.5 * (hi - lo)
        cnt = count_ge(mid)
        ge = cnt >= topk
        open_ = done < 0.5
        lo = jnp.where(open_ & ge, mid, lo)
        hi = jnp.where(open_ & (~ge), mid, hi)
        done = jnp.where(cnt == topk, 1.0, done)
        return lo, hi, done

    def bis_body(st):
        it, lo, hi, done, _ = st
        lo, hi, done = lax.fori_loop(0, steps_per_check, bis_step, (lo, hi, done))
        return it + steps_per_check, lo, hi, done, jnp.max(1.0 - done)

    done0 = jnp.where(search, 0.0, 1.0)
    _, lo_f, _, _, _ = lax.while_loop(bis_cond, bis_body, (jnp.int32(0), lo0, hi0, done0, jnp.max(1.0 - done0)))
    thr = jnp.where(search, lo_f, 0.5 * NEG_BIG)

    m_ref[...] = jnp.full(m_ref.shape, NEG_BIG, F32)
    l_ref[...] = jnp.zeros(l_ref.shape, F32)
    acc_ref[...] = jnp.zeros(acc_ref.shape, F32)

    n_grp = 4
    hg = nh // n_grp

    def attend(segments):
        scores, ct_tiles = [], []
        for rs, n, with_bias in segments:
            c_tile = c_ref[0, pl.ds(rs, n), :]
            ct_tiles.append(ct_ref[0, :, pl.ds(rs, n)])
            pen = jnp.where(sc_ref[pl.ds(rs, n), :] >= thr, 0.0, NEG_BIG)
            pen = jnp.concatenate([pen] * hg, axis=1)
            for g in range(n_grp):
                ls = slice(g * hg * Q_BLOCK, (g + 1) * hg * Q_BLOCK)
                st = jnp.dot(c_tile, qlt_ref[:, ls], preferred_element_type=F32) + pen
                if with_bias:
                    st = st + bias_ref[:, ls]
                scores.append(st)
        for k, ct_tile in enumerate(ct_tiles):
            for g in range(n_grp):
                ls = slice(g * hg * Q_BLOCK, (g + 1) * hg * Q_BLOCK)
                st = scores[k * n_grp + g]
                m_old = m_ref[:, ls]
                m_new = jnp.maximum(m_old, jnp.max(st, axis=0, keepdims=True))
                alpha = jnp.exp2(m_old - m_new)
                p = jnp.exp2(st - m_new)
                l_ref[:, ls] = alpha * l_ref[:, ls] + jnp.sum(p, axis=0, keepdims=True)
                acc_ref[:, ls] = acc_ref[:, ls] * alpha + jnp.dot(ct_tile, p.astype(BF16),
                                                                  preferred_element_type=F32)
                m_ref[:, ls] = m_new

    r_first = tile_start(0)
    near = 2 * Q_BLOCK
    if near == tk:
        attend([(r_first, tk, True)])
    else:
        attend([(pl.multiple_of(r_first + tk - near, Q_BLOCK), near, True), (r_first, tk - near, False)])

    def attend_pair(i, carry):
        attend([(tile_start(2 * i + 1), tk, False), (tile_start(2 * i + 2), tk, False)])
        return carry
    n_rest = n_tiles - 1
    lax.fori_loop(0, n_rest // 2, attend_pair, 0)

    @pl.when(n_rest % 2 == 1)
    def _():
        attend([(tile_start(n_tiles - 1), tk, False)])

    ot = acc_ref[...] / l_ref[...]
    for h in range(nh):
        sl = slice(h * Q_BLOCK, (h + 1) * Q_BLOCK)
        y_ref[0, h * ATT_HEAD_DIM:(h + 1) * ATT_HEAD_DIM, :] = jnp.dot(
            wuvt_ref[h], ot[:, sl].astype(BF16), preferred_element_type=F32).astype(y_ref.dtype)


def _dsa(kidx_p, qit, widx_t, c_p, ct_p, qt, wuk_h, wuvt_h, bias_t, b, s, topk):
    nqb = s // Q_BLOCK
    sp = s + KEY_PAD
    nh = N_ATT_HEADS
    per_b = lambda bi, qi: (bi, 0, 0)
    per_q = lambda bi, qi: (bi, qi, 0, 0)
    c3 = lambda bi, qi: (0, 0, 0)
    return pl.pallas_call(
        functools.partial(_dsa_body, topk),
        out_shape=jax.ShapeDtypeStruct((b, ATT_WIDTH, s), BF16),
        grid=(b, nqb),
        in_specs=[
            pl.BlockSpec((1, sp, LANES), per_b),
            pl.BlockSpec((1, 1, LANES, IDX_HEADS * Q_BLOCK), per_q),
            pl.BlockSpec((1, 1, 1, IDX_HEADS * Q_BLOCK), per_q),
            pl.BlockSpec((1, sp, KV_LATENT), per_b),
            pl.BlockSpec((1, KV_LATENT, sp), per_b),
            pl.BlockSpec((1, 1, ATT_WIDTH, Q_BLOCK), per_q),
            pl.BlockSpec((nh, KV_LATENT, ATT_HEAD_DIM), c3),
            pl.BlockSpec((nh, ATT_HEAD_DIM, KV_LATENT), c3),
            pl.BlockSpec((2 * Q_BLOCK, nh * Q_BLOCK), lambda bi, qi: (0, 0)),
        ],
        out_specs=pl.BlockSpec((1, ATT_WIDTH, Q_BLOCK), lambda bi, qi: (bi, 0, qi)),
        scratch_shapes=[
            pltpu.VMEM((sp, Q_BLOCK), F32),
            pltpu.VMEM((KV_LATENT, nh * Q_BLOCK), BF16),
            pltpu.VMEM((KV_LATENT, nh * Q_BLOCK), F32),
            pltpu.VMEM((1, nh * Q_BLOCK), F32),
            pltpu.VMEM((1, nh * Q_BLOCK), F32),
        ],
        compiler_params=_cparams(("arbitrary", "arbitrary")),
        name="dsa",
    )(kidx_p, qit, widx_t, c_p, ct_p, qt, wuk_h, wuvt_h, bias_t)


def _t5_bucket(dist):
    n = jnp.maximum(dist, 0)
    max_exact = N_BUCKETS // 2
    nf = jnp.maximum(n, 1).astype(F32)
    large = max_exact + (jnp.log(nf / max_exact) / math.log(MAX_DISTANCE / max_exact)
                         * (N_BUCKETS - max_exact)).astype(jnp.int32)
    return jnp.where(n < max_exact, n, jnp.minimum(large, N_BUCKETS - 1))


def _dsa_stage(st, w_uk, w_uv, rel_bias, b, s):
    assert Q_BLOCK >= MAX_DISTANCE and 2 * Q_BLOCK <= KEY_TILE and s % Q_BLOCK == 0
    nqb = s // Q_BLOCK
    nh = N_ATT_HEADS
    proj = st["proj"].reshape(b, nqb, Q_BLOCK, PROJ_WIDTH)
    qt = jnp.swapaxes(proj[..., COL_Q_ATT:COL_Q_ATT + ATT_WIDTH], 2, 3)
    qi = proj[..., COL_Q_IDX:COL_Q_IDX + IDX_WIDTH].reshape(b, nqb, Q_BLOCK, IDX_HEADS, IDX_DIM)
    qit = jnp.transpose(qi, (0, 1, 4, 3, 2)).reshape(b, nqb, IDX_DIM, IDX_HEADS * Q_BLOCK)
    qit = jnp.pad(qit, ((0, 0), (0, 0), (0, LANES - IDX_DIM), (0, 0)))
    wi = st["widx"].reshape(b, nqb, Q_BLOCK, LANES)[..., IDX_DIM:IDX_DIM + IDX_HEADS]
    widx_t = jnp.swapaxes(wi, 2, 3).reshape(b, nqb, 1, IDX_HEADS * Q_BLOCK)
    front = ((0, 0), (KEY_PAD, 0), (0, 0))
    kidx_p = jnp.pad(st["kidx"].reshape(b, s, LANES), front)
    c_p = jnp.pad(st["c"].reshape(b, s, KV_LATENT), front)
    ct_p = jnp.swapaxes(c_p, 1, 2)
    wuk_h = jnp.transpose(w_uk, (1, 0, 2)).astype(BF16)
    wuvt_h = jnp.transpose(w_uv, (1, 2, 0)).astype(BF16)
    i = jnp.arange(2 * Q_BLOCK)[:, None]
    j = jnp.arange(Q_BLOCK)[None, :]
    rb = (rel_bias - rel_bias[N_BUCKETS - 1]) * LOG2E
    onehot = (_t5_bucket(j - i + Q_BLOCK)[None] == jnp.arange(N_BUCKETS)[:, None, None]).astype(F32)
    bias_t = jnp.einsum("bij,bh->ihj", onehot, rb, precision=lax.Precision.HIGHEST).reshape(2 * Q_BLOCK, nh * Q_BLOCK)
    topk = min(IDX_TOPK_MAX, s // 4)
    yt = _dsa(kidx_p, qit, widx_t, c_p, ct_p, qt, wuk_h, wuvt_h, bias_t, b, s, topk)
    return jnp.swapaxes(yt, 1, 2).reshape(b * s, ATT_WIDTH)


def _memkv_body(m_ref, g_ref, wk_ref, wv_ref, k_ref, v_ref):
    mn = _rms(m_ref[0], g_ref[...]).astype(BF16)
    k_ref[0] = jnp.dot(mn, wk_ref[...], preferred_element_type=F32).astype(BF16)
    v_ref[0] = jnp.dot(mn, wv_ref[...], preferred_element_type=F32).astype(BF16)


def _memkv(mem, gain, w_ck, w_cv):
    b, m, d = mem.shape
    w = w_ck.shape[1]
    const = lambda i: (0, 0)
    out = jax.ShapeDtypeStruct((b, m, w), BF16)
    return pl.pallas_call(
        _memkv_body,
        out_shape=(out, out),
        grid=(b,),
        in_specs=[
            pl.BlockSpec((1, m, d), lambda i: (i, 0, 0)),
            pl.BlockSpec((1, d), const),
            pl.BlockSpec((d, w), const),
            pl.BlockSpec((d, w), const),
        ],
        out_specs=(pl.BlockSpec((1, m, w), lambda i: (i, 0, 0)), pl.BlockSpec((1, m, w), lambda i: (i, 0, 0))),
        compiler_params=_cparams(("arbitrary",)),
        name="memkv",
    )(mem, gain, w_ck, w_cv)


def _mid_body(x_ref, ya_ref, yr_ref, woa_ref, wob_ref, gc_ref, wq_ref, km_ref, vm_ref, wo_ref,
              gf_ref, wr_ref, br_ref, h2_ref, xn_ref, idx_ref, gate_ref):
    h1 = (x_ref[...]
          + jnp.dot(ya_ref[...], woa_ref[...], preferred_element_type=F32)
          + jnp.dot(yr_ref[...], wob_ref[...], preferred_element_type=F32))
    hn = _rms(h1, gc_ref[...]).astype(BF16)
    q = jnp.dot(hn, wq_ref[...], preferred_element_type=F32).astype(BF16)
    heads = []
    for h in range(N_CROSS_HEADS):
        sl = slice(h * CROSS_HEAD_DIM, (h + 1) * CROSS_HEAD_DIM)
        s = lax.dot_general(q[:, sl], km_ref[0, :, sl], (((1,), (1,)), ((), ())),
                            preferred_element_type=F32) * (CROSS_HEAD_DIM ** -0.5)
        p = jnp.exp(s - jnp.max(s, axis=-1, keepdims=True))
        p = p / jnp.sum(p, axis=-1, keepdims=True)
        heads.append(jnp.dot(p.astype(BF16), vm_ref[0, :, sl], preferred_element_type=F32).astype(BF16))
    o = jnp.concatenate(heads, axis=-1)
    h2 = h1 + jnp.dot(o, wo_ref[...], preferred_element_type=F32)
    h2_ref[...] = h2

    xn = _rms(h2, gf_ref[...])
    _store_rows(xn_ref, xn)
    logit = lax.dot_general(wr_ref[...], xn.astype(BF16), (((1,), (1,)), ((), ())),
                            preferred_element_type=F32) + br_ref[...]
    eid = lax.broadcasted_iota(jnp.int32, logit.shape, 0)
    vals, ids = [], []
    for _ in range(TOP_K):
        mx = jnp.max(logit, axis=0, keepdims=True)
        sel = jnp.min(jnp.where(logit == mx, eid, N_EXPERTS), axis=0, keepdims=True)
        vals.append(mx)
        ids.append(sel)
        logit = jnp.where(eid == sel, -jnp.inf, logit)
    top = jnp.concatenate(vals, axis=0)
    e = jnp.exp(top - vals[0])
    gate_ref[...] = e / jnp.sum(e, axis=0, keepdims=True)
    idx_ref[...] = jnp.concatenate(ids, axis=0)


def _mid(x2, ya, yr, w_out_a, w_out_b, g_cross, w_cq, km, vm, w_co, g_ffn, w_rt, b_r, b, s, tm):
    t, d = x2.shape
    nt = s // tm
    row = lambda bi, i: (bi * nt + i, 0)
    const = lambda bi, i: (0, 0)
    once = pl.Buffered(1)
    m = km.shape[1]
    return pl.pallas_call(
        _mid_body,
        out_shape=(
            jax.ShapeDtypeStruct((t, d), F32),
            jax.ShapeDtypeStruct((t * (d // LANES), LANES), F32),
            jax.ShapeDtypeStruct((TOP_K, t), jnp.int32),
            jax.ShapeDtypeStruct((TOP_K, t), F32),
        ),
        grid=(b, nt),
        in_specs=[
            pl.BlockSpec((tm, d), row),
            pl.BlockSpec((tm, ATT_WIDTH), row),
            pl.BlockSpec((tm, RET_WIDTH), row),
            pl.BlockSpec((ATT_WIDTH, d), const, pipeline_mode=once),
            pl.BlockSpec((RET_WIDTH, d), const, pipeline_mode=once),
            pl.BlockSpec((1, d), const),
            pl.BlockSpec((d, CROSS_WIDTH), const, pipeline_mode=once),
            pl.BlockSpec((1, m, CROSS_WIDTH), lambda bi, i: (bi, 0, 0)),
            pl.BlockSpec((1, m, CROSS_WIDTH), lambda bi, i: (bi, 0, 0)),
            pl.BlockSpec((CROSS_WIDTH, d), const, pipeline_mode=once),
            pl.BlockSpec((1, d), const),
            pl.BlockSpec((N_EXPERTS, d), const),
            pl.BlockSpec((N_EXPERTS, 1), const),
        ],
        out_specs=(
            pl.BlockSpec((tm, d), row),
            pl.BlockSpec((tm * (d // LANES), LANES), row),
            pl.BlockSpec((TOP_K, tm), lambda bi, i: (0, bi * nt + i)),
            pl.BlockSpec((TOP_K, tm), lambda bi, i: (0, bi * nt + i)),
        ),
        compiler_params=_cparams(("arbitrary", "arbitrary")),
        name="mid",
    )(x2, ya, yr, w_out_a, w_out_b, g_cross, w_cq, km, vm, w_co, g_ffn, w_rt, b_r)


FP8 = jnp.float8_e4m3fn
FP8_MAX = 448.0
TINY = 1e-30


def _quant_body(w_ref, q_ref, s_ref):
    w = w_ref[0]
    amax = jnp.maximum(jnp.max(jnp.abs(w), axis=0, keepdims=True), TINY)
    q_ref[0] = (w * (FP8_MAX / amax)).astype(FP8)
    s_ref[0] = amax * (1.0 / FP8_MAX)


def _quant_weights(w, tn):
    e, k, n = w.shape
    return pl.pallas_call(
        _quant_body,
        out_shape=(jax.ShapeDtypeStruct((e, k, n), FP8), jax.ShapeDtypeStruct((e, 1, n), F32)),
        grid=(e, n // tn),
        in_specs=[pl.BlockSpec((1, k, tn), lambda i, j: (i, 0, j))],
        out_specs=(pl.BlockSpec((1, k, tn), lambda i, j: (i, 0, j)), pl.BlockSpec((1, 1, tn), lambda i, j: (i, 0, j))),
        compiler_params=_cparams(("arbitrary", "arbitrary")),
        name="quant",
    )(w)


def _quant_rows(x):
    amax = jnp.maximum(jnp.max(jnp.abs(x), axis=1, keepdims=True), TINY)
    return (x * (FP8_MAX / amax)).astype(FP8), amax * (1.0 / FP8_MAX)


def _experts_body(be_ref, nu_ref, idx_hbm, gate_ref, xn_hbm, wg_ref, wu_ref, sg_ref, su_ref, bg_ref, bu_ref,
                  wd_ref, sd_ref, bd_ref, y_hbm, idx_smem, xbuf, xb_ref, sx_ref, acc_ref, ybuf,
                  sem_idx, sem_g, sem_s):
    i = pl.program_id(0)
    j = pl.program_id(1)
    n_used = nu_ref[0]
    rows, d = xb_ref.shape
    w = d // LANES

    def idx_copy(blk, slot):
        return pltpu.make_async_copy(idx_hbm.at[blk],
                                     idx_smem.at[pl.ds(pl.multiple_of(slot * 2 * rows, 2 * rows), 2 * rows)],
                                     sem_idx.at[slot])

    def gather_start(islot, xslot):
        def body(r, carry):
            src = pl.multiple_of(idx_smem[islot * 2 * rows + r], w)
            pltpu.make_async_copy(xn_hbm.at[pl.ds(src, w)], xbuf.at[xslot, pl.ds(r * ROW_PITCH, w)],
                                  sem_g.at[xslot]).start()
            return carry
        lax.fori_loop(0, rows, body, 0, unroll=8)

    def gather_wait(xslot):
        pltpu.make_async_copy(xn_hbm.at[pl.ds(0, rows * w)], xbuf.at[xslot, pl.ds(0, rows * w)],
                              sem_g.at[xslot]).wait()

    def scatter_wait():
        pltpu.make_async_copy(ybuf.at[pl.ds(0, rows * w)], y_hbm.at[pl.ds(0, rows * w)], sem_s).wait()

    @pl.when(j == 0)
    def _():
        @pl.when(i == 0)
        def _():
            ybuf[...] = jnp.zeros_like(ybuf)
            dump = pltpu.make_async_copy(ybuf.at[pl.ds(0, rows * w)],
                                         y_hbm.at[pl.ds(y_hbm.shape[0] - rows * w, rows * w)], sem_s)
            dump.start()
            dump.wait()
            idx_copy(0, 0).start()

            @pl.when(n_used > 1)
            def _():
                idx_copy(1, 1).start()

            idx_copy(0, 0).wait()
            gather_start(0, 0)

        @pl.when(i + 1 < n_used)
        def _():
            idx_copy(i + 1, (i + 1) % IDX_RING).wait()

        @pl.when(i + 2 < n_used)
        def _():
            idx_copy(i + 2, (i + 2) % IDX_RING).start()

        @pl.when(i < n_used)
        def _():
            gather_wait(i % 2)
            xb_ref[...], sx_ref[...] = _quant_rows(_load_rows(xbuf.at[i % 2], rows, d, ROW_PITCH))
            acc_ref[...] = jnp.zeros_like(acc_ref)

    @pl.when(i < n_used)
    def _():
        nxt = jnp.minimum(i + 1, n_used - 1)
        per_step = rows // (D_FF // wd_ref.shape[1])
        gbase = (nxt % IDX_RING) * 2 * rows + j * per_step
        xnext = xbuf.at[(i + 1) % 2]
        for k in range(per_step):
            src = pl.multiple_of(idx_smem[gbase + k], w)
            pltpu.make_async_copy(xn_hbm.at[pl.ds(src, w)], xnext.at[pl.ds((j * per_step + k) * ROW_PITCH, w)],
                                  sem_g.at[(i + 1) % 2]).start()

        n_half = 2
        hr = rows // n_half
        for k in range(n_half):
            rs = slice(k * hr, (k + 1) * hr)
            xb = xb_ref[rs, :]
            sx = sx_ref[rs, :]
            g = jnp.dot(xb, wg_ref[0], preferred_element_type=F32) * sx * sg_ref[0] + bg_ref[0]
            u = jnp.dot(xb, wu_ref[0], preferred_element_type=F32) * sx * su_ref[0] + bu_ref[0]
            gt = jnp.minimum(g, SWIGLU_LIMIT)
            up = jnp.clip(u, -SWIGLU_LIMIT, SWIGLU_LIMIT)
            hq, sh = _quant_rows((up + 1.0) * (gt * jax.nn.sigmoid(SWIGLU_ALPHA * gt)))
            acc_ref[rs, :] += jnp.dot(hq, wd_ref[0], preferred_element_type=F32) * sh * sd_ref[0]

        @pl.when(j == pl.num_programs(1) - 1)
        def _():
            y = (acc_ref[...] + bd_ref[0]) * gate_ref[0]

            @pl.when(i > 0)
            def _():
                scatter_wait()

            _store_rows(ybuf, y, ROW_PITCH)
            islot = i % IDX_RING

            def body(r, carry):
                dst = pl.multiple_of(idx_smem[islot * 2 * rows + rows + r], w)
                pltpu.make_async_copy(ybuf.at[pl.ds(r * ROW_PITCH, w)], y_hbm.at[pl.ds(dst, w)], sem_s).start()
                return carry
            lax.fori_loop(0, rows, body, 0, unroll=8)

            @pl.when(i == n_used - 1)
            def _():
                scatter_wait()
                gather_wait((i + 1) % 2)


def _experts(block_expert, n_used, slot_idx, slot_gate, xn, w_gu, s_gu, b_gu, w_down, s_down, b_down):
    nb = slot_idx.shape[0]
    rows = slot_idx.shape[1] // 2
    e, d, ff2 = w_gu.shape
    w = d // LANES
    t = xn.shape[0] // w
    ff = ff2 // 2
    tf = MOE_TF
    nj = ff // tf
    a_rows = t * TOP_K + rows
    grid_spec = pltpu.PrefetchScalarGridSpec(
        num_scalar_prefetch=2,
        grid=(nb, nj),
        in_specs=[
            pl.BlockSpec(memory_space=pl.ANY),
            pl.BlockSpec((1, rows, 1), lambda i, j, be, nu: (i, 0, 0)),
            pl.BlockSpec(memory_space=pl.ANY),
            pl.BlockSpec((1, d, tf), lambda i, j, be, nu: (be[i], 0, j)),
            pl.BlockSpec((1, d, tf), lambda i, j, be, nu: (be[i], 0, j + nj)),
            pl.BlockSpec((1, 1, tf), lambda i, j, be, nu: (be[i], 0, j)),
            pl.BlockSpec((1, 1, tf), lambda i, j, be, nu: (be[i], 0, j + nj)),
            pl.BlockSpec((1, 1, tf), lambda i, j, be, nu: (be[i], 0, j)),
            pl.BlockSpec((1, 1, tf), lambda i, j, be, nu: (be[i], 0, j + nj)),
            pl.BlockSpec((1, tf, d), lambda i, j, be, nu: (be[i], j, 0)),
            pl.BlockSpec((1, 1, d), lambda i, j, be, nu: (be[i], 0, 0)),
            pl.BlockSpec((1, 1, d), lambda i, j, be, nu: (be[i], 0, 0)),
        ],
        out_specs=pl.BlockSpec(memory_space=pl.ANY),
        scratch_shapes=[
            pltpu.SMEM((IDX_RING * 2 * rows,), jnp.int32),
            pltpu.VMEM((2, rows * ROW_PITCH, LANES), F32),
            pltpu.VMEM((rows, d), FP8),
            pltpu.VMEM((rows, 1), F32),
            pltpu.VMEM((rows, d), F32),
            pltpu.VMEM((rows * ROW_PITCH, LANES), F32),
            pltpu.SemaphoreType.DMA((IDX_RING,)),
            pltpu.SemaphoreType.DMA((2,)),
            pltpu.SemaphoreType.DMA,
        ],
    )
    return pl.pallas_call(
        _experts_body,
        out_shape=jax.ShapeDtypeStruct((a_rows * w, LANES), F32),
        grid_spec=grid_spec,
        compiler_params=_cparams(("arbitrary", "arbitrary")),
        name="experts",
    )(block_expert, n_used, slot_idx, slot_gate, xn, w_gu, w_gu, s_gu, s_gu, b_gu, b_gu, w_down, s_down, b_down)


def _combine_body(final_norm, h_ref, y0_ref, y1_ref, y2_ref, y3_ref, g_ref, o_ref):
    n, d = h_ref.shape
    acc = h_ref[...] + _load_rows(y0_ref, n, d) + _load_rows(y1_ref, n, d) + _load_rows(y2_ref, n, d) \
        + _load_rows(y3_ref, n, d)
    o_ref[...] = _rms(acc, g_ref[...]) if final_norm else acc


def _combine(h2, y_tok, gain, tm):
    t, d = h2.shape
    w = d // LANES
    nt = t // tm
    final_norm = gain is not None
    if gain is None:
        gain = jnp.ones((d,), F32)
    gain = gain.reshape(1, d)
    plane = lambda k: pl.BlockSpec((tm * w, LANES), lambda i: (k * nt + i, 0))
    return pl.pallas_call(
        functools.partial(_combine_body, final_norm),
        out_shape=jax.ShapeDtypeStruct((t, d), F32),
        grid=(nt,),
        in_specs=[pl.BlockSpec((tm, d), lambda i: (i, 0))] + [plane(k) for k in range(TOP_K)]
                 + [pl.BlockSpec((1, d), lambda i: (0, 0))],
        out_specs=pl.BlockSpec((tm, d), lambda i: (i, 0)),
        compiler_params=_cparams(("arbitrary",)),
        name="combine",
    )(h2, y_tok, y_tok, y_tok, y_tok, gain)


def _routing(top_idx, gates, rows, row_width):
    k, t = top_idx.shape
    a = t * k
    flat_e = top_idx.reshape(-1)
    _, s_asg, s_gate = lax.sort((flat_e, jnp.arange(a, dtype=jnp.int32), gates.reshape(-1)), num_keys=1)
    experts = jnp.arange(N_EXPERTS, dtype=jnp.int32)
    counts = jnp.sum((flat_e[None, :] == experts[:, None]).astype(jnp.int32), axis=1)
    start = jnp.cumsum(counts) - counts
    padded = (counts + rows - 1) // rows * rows
    padded_end = jnp.cumsum(padded)
    padded_start = padded_end - padded
    nb = -(-a // rows) + N_EXPERTS
    n_used = (padded_end[-1] // rows).astype(jnp.int32)
    blk = jnp.arange(nb, dtype=jnp.int32)
    bexp = jnp.minimum(jnp.sum((padded_end[None, :] <= (blk * rows)[:, None]).astype(jnp.int32), axis=1),
                       N_EXPERTS - 1)
    bexp = jnp.where(blk < n_used, bexp, bexp[jnp.maximum(n_used - 1, 0)])
    off = blk * rows - padded_start[bexp]
    first = jnp.clip(start[bexp] + off, 0, a)
    nvalid = jnp.where(blk < n_used, jnp.clip(counts[bexp] - off, 0, rows), 0)
    r = jnp.arange(rows, dtype=jnp.int32)[None, :]
    pos = jnp.minimum(first[:, None] + r, a - 1)
    asg = s_asg[pos]
    gate = s_gate[pos]
    valid = r < nvalid[:, None]
    slot_src = jnp.where(valid, asg % t, 0)
    slot_dst = jnp.where(valid, asg, a + r)
    slot_gate = jnp.where(valid, gate, 0.0)
    slot_idx = jnp.concatenate([slot_src, slot_dst], axis=1) * row_width
    return bexp, n_used.reshape(1), slot_idx, slot_gate.reshape(nb, rows, 1)


def _moe_stage(h2, xn, top_idx, gates, w_gate_up, b_gate_up, w_down, b_down, norm_final):
    t, d = h2.shape
    bexp, n_used, slot_idx, slot_gate = _routing(top_idx, gates, MOE_ROWS, d // LANES)
    e = w_gate_up.shape[0]
    wq_gu, s_gu = _quant_weights(w_gate_up, MOE_TF)
    wq_d, s_d = _quant_weights(w_down, MOE_TF)
    y_tok = _experts(bexp, n_used, slot_idx, slot_gate, xn, wq_gu, s_gu, b_gate_up.reshape(e, 1, -1),
                     wq_d, s_d, b_down.reshape(e, 1, -1))
    return _combine(h2, y_tok, norm_final, min(256, t))


def _permute_w_in(w_in):
    sp = np.cumsum([0, ATT_WIDTH, KV_LATENT, IDX_WIDTH, IDX_DIM, IDX_HEADS, RET_QK_WIDTH, RET_QK_WIDTH,
                    RET_WIDTH, RET_WIDTH])
    piece = lambda i: w_in[:, int(sp[i]):int(sp[i + 1])]
    d = w_in.shape[0]
    cols = [piece(0), piece(2), piece(7), piece(8), piece(5), piece(6), piece(1), piece(3), piece(4)]
    used = sum(c.shape[1] for c in cols)
    cols.append(jnp.zeros((d, PROJ_WIDTH - used), w_in.dtype))
    return jnp.concatenate(cols, axis=1).astype(BF16)


def _rotary_tables(positions):
    half = RET_QK_DIM // 2
    inv = ROPE_BASE ** (-jnp.arange(0, RET_QK_DIM, 2, dtype=F32) / RET_QK_DIM)
    ang = positions.astype(F32).reshape(-1, 1) * inv
    cos, sin = jnp.cos(ang), jnp.sin(ang)
    return jnp.concatenate([cos, cos], axis=1), jnp.concatenate([-sin, sin], axis=1)


def _front(x, positions, norm_mix, w_in, kv_norm, idx_k_norm):
    b, s, d = x.shape
    t = b * s
    tm = min(1024, t)
    proj = _proj(x.reshape(t, d), norm_mix.reshape(1, d), _permute_w_in(w_in), tm)
    cos2, sin2 = _rotary_tables(positions)
    g_k = jnp.concatenate([idx_k_norm, jnp.zeros((LANES - IDX_DIM,), F32)]).reshape(1, LANES)
    c, kidx, widx, q_rot, k_rot = _prep(proj, cos2, sin2, kv_norm.reshape(1, KV_LATENT), g_k, tm)
    return dict(proj=proj, c=c, kidx=kidx, widx=widx, q_rot=q_rot, k_rot=k_rot)


def _mid_stage(x, mem, y_att, y_ret, w_out, norm_cross, norm_mem, w_cq, w_ck, w_cv, w_co, norm_ffn,
               w_router, b_router):
    b, s, d = x.shape
    km, vm = _memkv(mem, norm_mem.reshape(1, d), w_ck.astype(BF16), w_cv.astype(BF16))
    w_out_b16 = w_out.astype(BF16)
    return _mid(x.reshape(b * s, d), y_att, y_ret, w_out_b16[:ATT_WIDTH], w_out_b16[ATT_WIDTH:],
                norm_cross.reshape(1, d), w_cq.astype(BF16), km, vm, w_co.astype(BF16),
                norm_ffn.reshape(1, d), w_router.T.astype(BF16), b_router.reshape(N_EXPERTS, 1),
                b, s, min(512, s))


def _ret_stage(st, b, s):
    q_rot = st["q_rot"].reshape(b, s, RET_QK_WIDTH)
    kt_rot = jnp.swapaxes(st["k_rot"].reshape(b, s, RET_QK_WIDTH), 1, 2)
    return _retention(q_rot, kt_rot, st["proj"], b, s)


def kernel(x, mem, positions, norm_mix, w_in, kv_norm, idx_k_norm, w_uk, w_uv, rel_bias, w_out, norm_cross,
           norm_mem, w_cq, w_ck, w_cv, w_co, norm_ffn, w_router, b_router, w_gate_up, b_gate_up, w_down, b_down,
           norm_final):
    b, s, d = x.shape
    depth = norm_mix.shape[0]
    h = x
    for l in range(depth):
        st = _front(h, positions, norm_mix[l], w_in[l], kv_norm[l], idx_k_norm[l])
        y_att = _dsa_stage(st, w_uk[l], w_uv[l], rel_bias, b, s)
        y_ret = _ret_stage(st, b, s)
        h2, xn, top_idx, gates = _mid_stage(h, mem, y_att, y_ret, w_out[l], norm_cross[l], norm_mem[l], w_cq[l],
                                            w_ck[l], w_cv[l], w_co[l], norm_ffn[l], w_router[l], b_router[l])
        gain = norm_final if l == depth - 1 else None
        h = _moe_stage(h2, xn, top_idx, gates, w_gate_up[l], b_gate_up[l], w_down[l], b_down[l], gain).reshape(b, s, d)
    return h
```

```python
import functools
import math

import jax
import jax.numpy as jnp
import numpy as np
from jax import lax
from jax.experimental import pallas as pl
from jax.experimental.pallas import tpu as pltpu

F32 = jnp.float32
BF16 = jnp.bfloat16

D_MODEL = 2048
N_ATT_HEADS = 8
ATT_HEAD_DIM = 128
KV_LATENT = 256
IDX_HEADS = 16
IDX_DIM = 64
IDX_TOPK_MAX = 256
N_RET_HEADS = 4
RET_QK_DIM = 128
RET_V_DIM = 256
ROPE_BASE = 10000.0
N_BUCKETS = 32
MAX_DISTANCE = 128
N_CROSS_HEADS = 4
CROSS_HEAD_DIM = 128
N_EXPERTS = 32
TOP_K = 4
D_FF = D_MODEL
SWIGLU_LIMIT = 7.0
SWIGLU_ALPHA = 1.702
EPS = 1e-6

ATT_WIDTH = N_ATT_HEADS * ATT_HEAD_DIM
RET_WIDTH = N_RET_HEADS * RET_V_DIM
RET_QK_WIDTH = N_RET_HEADS * RET_QK_DIM
IDX_WIDTH = IDX_HEADS * IDX_DIM
CROSS_WIDTH = N_CROSS_HEADS * CROSS_HEAD_DIM

LANES = 128
VMEM_LIMIT = 56 * 1024 * 1024

COL_Q_ATT = 0
COL_Q_IDX = COL_Q_ATT + ATT_WIDTH
COL_V_RET = COL_Q_IDX + IDX_WIDTH
COL_G_RET = COL_V_RET + RET_WIDTH
COL_Q_RET = COL_G_RET + RET_WIDTH
COL_K_RET = COL_Q_RET + RET_QK_WIDTH
COL_C_KV = COL_K_RET + RET_QK_WIDTH
COL_KW = COL_C_KV + KV_LATENT
PROJ_TN = 512
PROJ_WIDTH = 5632

Q_BLOCK = 256
KEY_TILE = 512
KEY_PAD = KEY_TILE - Q_BLOCK
NEG_BIG = -1e30
LOG2E = math.log2(math.e)

RET_CHUNK = 256

MOE_ROWS = 512
IDX_RING = 3
MOE_TF = 1024


def _cparams(sem, vmem=VMEM_LIMIT):
    return pltpu.CompilerParams(dimension_semantics=sem, vmem_limit_bytes=vmem)


def _rms(x, g):
    return x * lax.rsqrt(jnp.mean(x * x, axis=-1, keepdims=True) + EPS) * g


ROW_PITCH = 20


def _store_rows(ref, x, pitch=None):
    n, d = x.shape
    w = d // LANES
    for a in range(w):
        ref[pl.ds(a, n, stride=pitch or w), :] = x[:, a * LANES:(a + 1) * LANES].astype(ref.dtype)


def _load_rows(ref, n, d, pitch=None):
    w = d // LANES
    return jnp.concatenate([ref[pl.ds(a, n, stride=pitch or w), :] for a in range(w)], axis=1)


def _proj_body(x_ref, g_ref, w_ref, o_ref, xn_ref):
    @pl.when(pl.program_id(1) == 0)
    def _():
        xn_ref[...] = _rms(x_ref[...], g_ref[...]).astype(BF16)

    o_ref[...] = jnp.dot(xn_ref[...], w_ref[...], preferred_element_type=F32).astype(o_ref.dtype)


def _proj(x2, gain, w_p, tm):
    t, d = x2.shape
    n = w_p.shape[1]
    return pl.pallas_call(
        _proj_body,
        out_shape=jax.ShapeDtypeStruct((t, n), BF16),
        grid=(t // tm, n // PROJ_TN),
        in_specs=[
            pl.BlockSpec((tm, d), lambda i, j: (i, 0)),
            pl.BlockSpec((1, d), lambda i, j: (0, 0)),
            pl.BlockSpec((d, PROJ_TN), lambda i, j: (0, j)),
        ],
        out_specs=pl.BlockSpec((tm, PROJ_TN), lambda i, j: (i, j)),
        scratch_shapes=[pltpu.VMEM((tm, d), BF16)],
        compiler_params=_cparams(("arbitrary", "arbitrary")),
        name="proj",
    )(x2, gain, w_p)


def _prep_body(c_ref, kw_ref, q_ref, k_ref, cos_ref, sin_ref, gkv_ref, gk_ref,
               c_out, kidx_out, widx_out, q_out, k_out):
    c_out[...] = _rms(c_ref[...].astype(F32), gkv_ref[...]).astype(BF16)

    kw = kw_ref[...].astype(F32)
    lane = lax.broadcasted_iota(jnp.int32, kw.shape, 1)
    is_k = lane < IDX_DIM
    kk = jnp.where(is_k, kw, 0.0)
    ms = jnp.sum(kk * kk, axis=-1, keepdims=True) * (1.0 / IDX_DIM)
    kidx_out[...] = (kk * lax.rsqrt(ms + EPS) * gk_ref[...]).astype(BF16)
    widx_out[...] = kw * (IDX_HEADS ** -0.5 * IDX_DIM ** -0.5)

    cos2 = cos_ref[...]
    sin2 = sin_ref[...]
    for h in range(N_RET_HEADS):
        sl = slice(h * RET_QK_DIM, (h + 1) * RET_QK_DIM)
        qh = q_ref[:, sl].astype(F32)
        kh = k_ref[:, sl].astype(F32)
        q_out[:, sl] = (qh * cos2 + pltpu.roll(qh, RET_QK_DIM // 2, 1) * sin2).astype(BF16)
        k_out[:, sl] = ((kh * cos2 + pltpu.roll(kh, RET_QK_DIM // 2, 1) * sin2)
                        * (RET_QK_DIM ** -0.5)).astype(BF16)


def _prep(proj, cos2, sin2, g_kv, g_k, tm):
    t = proj.shape[0]
    row = lambda i: (i, 0)
    return pl.pallas_call(
        _prep_body,
        out_shape=(
            jax.ShapeDtypeStruct((t, KV_LATENT), BF16),
            jax.ShapeDtypeStruct((t, LANES), BF16),
            jax.ShapeDtypeStruct((t, LANES), F32),
            jax.ShapeDtypeStruct((t, RET_QK_WIDTH), BF16),
            jax.ShapeDtypeStruct((t, RET_QK_WIDTH), BF16),
        ),
        grid=(t // tm,),
        in_specs=[
            pl.BlockSpec((tm, KV_LATENT), lambda i: (i, COL_C_KV // KV_LATENT)),
            pl.BlockSpec((tm, LANES), lambda i: (i, COL_KW // LANES)),
            pl.BlockSpec((tm, RET_QK_WIDTH), lambda i: (i, COL_Q_RET // RET_QK_WIDTH)),
            pl.BlockSpec((tm, RET_QK_WIDTH), lambda i: (i, COL_K_RET // RET_QK_WIDTH)),
            pl.BlockSpec((tm, LANES), row),
            pl.BlockSpec((tm, LANES), row),
            pl.BlockSpec((1, KV_LATENT), lambda i: (0, 0)),
            pl.BlockSpec((1, LANES), lambda i: (0, 0)),
        ],
        out_specs=(
            pl.BlockSpec((tm, KV_LATENT), row),
            pl.BlockSpec((tm, LANES), row),
            pl.BlockSpec((tm, LANES), row),
            pl.BlockSpec((tm, RET_QK_WIDTH), row),
            pl.BlockSpec((tm, RET_QK_WIDTH), row),
        ),
        compiler_params=_cparams(("arbitrary",)),
        name="prep",
    )(proj, proj, proj, proj, cos2, sin2, g_kv, g_k)


def _ret_body(q_ref, kt_ref, v_ref, g_ref, dec_ref, cross_ref, state_ref, cd_ref, o_ref, r_ref):
    @pl.when(pl.program_id(1) == 0)
    def _():
        r_ref[...] = jnp.zeros_like(r_ref)

    for h in range(N_RET_HEADS):
        qs = slice(h * RET_QK_DIM, (h + 1) * RET_QK_DIM)
        vs = slice(h * RET_V_DIM, (h + 1) * RET_V_DIM)
        q = q_ref[0, :, qs]
        kt = kt_ref[0, qs, :]
        v = v_ref[:, vs]
        r_old = r_ref[h]
        inner = jnp.dot(q, kt, preferred_element_type=F32) * dec_ref[h]
        qc = (q.astype(F32) * cross_ref[h]).astype(BF16)
        o = (jnp.dot(inner.astype(BF16), v, preferred_element_type=F32)
             + jnp.dot(qc, r_old.astype(BF16), preferred_element_type=F32))
        ks = (kt.astype(F32) * state_ref[h]).astype(BF16)
        r_ref[h] = r_old * cd_ref[h] + jnp.dot(ks, v, preferred_element_type=F32)
        o = o * lax.rsqrt(jnp.mean(o * o, axis=-1, keepdims=True) + EPS)
        g = g_ref[:, vs].astype(F32)
        o_ref[:, vs] = (g * jax.nn.sigmoid(g) * o).astype(o_ref.dtype)


def _retention(q_rot, kt_rot, proj, b, s):
    c = RET_CHUNK
    n = s // c
    hh = N_RET_HEADS
    log_gamma = jnp.log(1.0 - 2.0 ** (-5.0 - jnp.arange(hh, dtype=F32)))
    j = jnp.arange(c, dtype=F32)
    diff = j[:, None] - j[None, :]
    dec = jnp.where(diff >= 0, jnp.exp(log_gamma[:, None, None] * jnp.maximum(diff, 0.0)), 0.0)
    cross = jnp.broadcast_to(jnp.exp(log_gamma[:, None] * (j + 1.0))[:, :, None], (hh, c, RET_QK_DIM))
    state = jnp.exp(log_gamma[:, None] * (c - 1.0 - j))[:, None, :]
    cdec = jnp.broadcast_to(jnp.exp(log_gamma * c)[:, None, None], (hh, 1, RET_V_DIM))
    const = lambda bi, ci: (0, 0, 0)
    return pl.pallas_call(
        _ret_body,
        out_shape=jax.ShapeDtypeStruct((b * s, RET_WIDTH), BF16),
        grid=(b, n),
        in_specs=[
            pl.BlockSpec((1, c, RET_QK_WIDTH), lambda bi, ci: (bi, ci, 0)),
            pl.BlockSpec((1, RET_QK_WIDTH, c), lambda bi, ci: (bi, 0, ci)),
            pl.BlockSpec((c, RET_WIDTH), lambda bi, ci: (bi * n + ci, COL_V_RET // RET_WIDTH)),
            pl.BlockSpec((c, RET_WIDTH), lambda bi, ci: (bi * n + ci, COL_G_RET // RET_WIDTH)),
            pl.BlockSpec((hh, c, c), const),
            pl.BlockSpec((hh, c, RET_QK_DIM), const),
            pl.BlockSpec((hh, 1, c), const),
            pl.BlockSpec((hh, 1, RET_V_DIM), const),
        ],
        out_specs=pl.BlockSpec((c, RET_WIDTH), lambda bi, ci: (bi * n + ci, 0)),
        scratch_shapes=[pltpu.VMEM((hh, RET_QK_DIM, RET_V_DIM), F32)],
        compiler_params=_cparams(("arbitrary", "arbitrary")),
        name="ret",
    )(q_rot, kt_rot, proj, proj, dec, cross, state, cdec)


def _dsa_body(topk, kidx_ref, qit_ref, w_ref, c_ref, ct_ref, qt_ref, wuk_ref, wuvt_ref, bias_ref, y_ref,
              sc_ref, qlt_ref, acc_ref, m_ref, l_ref, st_ref):
    qb = pl.program_id(1)
    tk = KEY_TILE
    half = tk // 2
    e_pad = (qb + 1) * Q_BLOCK + KEY_PAD
    n_tiles = e_pad // tk
    nh = N_ATT_HEADS
    lane_q = qb * Q_BLOCK + lax.broadcasted_iota(jnp.int32, (1, Q_BLOCK), 1)

    def tile_start(j):
        return pl.multiple_of(e_pad - (j + 1) * tk, Q_BLOCK)

    for h in range(nh):
        sl = slice(h * Q_BLOCK, (h + 1) * Q_BLOCK)
        hd = slice(h * ATT_HEAD_DIM, (h + 1) * ATT_HEAD_DIM)
        qlt_ref[:, sl] = (jnp.dot(wuk_ref[h], qt_ref[0, 0, hd, :], preferred_element_type=F32)
                          * (ATT_HEAD_DIM ** -0.5 * LOG2E)).astype(BF16)

    def score_tile(j, carry):
        lo8, hi8 = carry
        r0 = tile_start(j)
        for part in range(2):
            rs = pl.multiple_of(r0 + part * half, Q_BLOCK)
            z = jnp.dot(kidx_ref[0, pl.ds(rs, half), :], qit_ref[0, 0], preferred_element_type=F32)
            tot = jnp.zeros((half, Q_BLOCK), F32)
            for h in range(IDX_HEADS):
                sl = slice(h * Q_BLOCK, (h + 1) * Q_BLOCK)
                tot = tot + jnp.maximum(z[:, sl], 0.0) * w_ref[0, 0, :, sl]
            key = rs - KEY_PAD + lax.broadcasted_iota(jnp.int32, (half, Q_BLOCK), 0)
            ok = (key >= 0) & (key <= lane_q)
            val = jnp.where(ok, tot, NEG_BIG)
            sc_ref[pl.ds(rs, half), :] = val
            hi8 = jnp.maximum(hi8, jnp.max(val.reshape(half // 8, 8, Q_BLOCK), axis=0))
            lo8 = jnp.minimum(lo8, jnp.min(jnp.where(ok, tot, -NEG_BIG).reshape(half // 8, 8, Q_BLOCK), axis=0))
        return lo8, hi8
    lo8, hi8 = lax.fori_loop(0, n_tiles, score_tile,
                             (jnp.full((8, Q_BLOCK), -NEG_BIG, F32), jnp.full((8, Q_BLOCK), NEG_BIG, F32)))

    lo0 = jnp.min(lo8, axis=0, keepdims=True)
    hi0 = jnp.max(hi8, axis=0, keepdims=True)
    search = lane_q + 1 > topk

    n_chain = 4
    rows_chain = tk // n_chain

    def count_ge(x):
        def body(j, cnts):
            r0 = tile_start(j)
            out = []
            for k in range(n_chain):
                s = sc_ref[pl.ds(pl.multiple_of(r0 + k * rows_chain, rows_chain), rows_chain), :]
                ind = jnp.where(s >= x, 1.0, 0.0).reshape(rows_chain // 8, 8, Q_BLOCK)
                out.append(cnts[k] + jnp.sum(ind, axis=0))
            return tuple(out)
        cnts = lax.fori_loop(0, n_tiles, body, tuple(jnp.zeros((8, Q_BLOCK), F32) for _ in range(n_chain)))
        return jnp.sum((cnts[0] + cnts[1]) + (cnts[2] + cnts[3]), axis=0, keepdims=True)

    steps_per_check = 3

    def bis_cond(st):
        it, _, _, _, active = st
        return (it < 42) & (active > 0.0)

    def bis_step(_, st):
        lo, hi, done = st
        mid = lo + 0.5 * (hi - lo)
        cnt = count_ge(mid)
        ge = cnt >= topk
        open_ = done < 0.5
        lo = jnp.where(open_ & ge, mid, lo)
        hi = jnp.where(open_ & (~ge), mid, hi)
        done = jnp.where(cnt == topk, 1.0, done)
        return lo, hi, done

    def bis_body(st):
        it, lo, hi, done, _ = st
        lo, hi, done = lax.fori_loop(0, steps_per_check, bis_step, (lo, hi, done))
        return it + steps_per_check, lo, hi, done, jnp.max(1.0 - done)

    done0 = jnp.where(search, 0.0, 1.0)
    _, lo_f, _, _, _ = lax.while_loop(bis_cond, bis_body, (jnp.int32(0), lo0, hi0, done0, jnp.max(1.0 - done0)))
    thr = jnp.where(search, lo_f, 0.5 * NEG_BIG)

    m_ref[...] = jnp.full(m_ref.shape, NEG_BIG, F32)
    l_ref[...] = jnp.zeros(l_ref.shape, F32)
    acc_ref[...] = jnp.zeros(acc_ref.shape, F32)

    n_grp = 8
    hg = nh // n_grp

    def attend(segments):
        ct_tiles = []
        for k, (rs, n, with_bias) in enumerate(segments):
            c_tile = c_ref[0, pl.ds(rs, n), :]
            ct_tiles.append((n, ct_ref[0, :, pl.ds(rs, n)]))
            pen = jnp.where(sc_ref[pl.ds(rs, n), :] >= thr, 0.0, NEG_BIG)
            pen = jnp.concatenate([pen] * hg, axis=1)
            for g in range(n_grp):
                ls = slice(g * hg * Q_BLOCK, (g + 1) * hg * Q_BLOCK)
                st = jnp.dot(c_tile, qlt_ref[:, ls], preferred_element_type=F32) + pen
                if with_bias:
                    st = st + bias_ref[:, ls]
                st_ref[k, 0:n, ls] = st
        for k, (n, ct_tile) in enumerate(ct_tiles):
            for g in range(n_grp):
                ls = slice(g * hg * Q_BLOCK, (g + 1) * hg * Q_BLOCK)
                st = st_ref[k, 0:n, ls]
                m_old = m_ref[:, ls]
                m_new = jnp.maximum(m_old, jnp.max(st, axis=0, keepdims=True))
                alpha = jnp.exp2(m_old - m_new)
                p = jnp.exp2(st - m_new)
                l_ref[:, ls] = alpha * l_ref[:, ls] + jnp.sum(p, axis=0, keepdims=True)
                acc_ref[:, ls] = acc_ref[:, ls] * alpha + jnp.dot(ct_tile, p.astype(BF16),
                                                                  preferred_element_type=F32)
                m_ref[:, ls] = m_new

    r_first = tile_start(0)
    near = 2 * Q_BLOCK
    if near == tk:
        attend([(r_first, tk, True)])
    else:
        attend([(pl.multiple_of(r_first + tk - near, Q_BLOCK), near, True), (r_first, tk - near, False)])

    def attend_pair(i, carry):
        attend([(tile_start(2 * i + 1), tk, False), (tile_start(2 * i + 2), tk, False)])
        return carry
    n_rest = n_tiles - 1
    lax.fori_loop(0, n_rest // 2, attend_pair, 0)

    @pl.when(n_rest % 2 == 1)
    def _():
        attend([(tile_start(n_tiles - 1), tk, False)])

    ot = acc_ref[...] / l_ref[...]
    for h in range(nh):
        sl = slice(h * Q_BLOCK, (h + 1) * Q_BLOCK)
        y_ref[0, h * ATT_HEAD_DIM:(h + 1) * ATT_HEAD_DIM, :] = jnp.dot(
            wuvt_ref[h], ot[:, sl].astype(BF16), preferred_element_type=F32).astype(y_ref.dtype)


def _dsa(kidx_p, qit, widx_t, c_p, ct_p, qt, wuk_h, wuvt_h, bias_t, b, s, topk):
    nqb = s // Q_BLOCK
    sp = s + KEY_PAD
    nh = N_ATT_HEADS
    per_b = lambda bi, qi: (bi, 0, 0)
    per_q = lambda bi, qi: (bi, qi, 0, 0)
    c3 = lambda bi, qi: (0, 0, 0)
    return pl.pallas_call(
        functools.partial(_dsa_body, topk),
        out_shape=jax.ShapeDtypeStruct((b, ATT_WIDTH, s), BF16),
        grid=(b, nqb),
        in_specs=[
            pl.BlockSpec((1, sp, LANES), per_b),
            pl.BlockSpec((1, 1, LANES, IDX_HEADS * Q_BLOCK), per_q),
            pl.BlockSpec((1, 1, 1, IDX_HEADS * Q_BLOCK), per_q),
            pl.BlockSpec((1, sp, KV_LATENT), per_b),
            pl.BlockSpec((1, KV_LATENT, sp), per_b),
            pl.BlockSpec((1, 1, ATT_WIDTH, Q_BLOCK), per_q),
            pl.BlockSpec((nh, KV_LATENT, ATT_HEAD_DIM), c3),
            pl.BlockSpec((nh, ATT_HEAD_DIM, KV_LATENT), c3),
            pl.BlockSpec((2 * Q_BLOCK, nh * Q_BLOCK), lambda bi, qi: (0, 0)),
        ],
        out_specs=pl.BlockSpec((1, ATT_WIDTH, Q_BLOCK), lambda bi, qi: (bi, 0, qi)),
        scratch_shapes=[
            pltpu.VMEM((sp, Q_BLOCK), F32),
            pltpu.VMEM((KV_LATENT, nh * Q_BLOCK), BF16),
            pltpu.VMEM((KV_LATENT, nh * Q_BLOCK), F32),
            pltpu.VMEM((1, nh * Q_BLOCK), F32),
            pltpu.VMEM((1, nh * Q_BLOCK), F32),
            pltpu.VMEM((2, KEY_TILE, nh * Q_BLOCK), F32),
        ],
        compiler_params=_cparams(("arbitrary", "arbitrary")),
        name="dsa",
    )(kidx_p, qit, widx_t, c_p, ct_p, qt, wuk_h, wuvt_h, bias_t)


def _t5_bucket(dist):
    n = jnp.maximum(dist, 0)
    max_exact = N_BUCKETS // 2
    nf = jnp.maximum(n, 1).astype(F32)
    large = max_exact + (jnp.log(nf / max_exact) / math.log(MAX_DISTANCE / max_exact)
                         * (N_BUCKETS - max_exact)).astype(jnp.int32)
    return jnp.where(n < max_exact, n, jnp.minimum(large, N_BUCKETS - 1))


def _dsa_stage(st, w_uk, w_uv, rel_bias, b, s):
    assert Q_BLOCK >= MAX_DISTANCE and 2 * Q_BLOCK <= KEY_TILE and s % Q_BLOCK == 0
    nqb = s // Q_BLOCK
    nh = N_ATT_HEADS
    proj = st["proj"].reshape(b, nqb, Q_BLOCK, PROJ_WIDTH)
    qt = jnp.swapaxes(proj[..., COL_Q_ATT:COL_Q_ATT + ATT_WIDTH], 2, 3)
    qi = proj[..., COL_Q_IDX:COL_Q_IDX + IDX_WIDTH].reshape(b, nqb, Q_BLOCK, IDX_HEADS, IDX_DIM)
    qit = jnp.transpose(qi, (0, 1, 4, 3, 2)).reshape(b, nqb, IDX_DIM, IDX_HEADS * Q_BLOCK)
    qit = jnp.pad(qit, ((0, 0), (0, 0), (0, LANES - IDX_DIM), (0, 0)))
    wi = st["widx"].reshape(b, nqb, Q_BLOCK, LANES)[..., IDX_DIM:IDX_DIM + IDX_HEADS]
    widx_t = jnp.swapaxes(wi, 2, 3).reshape(b, nqb, 1, IDX_HEADS * Q_BLOCK)
    front = ((0, 0), (KEY_PAD, 0), (0, 0))
    kidx_p = jnp.pad(st["kidx"].reshape(b, s, LANES), front)
    c_p = jnp.pad(st["c"].reshape(b, s, KV_LATENT), front)
    ct_p = jnp.swapaxes(c_p, 1, 2)
    wuk_h = jnp.transpose(w_uk, (1, 0, 2)).astype(BF16)
    wuvt_h = jnp.transpose(w_uv, (1, 2, 0)).astype(BF16)
    i = jnp.arange(2 * Q_BLOCK)[:, None]
    j = jnp.arange(Q_BLOCK)[None, :]
    rb = (rel_bias - rel_bias[N_BUCKETS - 1]) * LOG2E
    onehot = (_t5_bucket(j - i + Q_BLOCK)[None] == jnp.arange(N_BUCKETS)[:, None, None]).astype(F32)
    bias_t = jnp.einsum("bij,bh->ihj", onehot, rb, precision=lax.Precision.HIGHEST).reshape(2 * Q_BLOCK, nh * Q_BLOCK)
    topk = min(IDX_TOPK_MAX, s // 4)
    yt = _dsa(kidx_p, qit, widx_t, c_p, ct_p, qt, wuk_h, wuvt_h, bias_t, b, s, topk)
    return jnp.swapaxes(yt, 1, 2).reshape(b * s, ATT_WIDTH)


def _memkv_body(m_ref, g_ref, wk_ref, wv_ref, k_ref, v_ref):
    mn = _rms(m_ref[0], g_ref[...]).astype(BF16)
    k_ref[0] = jnp.dot(mn, wk_ref[...], preferred_element_type=F32).astype(BF16)
    v_ref[0] = jnp.dot(mn, wv_ref[...], preferred_element_type=F32).astype(BF16)


def _memkv(mem, gain, w_ck, w_cv):
    b, m, d = mem.shape
    w = w_ck.shape[1]
    const = lambda i: (0, 0)
    out = jax.ShapeDtypeStruct((b, m, w), BF16)
    return pl.pallas_call(
        _memkv_body,
        out_shape=(out, out),
        grid=(b,),
        in_specs=[
            pl.BlockSpec((1, m, d), lambda i: (i, 0, 0)),
            pl.BlockSpec((1, d), const),
            pl.BlockSpec((d, w), const),
            pl.BlockSpec((d, w), const),
        ],
        out_specs=(pl.BlockSpec((1, m, w), lambda i: (i, 0, 0)), pl.BlockSpec((1, m, w), lambda i: (i, 0, 0))),
        compiler_params=_cparams(("arbitrary",)),
        name="memkv",
    )(mem, gain, w_ck, w_cv)


def _mid_body(x_ref, ya_ref, yr_ref, woa_ref, wob_ref, gc_ref, wq_ref, km_ref, vm_ref, wo_ref,
              gf_ref, wr_ref, br_ref, h2_ref, xn_ref, idx_ref, gate_ref):
    h1 = (x_ref[...]
          + jnp.dot(ya_ref[...], woa_ref[...], preferred_element_type=F32)
          + jnp.dot(yr_ref[...], wob_ref[...], preferred_element_type=F32))
    hn = _rms(h1, gc_ref[...]).astype(BF16)
    q = jnp.dot(hn, wq_ref[...], preferred_element_type=F32).astype(BF16)
    heads = []
    for h in range(N_CROSS_HEADS):
        sl = slice(h * CROSS_HEAD_DIM, (h + 1) * CROSS_HEAD_DIM)
        s = lax.dot_general(q[:, sl], km_ref[0, :, sl], (((1,), (1,)), ((), ())),
                            preferred_element_type=F32) * (CROSS_HEAD_DIM ** -0.5)
        p = jnp.exp(s - jnp.max(s, axis=-1, keepdims=True))
        p = p / jnp.sum(p, axis=-1, keepdims=True)
        heads.append(jnp.dot(p.astype(BF16), vm_ref[0, :, sl], preferred_element_type=F32).astype(BF16))
    o = jnp.concatenate(heads, axis=-1)
    h2 = h1 + jnp.dot(o, wo_ref[...], preferred_element_type=F32)
    h2_ref[...] = h2

    xn = _rms(h2, gf_ref[...])
    _store_rows(xn_ref, xn)
    logit = lax.dot_general(wr_ref[...], xn.astype(BF16), (((1,), (1,)), ((), ())),
                            preferred_element_type=F32) + br_ref[...]
    eid = lax.broadcasted_iota(jnp.int32, logit.shape, 0)
    vals, ids = [], []
    for _ in range(TOP_K):
        mx = jnp.max(logit, axis=0, keepdims=True)
        sel = jnp.min(jnp.where(logit == mx, eid, N_EXPERTS), axis=0, keepdims=True)
        vals.append(mx)
        ids.append(sel)
        logit = jnp.where(eid == sel, -jnp.inf, logit)
    top = jnp.concatenate(vals, axis=0)
    e = jnp.exp(top - vals[0])
    gate_ref[...] = e / jnp.sum(e, axis=0, keepdims=True)
    idx_ref[...] = jnp.concatenate(ids, axis=0)


def _mid(x2, ya, yr, w_out_a, w_out_b, g_cross, w_cq, km, vm, w_co, g_ffn, w_rt, b_r, b, s, tm):
    t, d = x2.shape
    nt = s // tm
    row = lambda bi, i: (bi * nt + i, 0)
    const = lambda bi, i: (0, 0)
    once = pl.Buffered(1)
    m = km.shape[1]
    return pl.pallas_call(
        _mid_body,
        out_shape=(
            jax.ShapeDtypeStruct((t, d), F32),
            jax.ShapeDtypeStruct((t * (d // LANES), LANES), F32),
            jax.ShapeDtypeStruct((TOP_K, t), jnp.int32),
            jax.ShapeDtypeStruct((TOP_K, t), F32),
        ),
        grid=(b, nt),
        in_specs=[
            pl.BlockSpec((tm, d), row),
            pl.BlockSpec((tm, ATT_WIDTH), row),
            pl.BlockSpec((tm, RET_WIDTH), row),
            pl.BlockSpec((ATT_WIDTH, d), const, pipeline_mode=once),
            pl.BlockSpec((RET_WIDTH, d), const, pipeline_mode=once),
            pl.BlockSpec((1, d), const),
            pl.BlockSpec((d, CROSS_WIDTH), const, pipeline_mode=once),
            pl.BlockSpec((1, m, CROSS_WIDTH), lambda bi, i: (bi, 0, 0)),
            pl.BlockSpec((1, m, CROSS_WIDTH), lambda bi, i: (bi, 0, 0)),
            pl.BlockSpec((CROSS_WIDTH, d), const, pipeline_mode=once),
            pl.BlockSpec((1, d), const),
            pl.BlockSpec((N_EXPERTS, d), const),
            pl.BlockSpec((N_EXPERTS, 1), const),
        ],
        out_specs=(
            pl.BlockSpec((tm, d), row),
            pl.BlockSpec((tm * (d // LANES), LANES), row),
            pl.BlockSpec((TOP_K, tm), lambda bi, i: (0, bi * nt + i)),
            pl.BlockSpec((TOP_K, tm), lambda bi, i: (0, bi * nt + i)),
        ),
        compiler_params=_cparams(("arbitrary", "arbitrary")),
        name="mid",
    )(x2, ya, yr, w_out_a, w_out_b, g_cross, w_cq, km, vm, w_co, g_ffn, w_rt, b_r)


FP8 = jnp.float8_e4m3fn
FP8_MAX = 448.0
TINY = 1e-30


def _quant_body(w_ref, q_ref, s_ref):
    w = w_ref[0]
    amax = jnp.maximum(jnp.max(jnp.abs(w), axis=0, keepdims=True), TINY)
    q_ref[0] = (w * (FP8_MAX / amax)).astype(FP8)
    s_ref[0] = amax * (1.0 / FP8_MAX)


def _quant_weights(w, tn):
    e, k, n = w.shape
    return pl.pallas_call(
        _quant_body,
        out_shape=(jax.ShapeDtypeStruct((e, k, n), FP8), jax.ShapeDtypeStruct((e, 1, n), F32)),
        grid=(e, n // tn),
        in_specs=[pl.BlockSpec((1, k, tn), lambda i, j: (i, 0, j))],
        out_specs=(pl.BlockSpec((1, k, tn), lambda i, j: (i, 0, j)), pl.BlockSpec((1, 1, tn), lambda i, j: (i, 0, j))),
        compiler_params=_cparams(("arbitrary", "arbitrary")),
        name="quant",
    )(w)


def _quant_rows(x):
    amax = jnp.maximum(jnp.max(jnp.abs(x), axis=1, keepdims=True), TINY)
    return (x * (FP8_MAX / amax)).astype(FP8), amax * (1.0 / FP8_MAX)


def _experts_body(be_ref, nu_ref, idx_hbm, gate_ref, xn_hbm, wg_ref, wu_ref, sg_ref, su_ref, bg_ref, bu_ref,
                  wd_ref, sd_ref, bd_ref, y_hbm, idx_smem, xbuf, xb_ref, sx_ref, acc_ref, ybuf,
                  sem_idx, sem_g, sem_s):
    i = pl.program_id(0)
    j = pl.program_id(1)
    n_used = nu_ref[0]
    rows, d = xb_ref.shape
    w = d // LANES

    def idx_copy(blk, slot):
        return pltpu.make_async_copy(idx_hbm.at[blk],
                                     idx_smem.at[pl.ds(pl.multiple_of(slot * 2 * rows, 2 * rows), 2 * rows)],
                                     sem_idx.at[slot])

    def gather_start(islot, xslot):
        def body(r, carry):
            src = pl.multiple_of(idx_smem[islot * 2 * rows + r], w)
            pltpu.make_async_copy(xn_hbm.at[pl.ds(src, w)], xbuf.at[xslot, pl.ds(r * ROW_PITCH, w)],
                                  sem_g.at[xslot]).start()
            return carry
        lax.fori_loop(0, rows, body, 0, unroll=8)

    def gather_wait(xslot):
        pltpu.make_async_copy(xn_hbm.at[pl.ds(0, rows * w)], xbuf.at[xslot, pl.ds(0, rows * w)],
                              sem_g.at[xslot]).wait()

    def scatter_wait():
        pltpu.make_async_copy(ybuf.at[pl.ds(0, rows * w)], y_hbm.at[pl.ds(0, rows * w)], sem_s).wait()

    @pl.when(j == 0)
    def _():
        @pl.when(i == 0)
        def _():
            ybuf[...] = jnp.zeros_like(ybuf)
            dump = pltpu.make_async_copy(ybuf.at[pl.ds(0, rows * w)],
                                         y_hbm.at[pl.ds(y_hbm.shape[0] - rows * w, rows * w)], sem_s)
            dump.start()
            dump.wait()
            idx_copy(0, 0).start()

            @pl.when(n_used > 1)
            def _():
                idx_copy(1, 1).start()

            idx_copy(0, 0).wait()
            gather_start(0, 0)

        @pl.when(i + 1 < n_used)
        def _():
            idx_copy(i + 1, (i + 1) % IDX_RING).wait()

        @pl.when(i + 2 < n_used)
        def _():
            idx_copy(i + 2, (i + 2) % IDX_RING).start()

        @pl.when(i < n_used)
        def _():
            gather_wait(i % 2)
            xb_ref[...], sx_ref[...] = _quant_rows(_load_rows(xbuf.at[i % 2], rows, d, ROW_PITCH))
            acc_ref[...] = jnp.zeros_like(acc_ref)

    @pl.when(i < n_used)
    def _():
        nxt = jnp.minimum(i + 1, n_used - 1)
        per_step = rows // (D_FF // wd_ref.shape[1])
        gbase = (nxt % IDX_RING) * 2 * rows + j * per_step
        xnext = xbuf.at[(i + 1) % 2]
        for k in range(per_step):
            src = pl.multiple_of(idx_smem[gbase + k], w)
            pltpu.make_async_copy(xn_hbm.at[pl.ds(src, w)], xnext.at[pl.ds((j * per_step + k) * ROW_PITCH, w)],
                                  sem_g.at[(i + 1) % 2]).start()

        n_half = 2
        hr = rows // n_half
        for k in range(n_half):
            rs = slice(k * hr, (k + 1) * hr)
            xb = xb_ref[rs, :]
            sx = sx_ref[rs, :]
            g = jnp.dot(xb, wg_ref[0], preferred_element_type=F32) * sx * sg_ref[0] + bg_ref[0]
            u = jnp.dot(xb, wu_ref[0], preferred_element_type=F32) * sx * su_ref[0] + bu_ref[0]
            gt = jnp.minimum(g, SWIGLU_LIMIT)
            up = jnp.clip(u, -SWIGLU_LIMIT, SWIGLU_LIMIT)
            hq, sh = _quant_rows((up + 1.0) * (gt * jax.nn.sigmoid(SWIGLU_ALPHA * gt)))
            acc_ref[rs, :] += jnp.dot(hq, wd_ref[0], preferred_element_type=F32) * sh * sd_ref[0]

        @pl.when(j == pl.num_programs(1) - 1)
        def _():
            y = (acc_ref[...] + bd_ref[0]) * gate_ref[0]

            @pl.when(i > 0)
            def _():
                scatter_wait()

            _store_rows(ybuf, y, ROW_PITCH)
            islot = i % IDX_RING

            def body(r, carry):
                dst = pl.multiple_of(idx_smem[islot * 2 * rows + rows + r], w)
                pltpu.make_async_copy(ybuf.at[pl.ds(r * ROW_PITCH, w)], y_hbm.at[pl.ds(dst, w)], sem_s).start()
                return carry
            lax.fori_loop(0, rows, body, 0, unroll=8)

            @pl.when(i == n_used - 1)
            def _():
                scatter_wait()
                gather_wait((i + 1) % 2)


def _experts(block_expert, n_used, slot_idx, slot_gate, xn, w_gu, s_gu, b_gu, w_down, s_down, b_down):
    nb = slot_idx.shape[0]
    rows = slot_idx.shape[1] // 2
    e, d, ff2 = w_gu.shape
    w = d // LANES
    t = xn.shape[0] // w
    ff = ff2 // 2
    tf = MOE_TF
    nj = ff // tf
    a_rows = t * TOP_K + rows
    grid_spec = pltpu.PrefetchScalarGridSpec(
        num_scalar_prefetch=2,
        grid=(nb, nj),
        in_specs=[
            pl.BlockSpec(memory_space=pl.ANY),
            pl.BlockSpec((1, rows, 1), lambda i, j, be, nu: (i, 0, 0)),
            pl.BlockSpec(memory_space=pl.ANY),
            pl.BlockSpec((1, d, tf), lambda i, j, be, nu: (be[i], 0, j)),
            pl.BlockSpec((1, d, tf), lambda i, j, be, nu: (be[i], 0, j + nj)),
            pl.BlockSpec((1, 1, tf), lambda i, j, be, nu: (be[i], 0, j)),
            pl.BlockSpec((1, 1, tf), lambda i, j, be, nu: (be[i], 0, j + nj)),
            pl.BlockSpec((1, 1, tf), lambda i, j, be, nu: (be[i], 0, j)),
            pl.BlockSpec((1, 1, tf), lambda i, j, be, nu: (be[i], 0, j + nj)),
            pl.BlockSpec((1, tf, d), lambda i, j, be, nu: (be[i], j, 0)),
            pl.BlockSpec((1, 1, d), lambda i, j, be, nu: (be[i], 0, 0)),
            pl.BlockSpec((1, 1, d), lambda i, j, be, nu: (be[i], 0, 0)),
        ],
        out_specs=pl.BlockSpec(memory_space=pl.ANY),
        scratch_shapes=[
            pltpu.SMEM((IDX_RING * 2 * rows,), jnp.int32),
            pltpu.VMEM((2, rows * ROW_PITCH, LANES), F32),
            pltpu.VMEM((rows, d), FP8),
            pltpu.VMEM((rows, 1), F32),
            pltpu.VMEM((rows, d), F32),
            pltpu.VMEM((rows * ROW_PITCH, LANES), F32),
            pltpu.SemaphoreType.DMA((IDX_RING,)),
            pltpu.SemaphoreType.DMA((2,)),
            pltpu.SemaphoreType.DMA,
        ],
    )
    return pl.pallas_call(
        _experts_body,
        out_shape=jax.ShapeDtypeStruct((a_rows * w, LANES), F32),
        grid_spec=grid_spec,
        compiler_params=_cparams(("arbitrary", "arbitrary")),
        name="experts",
    )(block_expert, n_used, slot_idx, slot_gate, xn, w_gu, w_gu, s_gu, s_gu, b_gu, b_gu, w_down, s_down, b_down)


def _combine_body(final_norm, h_ref, y0_ref, y1_ref, y2_ref, y3_ref, g_ref, o_ref):
    n, d = h_ref.shape
    acc = h_ref[...] + _load_rows(y0_ref, n, d) + _load_rows(y1_ref, n, d) + _load_rows(y2_ref, n, d) \
        + _load_rows(y3_ref, n, d)
    o_ref[...] = _rms(acc, g_ref[...]) if final_norm else acc


def _combine(h2, y_tok, gain, tm):
    t, d = h2.shape
    w = d // LANES
    nt = t // tm
    final_norm = gain is not None
    if gain is None:
        gain = jnp.ones((d,), F32)
    gain = gain.reshape(1, d)
    plane = lambda k: pl.BlockSpec((tm * w, LANES), lambda i: (k * nt + i, 0))
    return pl.pallas_call(
        functools.partial(_combine_body, final_norm),
        out_shape=jax.ShapeDtypeStruct((t, d), F32),
        grid=(nt,),
        in_specs=[pl.BlockSpec((tm, d), lambda i: (i, 0))] + [plane(k) for k in range(TOP_K)]
                 + [pl.BlockSpec((1, d), lambda i: (0, 0))],
        out_specs=pl.BlockSpec((tm, d), lambda i: (i, 0)),
        compiler_params=_cparams(("arbitrary",)),
        name="combine",
    )(h2, y_tok, y_tok, y_tok, y_tok, gain)


def _routing(top_idx, gates, rows, row_width):
    k, t = top_idx.shape
    a = t * k
    flat_e = top_idx.reshape(-1)
    _, s_asg, s_gate = lax.sort((flat_e, jnp.arange(a, dtype=jnp.int32), gates.reshape(-1)), num_keys=1)
    experts = jnp.arange(N_EXPERTS, dtype=jnp.int32)
    counts = jnp.sum((flat_e[None, :] == experts[:, None]).astype(jnp.int32), axis=1)
    start = jnp.cumsum(counts) - counts
    padded = (counts + rows - 1) // rows * rows
    padded_end = jnp.cumsum(padded)
    padded_start = padded_end - padded
    nb = -(-a // rows) + N_EXPERTS
    n_used = (padded_end[-1] // rows).astype(jnp.int32)
    blk = jnp.arange(nb, dtype=jnp.int32)
    bexp = jnp.minimum(jnp.sum((padded_end[None, :] <= (blk * rows)[:, None]).astype(jnp.int32), axis=1),
                       N_EXPERTS - 1)
    bexp = jnp.where(blk < n_used, bexp, bexp[jnp.maximum(n_used - 1, 0)])
    off = blk * rows - padded_start[bexp]
    first = jnp.clip(start[bexp] + off, 0, a)
    nvalid = jnp.where(blk < n_used, jnp.clip(counts[bexp] - off, 0, rows), 0)
    r = jnp.arange(rows, dtype=jnp.int32)[None, :]
    pos = jnp.minimum(first[:, None] + r, a - 1)
    asg = s_asg[pos]
    gate = s_gate[pos]
    valid = r < nvalid[:, None]
    slot_src = jnp.where(valid, asg % t, 0)
    slot_dst = jnp.where(valid, asg, a + r)
    slot_gate = jnp.where(valid, gate, 0.0)
    slot_idx = jnp.concatenate([slot_src, slot_dst], axis=1) * row_width
    return bexp, n_used.reshape(1), slot_idx, slot_gate.reshape(nb, rows, 1)


def _moe_stage(h2, xn, top_idx, gates, w_gate_up, b_gate_up, w_down, b_down, norm_final):
    t, d = h2.shape
    bexp, n_used, slot_idx, slot_gate = _routing(top_idx, gates, MOE_ROWS, d // LANES)
    e = w_gate_up.shape[0]
    wq_gu, s_gu = _quant_weights(w_gate_up, MOE_TF)
    wq_d, s_d = _quant_weights(w_down, MOE_TF)
    y_tok = _experts(bexp, n_used, slot_idx, slot_gate, xn, wq_gu, s_gu, b_gate_up.reshape(e, 1, -1),
                     wq_d, s_d, b_down.reshape(e, 1, -1))
    return _combine(h2, y_tok, norm_final, min(256, t))


def _permute_w_in(w_in):
    sp = np.cumsum([0, ATT_WIDTH, KV_LATENT, IDX_WIDTH, IDX_DIM, IDX_HEADS, RET_QK_WIDTH, RET_QK_WIDTH,
                    RET_WIDTH, RET_WIDTH])
    piece = lambda i: w_in[:, int(sp[i]):int(sp[i + 1])]
    d = w_in.shape[0]
    cols = [piece(0), piece(2), piece(7), piece(8), piece(5), piece(6), piece(1), piece(3), piece(4)]
    used = sum(c.shape[1] for c in cols)
    cols.append(jnp.zeros((d, PROJ_WIDTH - used), w_in.dtype))
    return jnp.concatenate(cols, axis=1).astype(BF16)


def _rotary_tables(positions):
    half = RET_QK_DIM // 2
    inv = ROPE_BASE ** (-jnp.arange(0, RET_QK_DIM, 2, dtype=F32) / RET_QK_DIM)
    ang = positions.astype(F32).reshape(-1, 1) * inv
    cos, sin = jnp.cos(ang), jnp.sin(ang)
    return jnp.concatenate([cos, cos], axis=1), jnp.concatenate([-sin, sin], axis=1)


def _front(x, positions, norm_mix, w_in, kv_norm, idx_k_norm):
    b, s, d = x.shape
    t = b * s
    tm = min(1024, t)
    proj = _proj(x.reshape(t, d), norm_mix.reshape(1, d), _permute_w_in(w_in), tm)
    cos2, sin2 = _rotary_tables(positions)
    g_k = jnp.concatenate([idx_k_norm, jnp.zeros((LANES - IDX_DIM,), F32)]).reshape(1, LANES)
    c, kidx, widx, q_rot, k_rot = _prep(proj, cos2, sin2, kv_norm.reshape(1, KV_LATENT), g_k, tm)
    return dict(proj=proj, c=c, kidx=kidx, widx=widx, q_rot=q_rot, k_rot=k_rot)


def _mid_stage(x, mem, y_att, y_ret, w_out, norm_cross, norm_mem, w_cq, w_ck, w_cv, w_co, norm_ffn,
               w_router, b_router):
    b, s, d = x.shape
    km, vm = _memkv(mem, norm_mem.reshape(1, d), w_ck.astype(BF16), w_cv.astype(BF16))
    w_out_b16 = w_out.astype(BF16)
    return _mid(x.reshape(b * s, d), y_att, y_ret, w_out_b16[:ATT_WIDTH], w_out_b16[ATT_WIDTH:],
                norm_cross.reshape(1, d), w_cq.astype(BF16), km, vm, w_co.astype(BF16),
                norm_ffn.reshape(1, d), w_router.T.astype(BF16), b_router.reshape(N_EXPERTS, 1),
                b, s, min(512, s))


def _ret_stage(st, b, s):
    q_rot = st["q_rot"].reshape(b, s, RET_QK_WIDTH)
    kt_rot = jnp.swapaxes(st["k_rot"].reshape(b, s, RET_QK_WIDTH), 1, 2)
    return _retention(q_rot, kt_rot, st["proj"], b, s)


def kernel(x, mem, positions, norm_mix, w_in, kv_norm, idx_k_norm, w_uk, w_uv, rel_bias, w_out, norm_cross,
           norm_mem, w_cq, w_ck, w_cv, w_co, norm_ffn, w_router, b_router, w_gate_up, b_gate_up, w_down, b_down,
           norm_final):
    b, s, d = x.shape
    depth = norm_mix.shape[0]
    h = x
    for l in range(depth):
        st = _front(h, positions, norm_mix[l], w_in[l], kv_norm[l], idx_k_norm[l])
        y_att = _dsa_stage(st, w_uk[l], w_uv[l], rel_bias, b, s)
        y_ret = _ret_stage(st, b, s)
        h2, xn, top_idx, gates = _mid_stage(h, mem, y_att, y_ret, w_out[l], norm_cross[l], norm_mem[l], w_cq[l],
                                            w_ck[l], w_cv[l], w_co[l], norm_ffn[l], w_router[l], b_router[l])
        gain = norm_final if l == depth - 1 else None
        h = _moe_stage(h2, xn, top_idx, gates, w_gate_up[l], b_gate_up[l], w_down[l], b_down[l], gain).reshape(b, s, d)
    return h
```

```python
import functools
import math

import jax
import jax.numpy as jnp
import numpy as np
from jax import lax
from jax.experimental import pallas as pl
from jax.experimental.pallas import tpu as pltpu

F32 = jnp.float32
BF16 = jnp.bfloat16

D_MODEL = 2048
N_ATT_HEADS = 8
ATT_HEAD_DIM = 128
KV_LATENT = 256
IDX_HEADS = 16
IDX_DIM = 64
IDX_TOPK_MAX = 256
N_RET_HEADS = 4
RET_QK_DIM = 128
RET_V_DIM = 256
ROPE_BASE = 10000.0
N_BUCKETS = 32
MAX_DISTANCE = 128
N_CROSS_HEADS = 4
CROSS_HEAD_DIM = 128
N_EXPERTS = 32
TOP_K = 4
D_FF = D_MODEL
SWIGLU_LIMIT = 7.0
SWIGLU_ALPHA = 1.702
EPS = 1e-6

ATT_WIDTH = N_ATT_HEADS * ATT_HEAD_DIM
RET_WIDTH = N_RET_HEADS * RET_V_DIM
RET_QK_WIDTH = N_RET_HEADS * RET_QK_DIM
IDX_WIDTH = IDX_HEADS * IDX_DIM
CROSS_WIDTH = N_CROSS_HEADS * CROSS_HEAD_DIM

LANES = 128
VMEM_LIMIT = 56 * 1024 * 1024

COL_Q_ATT = 0
COL_Q_IDX = COL_Q_ATT + ATT_WIDTH
COL_V_RET = COL_Q_IDX + IDX_WIDTH
COL_G_RET = COL_V_RET + RET_WIDTH
COL_Q_RET = COL_G_RET + RET_WIDTH
COL_K_RET = COL_Q_RET + RET_QK_WIDTH
COL_C_KV = COL_K_RET + RET_QK_WIDTH
COL_KW = COL_C_KV + KV_LATENT
PROJ_TN = 512
PROJ_WIDTH = 5632

Q_BLOCK = 256
KEY_TILE = 512
KEY_PAD = KEY_TILE - Q_BLOCK
NEG_BIG = -1e30
LOG2E = math.log2(math.e)

RET_CHUNK = 256

MOE_ROWS = 512
IDX_RING = 3
MOE_TF = 1024


def _cparams(sem, vmem=VMEM_LIMIT):
    return pltpu.CompilerParams(dimension_semantics=sem, vmem_limit_bytes=vmem)


def _rms(x, g):
    return x * lax.rsqrt(jnp.mean(x * x, axis=-1, keepdims=True) + EPS) * g


ROW_PITCH = 20


def _store_rows(ref, x, pitch=None):
    n, d = x.shape
    w = d // LANES
    for a in range(w):
        ref[pl.ds(a, n, stride=pitch or w), :] = x[:, a * LANES:(a + 1) * LANES].astype(ref.dtype)


def _load_rows(ref, n, d, pitch=None):
    w = d // LANES
    return jnp.concatenate([ref[pl.ds(a, n, stride=pitch or w), :] for a in range(w)], axis=1)


def _proj_body(x_ref, g_ref, w_ref, o_ref, xn_ref):
    @pl.when(pl.program_id(1) == 0)
    def _():
        xn_ref[...] = _rms(x_ref[...], g_ref[...]).astype(BF16)

    o_ref[...] = jnp.dot(xn_ref[...], w_ref[...], preferred_element_type=F32).astype(o_ref.dtype)


def _proj(x2, gain, w_p, tm):
    t, d = x2.shape
    n = w_p.shape[1]
    return pl.pallas_call(
        _proj_body,
        out_shape=jax.ShapeDtypeStruct((t, n), BF16),
        grid=(t // tm, n // PROJ_TN),
        in_specs=[
            pl.BlockSpec((tm, d), lambda i, j: (i, 0)),
            pl.BlockSpec((1, d), lambda i, j: (0, 0)),
            pl.BlockSpec((d, PROJ_TN), lambda i, j: (0, j)),
        ],
        out_specs=pl.BlockSpec((tm, PROJ_TN), lambda i, j: (i, j)),
        scratch_shapes=[pltpu.VMEM((tm, d), BF16)],
        compiler_params=_cparams(("arbitrary", "arbitrary")),
        name="proj",
    )(x2, gain, w_p)


def _prep_body(c_ref, kw_ref, q_ref, k_ref, cos_ref, sin_ref, gkv_ref, gk_ref,
               c_out, kidx_out, widx_out, q_out, k_out):
    c_out[...] = _rms(c_ref[...].astype(F32), gkv_ref[...]).astype(BF16)

    kw = kw_ref[...].astype(F32)
    lane = lax.broadcasted_iota(jnp.int32, kw.shape, 1)
    is_k = lane < IDX_DIM
    kk = jnp.where(is_k, kw, 0.0)
    ms = jnp.sum(kk * kk, axis=-1, keepdims=True) * (1.0 / IDX_DIM)
    kidx_out[...] = (kk * lax.rsqrt(ms + EPS) * gk_ref[...]).astype(BF16)
    widx_out[...] = kw * (IDX_HEADS ** -0.5 * IDX_DIM ** -0.5)

    cos2 = cos_ref[...]
    sin2 = sin_ref[...]
    for h in range(N_RET_HEADS):
        sl = slice(h * RET_QK_DIM, (h + 1) * RET_QK_DIM)
        qh = q_ref[:, sl].astype(F32)
        kh = k_ref[:, sl].astype(F32)
        q_out[:, sl] = (qh * cos2 + pltpu.roll(qh, RET_QK_DIM // 2, 1) * sin2).astype(BF16)
        k_out[:, sl] = ((kh * cos2 + pltpu.roll(kh, RET_QK_DIM // 2, 1) * sin2)
                        * (RET_QK_DIM ** -0.5)).astype(BF16)


def _prep(proj, cos2, sin2, g_kv, g_k, tm):
    t = proj.shape[0]
    row = lambda i: (i, 0)
    return pl.pallas_call(
        _prep_body,
        out_shape=(
            jax.ShapeDtypeStruct((t, KV_LATENT), BF16),
            jax.ShapeDtypeStruct((t, LANES), BF16),
            jax.ShapeDtypeStruct((t, LANES), F32),
            jax.ShapeDtypeStruct((t, RET_QK_WIDTH), BF16),
            jax.ShapeDtypeStruct((t, RET_QK_WIDTH), BF16),
        ),
        grid=(t // tm,),
        in_specs=[
            pl.BlockSpec((tm, KV_LATENT), lambda i: (i, COL_C_KV // KV_LATENT)),
            pl.BlockSpec((tm, LANES), lambda i: (i, COL_KW // LANES)),
            pl.BlockSpec((tm, RET_QK_WIDTH), lambda i: (i, COL_Q_RET // RET_QK_WIDTH)),
            pl.BlockSpec((tm, RET_QK_WIDTH), lambda i: (i, COL_K_RET // RET_QK_WIDTH)),
            pl.BlockSpec((tm, LANES), row),
            pl.BlockSpec((tm, LANES), row),
            pl.BlockSpec((1, KV_LATENT), lambda i: (0, 0)),
            pl.BlockSpec((1, LANES), lambda i: (0, 0)),
        ],
        out_specs=(
            pl.BlockSpec((tm, KV_LATENT), row),
            pl.BlockSpec((tm, LANES), row),
            pl.BlockSpec((tm, LANES), row),
            pl.BlockSpec((tm, RET_QK_WIDTH), row),
            pl.BlockSpec((tm, RET_QK_WIDTH), row),
        ),
        compiler_params=_cparams(("arbitrary",)),
        name="prep",
    )(proj, proj, proj, proj, cos2, sin2, g_kv, g_k)


def _ret_body(q_ref, kt_ref, v_ref, g_ref, dec_ref, cross_ref, state_ref, cd_ref, o_ref, r_ref):
    @pl.when(pl.program_id(1) == 0)
    def _():
        r_ref[...] = jnp.zeros_like(r_ref)

    for h in range(N_RET_HEADS):
        qs = slice(h * RET_QK_DIM, (h + 1) * RET_QK_DIM)
        vs = slice(h * RET_V_DIM, (h + 1) * RET_V_DIM)
        q = q_ref[0, :, qs]
        kt = kt_ref[0, qs, :]
        v = v_ref[:, vs]
        r_old = r_ref[h]
        inner = jnp.dot(q, kt, preferred_element_type=F32) * dec_ref[h]
        qc = (q.astype(F32) * cross_ref[h]).astype(BF16)
        o = (jnp.dot(inner.astype(BF16), v, preferred_element_type=F32)
             + jnp.dot(qc, r_old.astype(BF16), preferred_element_type=F32))
        ks = (kt.astype(F32) * state_ref[h]).astype(BF16)
        r_ref[h] = r_old * cd_ref[h] + jnp.dot(ks, v, preferred_element_type=F32)
        o = o * lax.rsqrt(jnp.mean(o * o, axis=-1, keepdims=True) + EPS)
        g = g_ref[:, vs].astype(F32)
        o_ref[:, vs] = (g * jax.nn.sigmoid(g) * o).astype(o_ref.dtype)


def _retention(q_rot, kt_rot, proj, b, s):
    c = RET_CHUNK
    n = s // c
    hh = N_RET_HEADS
    log_gamma = jnp.log(1.0 - 2.0 ** (-5.0 - jnp.arange(hh, dtype=F32)))
    j = jnp.arange(c, dtype=F32)
    diff = j[:, None] - j[None, :]
    dec = jnp.where(diff >= 0, jnp.exp(log_gamma[:, None, None] * jnp.maximum(diff, 0.0)), 0.0)
    cross = jnp.broadcast_to(jnp.exp(log_gamma[:, None] * (j + 1.0))[:, :, None], (hh, c, RET_QK_DIM))
    state = jnp.exp(log_gamma[:, None] * (c - 1.0 - j))[:, None, :]
    cdec = jnp.broadcast_to(jnp.exp(log_gamma * c)[:, None, None], (hh, 1, RET_V_DIM))
    const = lambda bi, ci: (0, 0, 0)
    return pl.pallas_call(
        _ret_body,
        out_shape=jax.ShapeDtypeStruct((b * s, RET_WIDTH), BF16),
        grid=(b, n),
        in_specs=[
            pl.BlockSpec((1, c, RET_QK_WIDTH), lambda bi, ci: (bi, ci, 0)),
            pl.BlockSpec((1, RET_QK_WIDTH, c), lambda bi, ci: (bi, 0, ci)),
            pl.BlockSpec((c, RET_WIDTH), lambda bi, ci: (bi * n + ci, COL_V_RET // RET_WIDTH)),
            pl.BlockSpec((c, RET_WIDTH), lambda bi, ci: (bi * n + ci, COL_G_RET // RET_WIDTH)),
            pl.BlockSpec((hh, c, c), const),
            pl.BlockSpec((hh, c, RET_QK_DIM), const),
            pl.BlockSpec((hh, 1, c), const),
            pl.BlockSpec((hh, 1, RET_V_DIM), const),
        ],
        out_specs=pl.BlockSpec((c, RET_WIDTH), lambda bi, ci: (bi * n + ci, 0)),
        scratch_shapes=[pltpu.VMEM((hh, RET_QK_DIM, RET_V_DIM), F32)],
        compiler_params=_cparams(("arbitrary", "arbitrary")),
        name="ret",
    )(q_rot, kt_rot, proj, proj, dec, cross, state, cdec)


def _dsa_body(topk, kidx_ref, qit_ref, w_ref, c_ref, ct_ref, qt_ref, wuk_ref, wuvt_ref, bias_ref, y_ref,
              sc_ref, qlt_ref, acc_ref, m_ref, l_ref, st_ref):
    qb = pl.program_id(1)
    tk = KEY_TILE
    half = tk // 2
    e_pad = (qb + 1) * Q_BLOCK + KEY_PAD
    n_tiles = e_pad // tk
    nh = N_ATT_HEADS
    lane_q = qb * Q_BLOCK + lax.broadcasted_iota(jnp.int32, (1, Q_BLOCK), 1)

    def tile_start(j):
        return pl.multiple_of(e_pad - (j + 1) * tk, Q_BLOCK)

    for h in range(nh):
        sl = slice(h * Q_BLOCK, (h + 1) * Q_BLOCK)
        hd = slice(h * ATT_HEAD_DIM, (h + 1) * ATT_HEAD_DIM)
        qlt_ref[:, sl] = (jnp.dot(wuk_ref[h], qt_ref[0, 0, hd, :], preferred_element_type=F32)
                          * (ATT_HEAD_DIM ** -0.5 * LOG2E)).astype(BF16)

    def score_tile(j, carry):
        lo8, hi8 = carry
        r0 = tile_start(j)
        for part in range(2):
            rs = pl.multiple_of(r0 + part * half, Q_BLOCK)
            z = jnp.dot(kidx_ref[0, pl.ds(rs, half), :], qit_ref[0, 0], preferred_element_type=F32)
            tot = jnp.zeros((half, Q_BLOCK), F32)
            for h in range(IDX_HEADS):
                sl = slice(h * Q_BLOCK, (h + 1) * Q_BLOCK)
                tot = tot + jnp.maximum(z[:, sl], 0.0) * w_ref[0, 0, :, sl]
            key = rs - KEY_PAD + lax.broadcasted_iota(jnp.int32, (half, Q_BLOCK), 0)
            ok = (key >= 0) & (key <= lane_q)
            val = jnp.where(ok, tot, NEG_BIG)
            sc_ref[pl.ds(rs, half), :] = val
            hi8 = jnp.maximum(hi8, jnp.max(val.reshape(half // 8, 8, Q_BLOCK), axis=0))
            lo8 = jnp.minimum(lo8, jnp.min(jnp.where(ok, tot, -NEG_BIG).reshape(half // 8, 8, Q_BLOCK), axis=0))
        return lo8, hi8
    lo8, hi8 = lax.fori_loop(0, n_tiles, score_tile,
                             (jnp.full((8, Q_BLOCK), -NEG_BIG, F32), jnp.full((8, Q_BLOCK), NEG_BIG, F32)))

    lo0 = jnp.min(lo8, axis=0, keepdims=True)
    hi0 = jnp.max(hi8, axis=0, keepdims=True)
    search = lane_q + 1 > topk

    n_chain = 4
    rows_chain = tk // n_chain

    def count_ge(x):
        def body(j, cnts):
            r0 = tile_start(j)
            out = []
            for k in range(n_chain):
                s = sc_ref[pl.ds(pl.multiple_of(r0 + k * rows_chain, rows_chain), rows_chain), :]
                ind = jnp.where(s >= x, 1.0, 0.0).reshape(rows_chain // 8, 8, Q_BLOCK)
                out.append(cnts[k] + jnp.sum(ind, axis=0))
            return tuple(out)
        cnts = lax.fori_loop(0, n_tiles, body, tuple(jnp.zeros((8, Q_BLOCK), F32) for _ in range(n_chain)))
        return jnp.sum((cnts[0] + cnts[1]) + (cnts[2] + cnts[3]), axis=0, keepdims=True)

    steps_per_check = 3

    def bis_cond(st):
        it, _, _, _, active = st
        return (it < 42) & (active > 0.0)

    def bis_step(_, st):
        lo, hi, done = st
        mid = lo + 0.5 * (hi - lo)
        cnt = count_ge(mid)
        ge = cnt >= topk
        open_ = done < 0.5
        lo = jnp.where(open_ & ge, mid, lo)
        hi = jnp.where(open_ & (~ge), mid, hi)
        done = jnp.where(cnt == topk, 1.0, done)
        return lo, hi, done

    def bis_body(st):
        it, lo, hi, done, _ = st
        lo, hi, done = lax.fori_loop(0, steps_per_check, bis_step, (lo, hi, done))
        return it + steps_per_check, lo, hi, done, jnp.max(1.0 - done)

    done0 = jnp.where(search, 0.0, 1.0)
    _, lo_f, _, _, _ = lax.while_loop(bis_cond, bis_body, (jnp.int32(0), lo0, hi0, done0, jnp.max(1.0 - done0)))
    thr = jnp.where(search, lo_f, 0.5 * NEG_BIG)

    m_ref[...] = jnp.full(m_ref.shape, NEG_BIG, F32)
    l_ref[...] = jnp.zeros(l_ref.shape, F32)
    acc_ref[...] = jnp.zeros(acc_ref.shape, F32)

    n_grp = 8
    hg = nh // n_grp

    def attend(segments):
        ct_tiles = []
        for k, (rs, n, with_bias) in enumerate(segments):
            c_tile = c_ref[0, pl.ds(rs, n), :]
            ct_tiles.append((n, ct_ref[0, :, pl.ds(rs, n)]))
            pen = jnp.where(sc_ref[pl.ds(rs, n), :] >= thr, 0.0, NEG_BIG)
            pen = jnp.concatenate([pen] * hg, axis=1)
            for g in range(n_grp):
                ls = slice(g * hg * Q_BLOCK, (g + 1) * hg * Q_BLOCK)
                st = jnp.dot(c_tile, qlt_ref[:, ls], preferred_element_type=F32) + pen
                if with_bias:
                    st = st + bias_ref[:, ls]
                st_ref[k, 0:n, ls] = st
        for k, (n, ct_tile) in enumerate(ct_tiles):
            for g in range(n_grp):
                ls = slice(g * hg * Q_BLOCK, (g + 1) * hg * Q_BLOCK)
                st = st_ref[k, 0:n, ls]
                m_old = m_ref[:, ls]
                m_new = jnp.maximum(m_old, jnp.max(st, axis=0, keepdims=True))
                alpha = jnp.exp2(m_old - m_new)
                p = jnp.exp2(st - m_new)
                l_ref[:, ls] = alpha * l_ref[:, ls] + jnp.sum(p, axis=0, keepdims=True)
                acc_ref[:, ls] = acc_ref[:, ls] * alpha + jnp.dot(ct_tile, p.astype(BF16),
                                                                  preferred_element_type=F32)
                m_ref[:, ls] = m_new

    r_first = tile_start(0)
    near = 2 * Q_BLOCK
    if near == tk:
        attend([(r_first, tk, True)])
    else:
        attend([(pl.multiple_of(r_first + tk - near, Q_BLOCK), near, True), (r_first, tk - near, False)])

    def attend_pair(i, carry):
        attend([(tile_start(2 * i + 1), tk, False), (tile_start(2 * i + 2), tk, False)])
        return carry
    n_rest = n_tiles - 1
    lax.fori_loop(0, n_rest // 2, attend_pair, 0)

    @pl.when(n_rest % 2 == 1)
    def _():
        attend([(tile_start(n_tiles - 1), tk, False)])

    ot = acc_ref[...] / l_ref[...]
    for h in range(nh):
        sl = slice(h * Q_BLOCK, (h + 1) * Q_BLOCK)
        y_ref[0, h * ATT_HEAD_DIM:(h + 1) * ATT_HEAD_DIM, :] = jnp.dot(
            wuvt_ref[h], ot[:, sl].astype(BF16), preferred_element_type=F32).astype(y_ref.dtype)


def _dsa(kidx_p, qit, widx_t, c_p, ct_p, qt, wuk_h, wuvt_h, bias_t, b, s, topk):
    nqb = s // Q_BLOCK
    sp = s + KEY_PAD
    nh = N_ATT_HEADS
    per_b = lambda bi, qi: (bi, 0, 0)
    per_q = lambda bi, qi: (bi, qi, 0, 0)
    c3 = lambda bi, qi: (0, 0, 0)
    return pl.pallas_call(
        functools.partial(_dsa_body, topk),
        out_shape=jax.ShapeDtypeStruct((b, ATT_WIDTH, s), BF16),
        grid=(b, nqb),
        in_specs=[
            pl.BlockSpec((1, sp, LANES), per_b),
            pl.BlockSpec((1, 1, LANES, IDX_HEADS * Q_BLOCK), per_q),
            pl.BlockSpec((1, 1, 1, IDX_HEADS * Q_BLOCK), per_q),
            pl.BlockSpec((1, sp, KV_LATENT), per_b),
            pl.BlockSpec((1, KV_LATENT, sp), per_b),
            pl.BlockSpec((1, 1, ATT_WIDTH, Q_BLOCK), per_q),
            pl.BlockSpec((nh, KV_LATENT, ATT_HEAD_DIM), c3),
            pl.BlockSpec((nh, ATT_HEAD_DIM, KV_LATENT), c3),
            pl.BlockSpec((2 * Q_BLOCK, nh * Q_BLOCK), lambda bi, qi: (0, 0)),
        ],
        out_specs=pl.BlockSpec((1, ATT_WIDTH, Q_BLOCK), lambda bi, qi: (bi, 0, qi)),
        scratch_shapes=[
            pltpu.VMEM((sp, Q_BLOCK), F32),
            pltpu.VMEM((KV_LATENT, nh * Q_BLOCK), BF16),
            pltpu.VMEM((KV_LATENT, nh * Q_BLOCK), F32),
            pltpu.VMEM((1, nh * Q_BLOCK), F32),
            pltpu.VMEM((1, nh * Q_BLOCK), F32),
            pltpu.VMEM((2, KEY_TILE, nh * Q_BLOCK), F32),
        ],
        compiler_params=_cparams(("arbitrary", "arbitrary")),
        name="dsa",
    )(kidx_p, qit, widx_t, c_p, ct_p, qt, wuk_h, wuvt_h, bias_t)


def _t5_bucket(dist):
    n = jnp.maximum(dist, 0)
    max_exact = N_BUCKETS // 2
    nf = jnp.maximum(n, 1).astype(F32)
    large = max_exact + (jnp.log(nf / max_exact) / math.log(MAX_DISTANCE / max_exact)
                         * (N_BUCKETS - max_exact)).astype(jnp.int32)
    return jnp.where(n < max_exact, n, jnp.minimum(large, N_BUCKETS - 1))


def _dsa_stage(st, w_uk, w_uv, rel_bias, b, s):
    assert Q_BLOCK >= MAX_DISTANCE and 2 * Q_BLOCK <= KEY_TILE and s % Q_BLOCK == 0
    nqb = s // Q_BLOCK
    nh = N_ATT_HEADS
    proj = st["proj"].reshape(b, nqb, Q_BLOCK, PROJ_WIDTH)
    qt = jnp.swapaxes(proj[..., COL_Q_ATT:COL_Q_ATT + ATT_WIDTH], 2, 3)
    qi = proj[..., COL_Q_IDX:COL_Q_IDX + IDX_WIDTH].reshape(b, nqb, Q_BLOCK, IDX_HEADS, IDX_DIM)
    qit = jnp.transpose(qi, (0, 1, 4, 3, 2)).reshape(b, nqb, IDX_DIM, IDX_HEADS * Q_BLOCK)
    qit = jnp.pad(qit, ((0, 0), (0, 0), (0, LANES - IDX_DIM), (0, 0)))
    wi = st["widx"].reshape(b, nqb, Q_BLOCK, LANES)[..., IDX_DIM:IDX_DIM + IDX_HEADS]
    widx_t = jnp.swapaxes(wi, 2, 3).reshape(b, nqb, 1, IDX_HEADS * Q_BLOCK)
    front = ((0, 0), (KEY_PAD, 0), (0, 0))
    kidx_p = jnp.pad(st["kidx"].reshape(b, s, LANES), front)
    c_p = jnp.pad(st["c"].reshape(b, s, KV_LATENT), front)
    ct_p = jnp.swapaxes(c_p, 1, 2)
    wuk_h = jnp.transpose(w_uk, (1, 0, 2)).astype(BF16)
    wuvt_h = jnp.transpose(w_uv, (1, 2, 0)).astype(BF16)
    i = jnp.arange(2 * Q_BLOCK)[:, None]
    j = jnp.arange(Q_BLOCK)[None, :]
    rb = (rel_bias - rel_bias[N_BUCKETS - 1]) * LOG2E
    onehot = (_t5_bucket(j - i + Q_BLOCK)[None] == jnp.arange(N_BUCKETS)[:, None, None]).astype(F32)
    bias_t = jnp.einsum("bij,bh->ihj", onehot, rb, precision=lax.Precision.HIGHEST).reshape(2 * Q_BLOCK, nh * Q_BLOCK)
    topk = min(IDX_TOPK_MAX, s // 4)
    yt = _dsa(kidx_p, qit, widx_t, c_p, ct_p, qt, wuk_h, wuvt_h, bias_t, b, s, topk)
    return jnp.swapaxes(yt, 1, 2).reshape(b * s, ATT_WIDTH)


def _memkv_body(m_ref, g_ref, wk_ref, wv_ref, k_ref, v_ref):
    mn = _rms(m_ref[0], g_ref[...]).astype(BF16)
    k_ref[0] = jnp.dot(mn, wk_ref[...], preferred_element_type=F32).astype(BF16)
    v_ref[0] = jnp.dot(mn, wv_ref[...], preferred_element_type=F32).astype(BF16)


def _memkv(mem, gain, w_ck, w_cv):
    b, m, d = mem.shape
    w = w_ck.shape[1]
    const = lambda i: (0, 0)
    out = jax.ShapeDtypeStruct((b, m, w), BF16)
    return pl.pallas_call(
        _memkv_body,
        out_shape=(out, out),
        grid=(b,),
        in_specs=[
            pl.BlockSpec((1, m, d), lambda i: (i, 0, 0)),
            pl.BlockSpec((1, d), const),
            pl.BlockSpec((d, w), const),
            pl.BlockSpec((d, w), const),
        ],
        out_specs=(pl.BlockSpec((1, m, w), lambda i: (i, 0, 0)), pl.BlockSpec((1, m, w), lambda i: (i, 0, 0))),
        compiler_params=_cparams(("arbitrary",)),
        name="memkv",
    )(mem, gain, w_ck, w_cv)


def _mid_body(x_ref, ya_ref, yr_ref, woa_ref, wob_ref, gc_ref, wq_ref, km_ref, vm_ref, wo_ref,
              gf_ref, wr_ref, br_ref, h2_ref, xn_ref, idx_ref, gate_ref):
    h1 = (x_ref[...]
          + jnp.dot(ya_ref[...], woa_ref[...], preferred_element_type=F32)
          + jnp.dot(yr_ref[...], wob_ref[...], preferred_element_type=F32))
    hn = _rms(h1, gc_ref[...]).astype(BF16)
    q = jnp.dot(hn, wq_ref[...], preferred_element_type=F32).astype(BF16)
    heads = []
    for h in range(N_CROSS_HEADS):
        sl = slice(h * CROSS_HEAD_DIM, (h + 1) * CROSS_HEAD_DIM)
        s = lax.dot_general(q[:, sl], km_ref[0, :, sl], (((1,), (1,)), ((), ())),
                            preferred_element_type=F32) * (CROSS_HEAD_DIM ** -0.5)
        p = jnp.exp(s - jnp.max(s, axis=-1, keepdims=True))
        p = p / jnp.sum(p, axis=-1, keepdims=True)
        heads.append(jnp.dot(p.astype(BF16), vm_ref[0, :, sl], preferred_element_type=F32).astype(BF16))
    o = jnp.concatenate(heads, axis=-1)
    h2 = h1 + jnp.dot(o, wo_ref[...], preferred_element_type=F32)
    h2_ref[...] = h2

    xn = _rms(h2, gf_ref[...])
    _store_rows(xn_ref, xn)
    logit = lax.dot_general(wr_ref[...], xn.astype(BF16), (((1,), (1,)), ((), ())),
                            preferred_element_type=F32) + br_ref[...]
    eid = lax.broadcasted_iota(jnp.int32, logit.shape, 0)
    vals, ids = [], []
    for _ in range(TOP_K):
        mx = jnp.max(logit, axis=0, keepdims=True)
        sel = jnp.min(jnp.where(logit == mx, eid, N_EXPERTS), axis=0, keepdims=True)
        vals.append(mx)
        ids.append(sel)
        logit = jnp.where(eid == sel, -jnp.inf, logit)
    top = jnp.concatenate(vals, axis=0)
    e = jnp.exp(top - vals[0])
    gate_ref[...] = e / jnp.sum(e, axis=0, keepdims=True)
    idx_ref[...] = jnp.concatenate(ids, axis=0)


def _mid(x2, ya, yr, w_out_a, w_out_b, g_cross, w_cq, km, vm, w_co, g_ffn, w_rt, b_r, b, s, tm):
    t, d = x2.shape
    nt = s // tm
    row = lambda bi, i: (bi * nt + i, 0)
    const = lambda bi, i: (0, 0)
    once = pl.Buffered(1)
    m = km.shape[1]
    return pl.pallas_call(
        _mid_body,
        out_shape=(
            jax.ShapeDtypeStruct((t, d), F32),
            jax.ShapeDtypeStruct((t * (d // LANES), LANES), F32),
            jax.ShapeDtypeStruct((TOP_K, t), jnp.int32),
            jax.ShapeDtypeStruct((TOP_K, t), F32),
        ),
        grid=(b, nt),
        in_specs=[
            pl.BlockSpec((tm, d), row),
            pl.BlockSpec((tm, ATT_WIDTH), row),
            pl.BlockSpec((tm, RET_WIDTH), row),
            pl.BlockSpec((ATT_WIDTH, d), const, pipeline_mode=once),
            pl.BlockSpec((RET_WIDTH, d), const, pipeline_mode=once),
            pl.BlockSpec((1, d), const),
            pl.BlockSpec((d, CROSS_WIDTH), const, pipeline_mode=once),
            pl.BlockSpec((1, m, CROSS_WIDTH), lambda bi, i: (bi, 0, 0)),
            pl.BlockSpec((1, m, CROSS_WIDTH), lambda bi, i: (bi, 0, 0)),
            pl.BlockSpec((CROSS_WIDTH, d), const, pipeline_mode=once),
            pl.BlockSpec((1, d), const),
            pl.BlockSpec((N_EXPERTS, d), const),
            pl.BlockSpec((N_EXPERTS, 1), const),
        ],
        out_specs=(
            pl.BlockSpec((tm, d), row),
            pl.BlockSpec((tm * (d // LANES), LANES), row),
            pl.BlockSpec((TOP_K, tm), lambda bi, i: (0, bi * nt + i)),
            pl.BlockSpec((TOP_K, tm), lambda bi, i: (0, bi * nt + i)),
        ),
        compiler_params=_cparams(("arbitrary", "arbitrary")),
        name="mid",
    )(x2, ya, yr, w_out_a, w_out_b, g_cross, w_cq, km, vm, w_co, g_ffn, w_rt, b_r)


FP8 = jnp.float8_e4m3fn
FP8_MAX = 448.0
TINY = 1e-30


def _quant_body(w_ref, q_ref, s_ref):
    w = w_ref[0]
    amax = jnp.maximum(jnp.max(jnp.abs(w), axis=0, keepdims=True), TINY)
    q_ref[0] = (w * (FP8_MAX / amax)).astype(FP8)
    s_ref[0] = amax * (1.0 / FP8_MAX)


def _quant_weights(w, tn):
    e, k, n = w.shape
    return pl.pallas_call(
        _quant_body,
        out_shape=(jax.ShapeDtypeStruct((e, k, n), FP8), jax.ShapeDtypeStruct((e, 1, n), F32)),
        grid=(e, n // tn),
        in_specs=[pl.BlockSpec((1, k, tn), lambda i, j: (i, 0, j))],
        out_specs=(pl.BlockSpec((1, k, tn), lambda i, j: (i, 0, j)), pl.BlockSpec((1, 1, tn), lambda i, j: (i, 0, j))),
        compiler_params=_cparams(("arbitrary", "arbitrary")),
        name="quant",
    )(w)


def _quant_rows(x):
    amax = jnp.maximum(jnp.max(jnp.abs(x), axis=1, keepdims=True), TINY)
    return (x * (FP8_MAX / amax)).astype(FP8), amax * (1.0 / FP8_MAX)


def _experts_body(be_ref, nu_ref, idx_hbm, gate_ref, xn_hbm, wg_ref, wu_ref, sg_ref, su_ref, bg_ref, bu_ref,
                  wd_ref, sd_ref, bd_ref, y_hbm, idx_smem, xbuf, xb_ref, sx_ref, acc_ref, ybuf,
                  sem_idx, sem_g, sem_s):
    i = pl.program_id(0)
    j = pl.program_id(1)
    n_used = nu_ref[0]
    rows, d = xb_ref.shape
    w = d // LANES

    def idx_copy(blk, slot):
        return pltpu.make_async_copy(idx_hbm.at[blk],
                                     idx_smem.at[pl.ds(pl.multiple_of(slot * 2 * rows, 2 * rows), 2 * rows)],
                                     sem_idx.at[slot])

    def gather_start(islot, xslot):
        def body(r, carry):
            src = pl.multiple_of(idx_smem[islot * 2 * rows + r], w)
            pltpu.make_async_copy(xn_hbm.at[pl.ds(src, w)], xbuf.at[xslot, pl.ds(r * ROW_PITCH, w)],
                                  sem_g.at[xslot]).start()
            return carry
        lax.fori_loop(0, rows, body, 0, unroll=8)

    def gather_wait(xslot):
        pltpu.make_async_copy(xn_hbm.at[pl.ds(0, rows * w)], xbuf.at[xslot, pl.ds(0, rows * w)],
                              sem_g.at[xslot]).wait()

    def scatter_wait():
        pltpu.make_async_copy(ybuf.at[pl.ds(0, rows * w)], y_hbm.at[pl.ds(0, rows * w)], sem_s).wait()

    @pl.when(j == 0)
    def _():
        @pl.when(i == 0)
        def _():
            ybuf[...] = jnp.zeros_like(ybuf)
            dump = pltpu.make_async_copy(ybuf.at[pl.ds(0, rows * w)],
                                         y_hbm.at[pl.ds(y_hbm.shape[0] - rows * w, rows * w)], sem_s)
            dump.start()
            dump.wait()
            idx_copy(0, 0).start()

            @pl.when(n_used > 1)
            def _():
                idx_copy(1, 1).start()

            idx_copy(0, 0).wait()
            gather_start(0, 0)

        @pl.when(i + 1 < n_used)
        def _():
            idx_copy(i + 1, (i + 1) % IDX_RING).wait()

        @pl.when(i + 2 < n_used)
        def _():
            idx_copy(i + 2, (i + 2) % IDX_RING).start()

        @pl.when(i < n_used)
        def _():
            gather_wait(i % 2)
            xb_ref[...], sx_ref[...] = _quant_rows(_load_rows(xbuf.at[i % 2], rows, d, ROW_PITCH))
            acc_ref[...] = jnp.zeros_like(acc_ref)

    @pl.when(i < n_used)
    def _():
        nxt = jnp.minimum(i + 1, n_used - 1)
        per_step = rows // (D_FF // wd_ref.shape[1])
        gbase = (nxt % IDX_RING) * 2 * rows + j * per_step
        xnext = xbuf.at[(i + 1) % 2]
        for k in range(per_step):
            src = pl.multiple_of(idx_smem[gbase + k], w)
            pltpu.make_async_copy(xn_hbm.at[pl.ds(src, w)], xnext.at[pl.ds((j * per_step + k) * ROW_PITCH, w)],
                                  sem_g.at[(i + 1) % 2]).start()

        n_half = 2
        hr = rows // n_half
        for k in range(n_half):
            rs = slice(k * hr, (k + 1) * hr)
            xb = xb_ref[rs, :]
            sx = sx_ref[rs, :]
            g = jnp.dot(xb, wg_ref[0], preferred_element_type=F32) * sx * sg_ref[0] + bg_ref[0]
            u = jnp.dot(xb, wu_ref[0], preferred_element_type=F32) * sx * su_ref[0] + bu_ref[0]
            gt = jnp.minimum(g, SWIGLU_LIMIT)
            up = jnp.clip(u, -SWIGLU_LIMIT, SWIGLU_LIMIT)
            hq, sh = _quant_rows((up + 1.0) * (gt * jax.nn.sigmoid(SWIGLU_ALPHA * gt)))
            acc_ref[rs, :] += jnp.dot(hq, wd_ref[0], preferred_element_type=F32) * sh * sd_ref[0]

        @pl.when(j == pl.num_programs(1) - 1)
        def _():
            y = (acc_ref[...] + bd_ref[0]) * gate_ref[0]

            @pl.when(i > 0)
            def _():
                scatter_wait()

            _store_rows(ybuf, y, ROW_PITCH)
            islot = i % IDX_RING

            def body(r2, carry):
                for prio in range(2):
                    r = r2 * 2 + prio
                    dst = pl.multiple_of(idx_smem[islot * 2 * rows + rows + r], w)
                    pltpu.make_async_copy(ybuf.at[pl.ds(r * ROW_PITCH, w)], y_hbm.at[pl.ds(dst, w)],
                                          sem_s).start(priority=prio)
                return carry
            lax.fori_loop(0, rows // 2, body, 0, unroll=4)

            @pl.when(i == n_used - 1)
            def _():
                scatter_wait()
                gather_wait((i + 1) % 2)


def _experts(block_expert, n_used, slot_idx, slot_gate, xn, w_gu, s_gu, b_gu, w_down, s_down, b_down):
    nb = slot_idx.shape[0]
    rows = slot_idx.shape[1] // 2
    e, d, ff2 = w_gu.shape
    w = d // LANES
    t = xn.shape[0] // w
    ff = ff2 // 2
    tf = MOE_TF
    nj = ff // tf
    a_rows = t * TOP_K + rows
    grid_spec = pltpu.PrefetchScalarGridSpec(
        num_scalar_prefetch=2,
        grid=(nb, nj),
        in_specs=[
            pl.BlockSpec(memory_space=pl.ANY),
            pl.BlockSpec((1, rows, 1), lambda i, j, be, nu: (i, 0, 0)),
            pl.BlockSpec(memory_space=pl.ANY),
            pl.BlockSpec((1, d, tf), lambda i, j, be, nu: (be[i], 0, j)),
            pl.BlockSpec((1, d, tf), lambda i, j, be, nu: (be[i], 0, j + nj)),
            pl.BlockSpec((1, 1, tf), lambda i, j, be, nu: (be[i], 0, j)),
            pl.BlockSpec((1, 1, tf), lambda i, j, be, nu: (be[i], 0, j + nj)),
            pl.BlockSpec((1, 1, tf), lambda i, j, be, nu: (be[i], 0, j)),
            pl.BlockSpec((1, 1, tf), lambda i, j, be, nu: (be[i], 0, j + nj)),
            pl.BlockSpec((1, tf, d), lambda i, j, be, nu: (be[i], j, 0)),
            pl.BlockSpec((1, 1, d), lambda i, j, be, nu: (be[i], 0, 0)),
            pl.BlockSpec((1, 1, d), lambda i, j, be, nu: (be[i], 0, 0)),
        ],
        out_specs=pl.BlockSpec(memory_space=pl.ANY),
        scratch_shapes=[
            pltpu.SMEM((IDX_RING * 2 * rows,), jnp.int32),
            pltpu.VMEM((2, rows * ROW_PITCH, LANES), F32),
            pltpu.VMEM((rows, d), FP8),
            pltpu.VMEM((rows, 1), F32),
            pltpu.VMEM((rows, d), F32),
            pltpu.VMEM((rows * ROW_PITCH, LANES), F32),
            pltpu.SemaphoreType.DMA((IDX_RING,)),
            pltpu.SemaphoreType.DMA((2,)),
            pltpu.SemaphoreType.DMA,
        ],
    )
    return pl.pallas_call(
        _experts_body,
        out_shape=jax.ShapeDtypeStruct((a_rows * w, LANES), F32),
        grid_spec=grid_spec,
        compiler_params=_cparams(("arbitrary", "arbitrary")),
        name="experts",
    )(block_expert, n_used, slot_idx, slot_gate, xn, w_gu, w_gu, s_gu, s_gu, b_gu, b_gu, w_down, s_down, b_down)


def _combine_body(final_norm, h_ref, y0_ref, y1_ref, y2_ref, y3_ref, g_ref, o_ref):
    n, d = h_ref.shape
    acc = h_ref[...] + _load_rows(y0_ref, n, d) + _load_rows(y1_ref, n, d) + _load_rows(y2_ref, n, d) \
        + _load_rows(y3_ref, n, d)
    o_ref[...] = _rms(acc, g_ref[...]) if final_norm else acc


def _combine(h2, y_tok, gain, tm):
    t, d = h2.shape
    w = d // LANES
    nt = t // tm
    final_norm = gain is not None
    if gain is None:
        gain = jnp.ones((d,), F32)
    gain = gain.reshape(1, d)
    plane = lambda k: pl.BlockSpec((tm * w, LANES), lambda i: (k * nt + i, 0))
    return pl.pallas_call(
        functools.partial(_combine_body, final_norm),
        out_shape=jax.ShapeDtypeStruct((t, d), F32),
        grid=(nt,),
        in_specs=[pl.BlockSpec((tm, d), lambda i: (i, 0))] + [plane(k) for k in range(TOP_K)]
                 + [pl.BlockSpec((1, d), lambda i: (0, 0))],
        out_specs=pl.BlockSpec((tm, d), lambda i: (i, 0)),
        compiler_params=_cparams(("arbitrary",)),
        name="combine",
    )(h2, y_tok, y_tok, y_tok, y_tok, gain)


def _routing(top_idx, gates, rows, row_width):
    k, t = top_idx.shape
    a = t * k
    flat_e = top_idx.reshape(-1)
    _, s_asg, s_gate = lax.sort((flat_e, jnp.arange(a, dtype=jnp.int32), gates.reshape(-1)), num_keys=1)
    experts = jnp.arange(N_EXPERTS, dtype=jnp.int32)
    counts = jnp.sum((flat_e[None, :] == experts[:, None]).astype(jnp.int32), axis=1)
    start = jnp.cumsum(counts) - counts
    padded = (counts + rows - 1) // rows * rows
    padded_end = jnp.cumsum(padded)
    padded_start = padded_end - padded
    nb = -(-a // rows) + N_EXPERTS
    n_used = (padded_end[-1] // rows).astype(jnp.int32)
    blk = jnp.arange(nb, dtype=jnp.int32)
    bexp = jnp.minimum(jnp.sum((padded_end[None, :] <= (blk * rows)[:, None]).astype(jnp.int32), axis=1),
                       N_EXPERTS - 1)
    bexp = jnp.where(blk < n_used, bexp, bexp[jnp.maximum(n_used - 1, 0)])
    off = blk * rows - padded_start[bexp]
    first = jnp.clip(start[bexp] + off, 0, a)
    nvalid = jnp.where(blk < n_used, jnp.clip(counts[bexp] - off, 0, rows), 0)
    r = jnp.arange(rows, dtype=jnp.int32)[None, :]
    pos = jnp.minimum(first[:, None] + r, a - 1)
    asg = s_asg[pos]
    gate = s_gate[pos]
    valid = r < nvalid[:, None]
    slot_src = jnp.where(valid, asg % t, 0)
    slot_dst = jnp.where(valid, asg, a + r)
    slot_gate = jnp.where(valid, gate, 0.0)
    slot_idx = jnp.concatenate([slot_src, slot_dst], axis=1) * row_width
    return bexp, n_used.reshape(1), slot_idx, slot_gate.reshape(nb, rows, 1)


def _moe_stage(h2, xn, top_idx, gates, w_gate_up, b_gate_up, w_down, b_down, norm_final):
    t, d = h2.shape
    bexp, n_used, slot_idx, slot_gate = _routing(top_idx, gates, MOE_ROWS, d // LANES)
    e = w_gate_up.shape[0]
    wq_gu, s_gu = _quant_weights(w_gate_up, MOE_TF)
    wq_d, s_d = _quant_weights(w_down, MOE_TF)
    y_tok = _experts(bexp, n_used, slot_idx, slot_gate, xn, wq_gu, s_gu, b_gate_up.reshape(e, 1, -1),
                     wq_d, s_d, b_down.reshape(e, 1, -1))
    return _combine(h2, y_tok, norm_final, min(256, t))


def _permute_w_in(w_in):
    sp = np.cumsum([0, ATT_WIDTH, KV_LATENT, IDX_WIDTH, IDX_DIM, IDX_HEADS, RET_QK_WIDTH, RET_QK_WIDTH,
                    RET_WIDTH, RET_WIDTH])
    piece = lambda i: w_in[:, int(sp[i]):int(sp[i + 1])]
    d = w_in.shape[0]
    cols = [piece(0), piece(2), piece(7), piece(8), piece(5), piece(6), piece(1), piece(3), piece(4)]
    used = sum(c.shape[1] for c in cols)
    cols.append(jnp.zeros((d, PROJ_WIDTH - used), w_in.dtype))
    return jnp.concatenate(cols, axis=1).astype(BF16)


def _rotary_tables(positions):
    half = RET_QK_DIM // 2
    inv = ROPE_BASE ** (-jnp.arange(0, RET_QK_DIM, 2, dtype=F32) / RET_QK_DIM)
    ang = positions.astype(F32).reshape(-1, 1) * inv
    cos, sin = jnp.cos(ang), jnp.sin(ang)
    return jnp.concatenate([cos, cos], axis=1), jnp.concatenate([-sin, sin], axis=1)


def _front(x, positions, norm_mix, w_in, kv_norm, idx_k_norm):
    b, s, d = x.shape
    t = b * s
    tm = min(1024, t)
    proj = _proj(x.reshape(t, d), norm_mix.reshape(1, d), _permute_w_in(w_in), tm)
    cos2, sin2 = _rotary_tables(positions)
    g_k = jnp.concatenate([idx_k_norm, jnp.zeros((LANES - IDX_DIM,), F32)]).reshape(1, LANES)
    c, kidx, widx, q_rot, k_rot = _prep(proj, cos2, sin2, kv_norm.reshape(1, KV_LATENT), g_k, tm)
    return dict(proj=proj, c=c, kidx=kidx, widx=widx, q_rot=q_rot, k_rot=k_rot)


def _mid_stage(x, mem, y_att, y_ret, w_out, norm_cross, norm_mem, w_cq, w_ck, w_cv, w_co, norm_ffn,
               w_router, b_router):
    b, s, d = x.shape
    km, vm = _memkv(mem, norm_mem.reshape(1, d), w_ck.astype(BF16), w_cv.astype(BF16))
    w_out_b16 = w_out.astype(BF16)
    return _mid(x.reshape(b * s, d), y_att, y_ret, w_out_b16[:ATT_WIDTH], w_out_b16[ATT_WIDTH:],
                norm_cross.reshape(1, d), w_cq.astype(BF16), km, vm, w_co.astype(BF16),
                norm_ffn.reshape(1, d), w_router.T.astype(BF16), b_router.reshape(N_EXPERTS, 1),
                b, s, min(512, s))


def _ret_stage(st, b, s):
    q_rot = st["q_rot"].reshape(b, s, RET_QK_WIDTH)
    kt_rot = jnp.swapaxes(st["k_rot"].reshape(b, s, RET_QK_WIDTH), 1, 2)
    return _retention(q_rot, kt_rot, st["proj"], b, s)


def kernel(x, mem, positions, norm_mix, w_in, kv_norm, idx_k_norm, w_uk, w_uv, rel_bias, w_out, norm_cross,
           norm_mem, w_cq, w_ck, w_cv, w_co, norm_ffn, w_router, b_router, w_gate_up, b_gate_up, w_down, b_down,
           norm_final):
    b, s, d = x.shape
    depth = norm_mix.shape[0]
    h = x
    for l in range(depth):
        st = _front(h, positions, norm_mix[l], w_in[l], kv_norm[l], idx_k_norm[l])
        y_att = _dsa_stage(st, w_uk[l], w_uv[l], rel_bias, b, s)
        y_ret = _ret_stage(st, b, s)
        h2, xn, top_idx, gates = _mid_stage(h, mem, y_att, y_ret, w_out[l], norm_cross[l], norm_mem[l], w_cq[l],
                                            w_ck[l], w_cv[l], w_co[l], norm_ffn[l], w_router[l], b_router[l])
        gain = norm_final if l == depth - 1 else None
        h = _moe_stage(h2, xn, top_idx, gates, w_gate_up[l], b_gate_up[l], w_down[l], b_down[l], gain).reshape(b, s, d)
    return h
```
